```python
import jax, jax.numpy as jnp
from jax import lax
import numpy as np

D_MODEL = 2048
BATCH = 4
SEQ = 2048
DEPTH = 2
DEC_BATCH = 128
DEC_SEQ = 1
PAST_LEN = 16384
PAGE_SIZE = 128

N_MIXERS = 2
N_POOL_LAYERS = (DEPTH + 1) // 2
N_DELTA_LAYERS = DEPTH // 2
POOL_WINDOWS = (2, 4, 8, 16)
N_POOL_GROUPS = len(POOL_WINDOWS)
POOL_GROUP_DIM = D_MODEL // N_POOL_GROUPS
POOL_STATE = max(POOL_WINDOWS) - 1
N_QK_HEADS = 16
N_V_HEADS = 32
HEAD_K = 128
HEAD_V = 128
KEY_DIM = N_QK_HEADS * HEAD_K
VALUE_DIM = N_V_HEADS * HEAD_V
CONV_WIDTH = 4
CONV_DIM = 2 * KEY_DIM + VALUE_DIM
PROJ_DIM = CONV_DIM + VALUE_DIM + 2 * N_V_HEADS
CHUNK = 64
D_FF = 5632
FFN_CONV_WIDTH = 3
EPS = 1e-6

kernel_name = 'hybrid_pool_gdn_convffn_decode_step'


def rmsnorm(x, w):
    xf = x.astype(jnp.float32)
    y = xf * lax.rsqrt(jnp.mean(xf * xf, -1, keepdims=True) + EPS) * w.astype(jnp.float32)
    return y.astype(x.dtype)


def l2norm(x):
    return x * lax.rsqrt(jnp.sum(x * x, -1, keepdims=True) + EPS)


def causal_dwconv(x, past, w):
    width, ch = w.shape
    xp = jnp.concatenate([past.astype(x.dtype), x], axis=1)
    y = lax.conv_general_dilated(xp, w.astype(x.dtype)[:, None, :], window_strides=(1,), padding='VALID',
                                 dimension_numbers=('NWC', 'WIO', 'NWC'), feature_group_count=ch)
    return y, xp[:, -(width - 1):]


def pool_mixer(h, past, n_past, w_group, scale):
    B, T, _ = h.shape
    hp = jnp.concatenate([past.astype(h.dtype), h], axis=1).astype(jnp.float32)
    cs = jnp.concatenate([jnp.zeros((B, 1, D_MODEL), jnp.float32), jnp.cumsum(hp, axis=1)], axis=1)
    pos = jnp.arange(T) + n_past
    means = []
    for gi, w in enumerate(POOL_WINDOWS):
        sl = slice(gi * POOL_GROUP_DIM, (gi + 1) * POOL_GROUP_DIM)
        end = cs[:, POOL_STATE + 1:POOL_STATE + 1 + T, sl]
        start = cs[:, POOL_STATE + 1 - w:POOL_STATE + 1 - w + T, sl]
        cnt = jnp.minimum(pos + 1, w).astype(jnp.float32)[None, :, None]
        means.append((end - start) / cnt)
    mean = jnp.stack(means, axis=2)
    d = mean - hp[:, POOL_STATE:].reshape(B, T, N_POOL_GROUPS, POOL_GROUP_DIM)
    y = jnp.einsum('btgc,gcd->btgd', d, w_group.astype(jnp.float32)).reshape(B, T, D_MODEL)
    y = y * scale.astype(jnp.float32)
    return y.astype(h.dtype), hp[:, -POOL_STATE:].astype(h.dtype)


def chunk_gated_delta(q, k, v, beta, g, s0):
    B, T, H, dk = q.shape
    dv = v.shape[-1]
    C = min(CHUNK, T)
    N = -(-T // C)
    pad = N * C - T

    def prep(a):
        a = jnp.pad(a, [(0, 0), (0, pad)] + [(0, 0)] * (a.ndim - 2))
        a = jnp.moveaxis(a, 2, 1)
        return a.reshape((B, H, N, C) + a.shape[3:])

    q, k, v, beta, g = prep(q), prep(k), prep(v), prep(beta), prep(g)
    gc = jnp.cumsum(g, axis=-1)
    tril = jnp.tril(jnp.ones((C, C), bool))
    strict = jnp.tril(jnp.ones((C, C), bool), -1)
    decay = jnp.exp(jnp.where(tril, gc[..., :, None] - gc[..., None, :], -jnp.inf))
    kb = k * beta[..., None]
    vb = v * beta[..., None]
    lower = jnp.where(strict, jnp.einsum('bhnid,bhnjd->bhnij', kb, k) * decay, 0.0)
    eye = jnp.eye(C, dtype=jnp.float32)
    tinv = lax.linalg.triangular_solve(eye + lower, jnp.broadcast_to(eye, lower.shape), left_side=True,
                                       lower=True, unit_diagonal=True)
    u = jnp.einsum('bhnij,bhnjd->bhnid', tinv, vb)
    wk = jnp.einsum('bhnij,bhnjd->bhnid', tinv, kb * jnp.exp(gc)[..., None])
    qk = jnp.where(tril, jnp.einsum('bhnid,bhnjd->bhnij', q, k) * decay, 0.0)

    def step(s, xs):
        q_i, k_i, u_i, w_i, qk_i, g_i = xs
        v_new = u_i - jnp.einsum('bhcd,bhde->bhce', w_i, s)
        o = jnp.einsum('bhcd,bhde->bhce', q_i * jnp.exp(g_i)[..., None], s) + jnp.einsum('bhij,bhje->bhie', qk_i, v_new)
        g_last = g_i[..., -1]
        s = s * jnp.exp(g_last)[..., None, None] + jnp.einsum(
            'bhcd,bhce->bhde', k_i * jnp.exp(g_last[..., None] - g_i)[..., None], v_new)
        return s, o

    xs = tuple(jnp.moveaxis(a, 2, 0) for a in (q, k, u, wk, qk, gc))
    s, o = lax.scan(step, s0, xs)
    o = jnp.moveaxis(o, 0, 2).reshape(B, H, N * C, dv)[:, :, :T]
    return jnp.moveaxis(o, 1, 2), s


def delta_mixer(h, conv_past, s0, w_in, conv_w, a_log, dt_bias, norm_w, w_out):
    B, T, _ = h.shape
    proj = h @ w_in
    qkv = proj[..., :CONV_DIM]
    z = proj[..., CONV_DIM:CONV_DIM + VALUE_DIM]
    b_raw = proj[..., CONV_DIM + VALUE_DIM:CONV_DIM + VALUE_DIM + N_V_HEADS]
    a_raw = proj[..., CONV_DIM + VALUE_DIM + N_V_HEADS:]
    qkv, new_conv = causal_dwconv(qkv, conv_past, conv_w)
    qkv = jax.nn.silu(qkv.astype(jnp.float32))
    rep = N_V_HEADS // N_QK_HEADS
    q = jnp.repeat(l2norm(qkv[..., :KEY_DIM].reshape(B, T, N_QK_HEADS, HEAD_K)), rep, axis=2) * (HEAD_K ** -0.5)
    k = jnp.repeat(l2norm(qkv[..., KEY_DIM:2 * KEY_DIM].reshape(B, T, N_QK_HEADS, HEAD_K)), rep, axis=2)
    v = qkv[..., 2 * KEY_DIM:].reshape(B, T, N_V_HEADS, HEAD_V)
    beta = jax.nn.sigmoid(b_raw.astype(jnp.float32))
    g = -jnp.exp(a_log.astype(jnp.float32)) * jax.nn.softplus(a_raw.astype(jnp.float32) + dt_bias.astype(jnp.float32))
    o, s_new = chunk_gated_delta(q, k, v, beta, g, s0.astype(jnp.float32))
    o = o * lax.rsqrt(jnp.mean(o * o, -1, keepdims=True) + EPS) * norm_w.astype(jnp.float32)
    o = o * jax.nn.silu(z.astype(jnp.float32).reshape(B, T, N_V_HEADS, HEAD_V))
    y = o.reshape(B, T, VALUE_DIM).astype(h.dtype) @ w_out
    return y, new_conv, s_new


def conv_ffn(h, past, w_gate, w_up, conv_w, w_down):
    gate, new_past = causal_dwconv(h @ w_gate, past, conv_w)
    a = jax.nn.silu(gate) * (h @ w_up)
    return a @ w_down, new_past


def trunk(x, c, n_past, pool_st, conv_st, rec_st, ffn_st, norm_w, ada_w, ada_b, pool_w, pool_scale,
          dn_w_in, dn_conv_w, dn_a_log, dn_dt_bias, dn_norm_w, dn_w_out, ffn_w_gate, ffn_w_up, ffn_conv_w, ffn_w_down):
    new_pool, new_conv, new_rec, new_ffn = [], [], [], []
    cs = jax.nn.silu(c)
    for i in range(DEPTH):
        mod = cs @ ada_w[i] + ada_b[i]
        sh_m, sc_m, gt_m, sh_f, sc_f, gt_f = [m[:, None, :] for m in jnp.split(mod, 6, axis=-1)]
        h = rmsnorm(x, norm_w[i, 0]) * (1 + sc_m) + sh_m
        j = i // N_MIXERS
        if i % N_MIXERS == 0:
            y, st = pool_mixer(h, pool_st[j], n_past, pool_w[j], pool_scale[j])
            new_pool.append(st)
        else:
            y, cst, s = delta_mixer(h, conv_st[j], rec_st[j], dn_w_in[j], dn_conv_w[j], dn_a_log[j],
                                    dn_dt_bias[j], dn_norm_w[j], dn_w_out[j])
            new_conv.append(cst)
            new_rec.append(s)
        x = x + gt_m * rmsnorm(y, norm_w[i, 1])
        h = rmsnorm(x, norm_w[i, 2]) * (1 + sc_f) + sh_f
        y, fst = conv_ffn(h, ffn_st[i], ffn_w_gate[i], ffn_w_up[i], ffn_conv_w[i], ffn_w_down[i])
        new_ffn.append(fst)
        x = x + gt_f * rmsnorm(y, norm_w[i, 3])
    return x, jnp.stack(new_pool), jnp.stack(new_conv), jnp.stack(new_rec), jnp.stack(new_ffn)


def setup_inputs(seed: int = 0) -> dict:
    key = jax.random.key(seed)
    ks = jax.random.split(key, 24)

    def nrm(k, shape, s):
        return jax.random.normal(k, shape, jnp.float32) * s

    return {
        'x_prompt': nrm(ks[0], (BATCH, SEQ, D_MODEL), 1.0),
        'x_sample': nrm(ks[1], (DEC_BATCH, DEC_SEQ, D_MODEL), 1.0),
        'c_prompt': nrm(ks[2], (BATCH, D_MODEL), 1.0),
        'c_sample': nrm(ks[3], (DEC_BATCH, D_MODEL), 1.0),
        'cache_pool': nrm(ks[4], (N_POOL_LAYERS, DEC_BATCH, POOL_STATE, D_MODEL), 1.0),
        'state_conv': nrm(ks[5], (N_DELTA_LAYERS, DEC_BATCH, CONV_WIDTH - 1, CONV_DIM), 1.0),
        'state_rec': nrm(ks[6], (N_DELTA_LAYERS, DEC_BATCH, N_V_HEADS, HEAD_K, HEAD_V), HEAD_K ** -0.5),
        'cache_ffn_conv': nrm(ks[7], (DEPTH, DEC_BATCH, FFN_CONV_WIDTH - 1, D_FF), 1.0),
        'norm_w': 1.0 + nrm(ks[8], (DEPTH, 4, D_MODEL), 0.05),
        'ada_w': nrm(ks[9], (DEPTH, D_MODEL, 6 * D_MODEL), D_MODEL ** -0.5),
        'ada_b': nrm(ks[10], (DEPTH, 6 * D_MODEL), 0.01),
        'pool_w': nrm(ks[11], (N_POOL_LAYERS, N_POOL_GROUPS, POOL_GROUP_DIM, POOL_GROUP_DIM), POOL_GROUP_DIM ** -0.5),
        'pool_scale': 1.0 + nrm(ks[12], (N_POOL_LAYERS, D_MODEL), 0.1),
        'dn_w_in': nrm(ks[13], (N_DELTA_LAYERS, D_MODEL, PROJ_DIM), D_MODEL ** -0.5),
        'dn_conv_w': nrm(ks[14], (N_DELTA_LAYERS, CONV_WIDTH, CONV_DIM), CONV_WIDTH ** -0.5),
        'dn_a_log': jnp.log(jax.random.uniform(ks[15], (N_DELTA_LAYERS, N_V_HEADS), jnp.float32, 1.0, 16.0)),
        'dn_dt_bias': nrm(ks[16], (N_DELTA_LAYERS, N_V_HEADS), 0.1),
        'dn_norm_w': 1.0 + nrm(ks[17], (N_DELTA_LAYERS, HEAD_V), 0.05),
        'dn_w_out': nrm(ks[18], (N_DELTA_LAYERS, VALUE_DIM, D_MODEL), VALUE_DIM ** -0.5),
        'ffn_w_gate': nrm(ks[19], (DEPTH, D_MODEL, D_FF), D_MODEL ** -0.5),
        'ffn_w_up': nrm(ks[20], (DEPTH, D_MODEL, D_FF), D_MODEL ** -0.5),
        'ffn_conv_w': nrm(ks[21], (DEPTH, FFN_CONV_WIDTH, D_FF), FFN_CONV_WIDTH ** -0.5),
        'ffn_w_down': nrm(ks[22], (DEPTH, D_FF, D_MODEL), D_FF ** -0.5),
    }


def reference(x_prompt, x_sample, c_prompt, c_sample, cache_pool, state_conv, state_rec, cache_ffn_conv,
              norm_w, ada_w, ada_b, pool_w, pool_scale, dn_w_in, dn_conv_w, dn_a_log, dn_dt_bias, dn_norm_w,
              dn_w_out, ffn_w_gate, ffn_w_up, ffn_conv_w, ffn_w_down):
    bp = x_prompt.shape[0]
    dt = x_prompt.dtype
    pool0 = jnp.zeros((N_POOL_LAYERS, bp, POOL_STATE, D_MODEL), dt)
    conv0 = jnp.zeros((N_DELTA_LAYERS, bp, CONV_WIDTH - 1, CONV_DIM), dt)
    rec0 = jnp.zeros((N_DELTA_LAYERS, bp, N_V_HEADS, HEAD_K, HEAD_V), jnp.float32)
    ffn0 = jnp.zeros((DEPTH, bp, FFN_CONV_WIDTH - 1, D_FF), dt)
    y_prompt, pool_p, conv_p, rec_p, ffn_p = trunk(
        x_prompt, c_prompt, 0, pool0, conv0, rec0, ffn0, norm_w, ada_w, ada_b, pool_w, pool_scale,
        dn_w_in, dn_conv_w, dn_a_log, dn_dt_bias, dn_norm_w, dn_w_out, ffn_w_gate, ffn_w_up, ffn_conv_w, ffn_w_down)
    y_sample, pool_s, conv_s, rec_s, ffn_s = trunk(
        x_sample, c_sample, PAST_LEN, cache_pool, state_conv, state_rec, cache_ffn_conv, norm_w, ada_w, ada_b,
        pool_w, pool_scale, dn_w_in, dn_conv_w, dn_a_log, dn_dt_bias, dn_norm_w, dn_w_out, ffn_w_gate, ffn_w_up,
        ffn_conv_w, ffn_w_down)
    return (y_prompt, y_sample, pool_p, pool_s, conv_p, conv_s, rec_p, rec_s, ffn_p, ffn_s)
```

```python
import functools
import math

import jax
import jax.numpy as jnp
from jax import lax
from jax.experimental import pallas as pl
from jax.experimental.pallas import tpu as pltpu

F32 = jnp.float32
BF16 = jnp.bfloat16
EPS = 1e-6
POOL_WINDOWS = (2, 4, 8, 16)
POOL_STATE = max(POOL_WINDOWS) - 1
HEAD = 128
N_QK_HEADS = 16
N_V_HEADS = 32
DELTA_CHUNK = 128
PAST_LEN = 16384
VMEM_LIMIT = 56 * 1024 * 1024
MOD_ROWS = 136
PROMPT_MOD_BLOCK = 16


def _params(n_axes, vmem=VMEM_LIMIT):
    return pltpu.CompilerParams(dimension_semantics=("arbitrary",) * n_axes, vmem_limit_bytes=vmem)


def _rms(x, w):
    return x * lax.rsqrt(jnp.mean(x * x, axis=-1, keepdims=True) + EPS) * w


def _silu(x):
    return x * jax.nn.sigmoid(x)


def _ada_kernel(c_ref, w_ref, b_ref, o_ref):
    c = c_ref[...]
    o_ref[...] = jnp.dot(_silu(c), w_ref[...], preferred_element_type=F32) + b_ref[...]


def _ada(c_all, ada_w, ada_b):
    depth, d, n = ada_w.shape
    tn = 1024
    return pl.pallas_call(
        _ada_kernel,
        grid=(depth, n // tn),
        in_specs=[pl.BlockSpec((MOD_ROWS, d), lambda l, j: (0, 0)),
                  pl.BlockSpec((None, d, tn), lambda l, j: (l, 0, j)),
                  pl.BlockSpec((None, 1, tn), lambda l, j: (l, 0, j))],
        out_specs=pl.BlockSpec((None, MOD_ROWS, tn), lambda l, j: (l, 0, j)),
        out_shape=jax.ShapeDtypeStruct((depth, MOD_ROWS, n), F32),
        compiler_params=_params(2),
        name="ada_mod",
    )(c_all, ada_w, ada_b.reshape(depth, 1, n))


def _mod_spec_prompt(d, layer, chunk, n_axes):
    idx = (layer, PROMPT_MOD_BLOCK, chunk)
    if n_axes == 1:
        return pl.BlockSpec((None, 8, d), lambda i: idx)
    return pl.BlockSpec((None, 8, d), lambda i, j: idx)


def _mod_spec_sample(d, rows, layer, chunk, n_axes):
    if n_axes == 1:
        return pl.BlockSpec((None, rows, d), lambda i: (layer, i, chunk))
    return pl.BlockSpec((None, rows, d), lambda i, j: (layer, i, chunk))


def _pool_groups(h, past_rows, recip_cnt, pw_ref, ps_ref, gd):
    ys = []
    for gi, w in enumerate(POOL_WINDOWS):
        c0 = gi * gd
        hg = h[:, c0:c0 + gd]
        acc = hg
        for j in range(1, w):
            acc = acc + past_rows(j, c0)
        dm = acc * recip_cnt(w) - hg
        ys.append(jnp.dot(dm, pw_ref[gi], preferred_element_type=F32))
    return jnp.concatenate(ys, axis=1) * ps_ref[...]


def _pool_prompt_kernel(x_ref, shm, scm, gtm, shf, scf, nw_ref, pw_ref, ps_ref,
                        xo_ref, hf_ref, st_ref, hp, *, tm, tiles_per_b, n_past):
    i = pl.program_id(0)
    b = i // tiles_per_b
    t = i % tiles_per_b
    d = x_ref.shape[1]
    gd = d // len(POOL_WINDOWS)

    def row(r):
        return r[pl.ds(b, 1), :]

    x = x_ref[...]
    h = _rms(x, nw_ref[0:1, :]) * (1.0 + row(scm)) + row(shm)

    @pl.when(t == 0)
    def _():
        hp[0:16, :] = jnp.zeros((16, d), F32)

    hp[16:16 + tm, :] = h
    pos = t * tm + lax.broadcasted_iota(jnp.int32, (tm, 1), 0) + n_past

    def past_rows(j, c0):
        return hp[pl.ds(16 - j, tm), c0:c0 + gd]

    def recip_cnt(w):
        return 1.0 / jnp.minimum(pos + 1, w).astype(F32)

    y = _pool_groups(h, past_rows, recip_cnt, pw_ref, ps_ref, gd)
    tail = hp[tm:tm + 16, :]
    st_ref[...] = tail
    hp[0:16, :] = tail
    xn = x + row(gtm) * _rms(y, nw_ref[1:2, :])
    xo_ref[...] = xn
    hf_ref[...] = (_rms(xn, nw_ref[2:3, :]) * (1.0 + row(scf)) + row(shf)).astype(BF16)


def _pool_prompt(x2, mods, layer, nw, pw, ps, seq):
    m, d = x2.shape
    tm = 256
    tiles_per_b = seq // tm
    nb = m // seq
    ms = lambda c: _mod_spec_prompt(d, layer, c, 1)
    return pl.pallas_call(
        functools.partial(_pool_prompt_kernel, tm=tm, tiles_per_b=tiles_per_b, n_past=0),
        grid=(m // tm,),
        in_specs=[pl.BlockSpec((tm, d), lambda i: (i, 0)),
                  ms(0), ms(1), ms(2), ms(3), ms(4),
                  pl.BlockSpec(nw.shape, lambda i: (0, 0)),
                  pl.BlockSpec(pw.shape, lambda i: (0, 0, 0)),
                  pl.BlockSpec((1, d), lambda i: (0, 0))],
        out_specs=[pl.BlockSpec((tm, d), lambda i: (i, 0)),
                   pl.BlockSpec((tm, d), lambda i: (i, 0)),
                   pl.BlockSpec((None, 16, d), lambda i: (i // tiles_per_b, 0, 0))],
        out_shape=[jax.ShapeDtypeStruct((m, d), F32),
                   jax.ShapeDtypeStruct((m, d), BF16),
                   jax.ShapeDtypeStruct((nb, 16, d), F32)],
        scratch_shapes=[pltpu.VMEM((tm + 16, d), F32)],
        compiler_params=_params(1),
        name="pool_prompt",
    )(x2, mods, mods, mods, mods, mods, nw, pw, ps.reshape(1, d))


def _pool_sample_kernel(x_ref, shm, scm, gtm, shf, scf, nw_ref, pw_ref, ps_ref, c_ref,
                        xo_ref, hf_ref, co_ref, *, n_past):
    d = x_ref.shape[1]
    gd = d // len(POOL_WINDOWS)
    x = x_ref[...]
    h = _rms(x, nw_ref[0:1, :]) * (1.0 + scm[...]) + shm[...]

    def past_rows(j, c0):
        return c_ref[POOL_STATE - j, :, c0:c0 + gd]

    def recip_cnt(w):
        return 1.0 / float(min(n_past + 1, w))

    y = _pool_groups(h, past_rows, recip_cnt, pw_ref, ps_ref, gd)
    for r in range(POOL_STATE - 1):
        co_ref[r] = c_ref[r + 1]
    co_ref[POOL_STATE - 1] = h
    xn = x + gtm[...] * _rms(y, nw_ref[1:2, :])
    xo_ref[...] = xn
    hf_ref[...] = (_rms(xn, nw_ref[2:3, :]) * (1.0 + scf[...]) + shf[...]).astype(BF16)


def _pool_sample(x2, mods, layer, nw, pw, ps, cache_t):
    m, d = x2.shape
    bb = 32
    ms = lambda c: _mod_spec_sample(d, bb, layer, c, 1)
    return pl.pallas_call(
        functools.partial(_pool_sample_kernel, n_past=PAST_LEN),
        grid=(m // bb,),
        in_specs=[pl.BlockSpec((bb, d), lambda i: (i, 0)),
                  ms(0), ms(1), ms(2), ms(3), ms(4),
                  pl.BlockSpec(nw.shape, lambda i: (0, 0)),
                  pl.BlockSpec(pw.shape, lambda i: (0, 0, 0)),
                  pl.BlockSpec((1, d), lambda i: (0, 0)),
                  pl.BlockSpec((POOL_STATE, bb, d), lambda i: (0, i, 0))],
        out_specs=[pl.BlockSpec((bb, d), lambda i: (i, 0)),
                   pl.BlockSpec((bb, d), lambda i: (i, 0)),
                   pl.BlockSpec((POOL_STATE, bb, d), lambda i: (0, i, 0))],
        out_shape=[jax.ShapeDtypeStruct((m, d), F32),
                   jax.ShapeDtypeStruct((m, d), BF16),
                   jax.ShapeDtypeStruct((POOL_STATE, m, d), F32)],
        compiler_params=_params(1),
        name="pool_sample",
    )(x2, mods, mods, mods, mods, mods, nw, pw, ps.reshape(1, d), cache_t)


def _ffn_a_prompt_kernel(h_ref, wg_ref, wu_ref, cw_ref, a_ref, st_ref, wgb, wub, gbuf,
                         *, tm, tiles_per_b):
    i = pl.program_id(1)
    t = i % tiles_per_b

    @pl.when(i == 0)
    def _():
        wgb[...] = wg_ref[...].astype(BF16)
        wub[...] = wu_ref[...].astype(BF16)

    h = h_ref[...]
    g = jnp.dot(h, wgb[...], preferred_element_type=F32)
    u = jnp.dot(h, wub[...], preferred_element_type=F32)

    @pl.when(t == 0)
    def _():
        gbuf[0:8, :] = jnp.zeros((8, gbuf.shape[1]), F32)

    gbuf[8:8 + tm, :] = g
    conv = (cw_ref[2:3, :] * g + cw_ref[1:2, :] * gbuf[pl.ds(7, tm), :]
            + cw_ref[0:1, :] * gbuf[pl.ds(6, tm), :])
    tail = gbuf[tm:tm + 8, :]
    st_ref[...] = tail
    gbuf[0:8, :] = tail
    a_ref[...] = (_silu(conv) * u).astype(BF16)


def _ffn_a_prompt(h, wg, wu, cw, seq):
    m, d = h.shape
    f = wg.shape[1]
    tm, tf = 1024, 512
    tiles_per_b = seq // tm
    nb = m // seq
    return pl.pallas_call(
        functools.partial(_ffn_a_prompt_kernel, tm=tm, tiles_per_b=tiles_per_b),
        grid=(f // tf, m // tm),
        in_specs=[pl.BlockSpec((tm, d), lambda j, i: (i, 0)),
                  pl.BlockSpec((d, tf), lambda j, i: (0, j)),
                  pl.BlockSpec((d, tf), lambda j, i: (0, j)),
                  pl.BlockSpec((cw.shape[0], tf), lambda j, i: (0, j))],
        out_specs=[pl.BlockSpec((tm, tf), lambda j, i: (i, j)),
                   pl.BlockSpec((None, 8, tf), lambda j, i: (i // tiles_per_b, 0, j))],
        out_shape=[jax.ShapeDtypeStruct((m, f), BF16),
                   jax.ShapeDtypeStruct((nb, 8, f), F32)],
        scratch_shapes=[pltpu.VMEM((d, tf), BF16), pltpu.VMEM((d, tf), BF16),
                        pltpu.VMEM((tm + 8, tf), F32)],
        compiler_params=_params(2),
        name="ffn_a_prompt",
    )(h, wg, wu, cw)


def _ffn_a_sample_kernel(h_ref, wg_ref, wu_ref, cw_ref, c_ref, a_ref, co_ref):
    h = h_ref[...]
    g = jnp.dot(h, wg_ref[...].astype(BF16), preferred_element_type=F32)
    u = jnp.dot(h, wu_ref[...].astype(BF16), preferred_element_type=F32)
    conv = cw_ref[2:3, :] * g + cw_ref[1:2, :] * c_ref[1] + cw_ref[0:1, :] * c_ref[0]
    co_ref[0] = c_ref[1]
    co_ref[1] = g
    a_ref[...] = (_silu(conv) * u).astype(BF16)


def _ffn_a_sample(h, wg, wu, cw, cache_t):
    m, d = h.shape
    f = wg.shape[1]
    tf = 512
    return pl.pallas_call(
        _ffn_a_sample_kernel,
        grid=(f // tf,),
        in_specs=[pl.BlockSpec((m, d), lambda j: (0, 0)),
                  pl.BlockSpec((d, tf), lambda j: (0, j)),
                  pl.BlockSpec((d, tf), lambda j: (0, j)),
                  pl.BlockSpec((cw.shape[0], tf), lambda j: (0, j)),
                  pl.BlockSpec((2, m, tf), lambda j: (0, 0, j))],
        out_specs=[pl.BlockSpec((m, tf), lambda j: (0, j)),
                   pl.BlockSpec((2, m, tf), lambda j: (0, 0, j))],
        out_shape=[jax.ShapeDtypeStruct((m, f), BF16),
                   jax.ShapeDtypeStruct((2, m, f), F32)],
        compiler_params=_params(1),
        name="ffn_a_sample",
    )(h, wg, wu, cw, cache_t)


def _mm_epi_kernel(*refs, nk, per_row, tiles_per_b, has_next, next_dtype):
    if has_next:
        a_ref, w_ref, x_ref, gt, shn, scn, nwp, nwn, xo_ref, ho_ref, acc = refs
    else:
        a_ref, w_ref, x_ref, gt, nwp, xo_ref, acc = refs
    k = pl.program_id(1)
    b = pl.program_id(0) // tiles_per_b

    def sel(r):
        return r[...] if per_row else r[pl.ds(b, 1), :]

    part = jnp.dot(a_ref[...], w_ref[...].astype(BF16), preferred_element_type=F32)

    @pl.when(k == 0)
    def _():
        acc[...] = part

    @pl.when(k > 0)
    def _():
        acc[...] += part

    @pl.when(k == nk - 1)
    def _():
        xn = x_ref[...] + sel(gt) * _rms(acc[...], nwp[...])
        xo_ref[...] = xn
        if has_next:
            ho_ref[...] = (_rms(xn, nwn[...]) * (1.0 + sel(scn)) + sel(shn)).astype(next_dtype)


def _mm_epi(a, w, x, mods, layer, gate_chunk, nw_post, nxt, *, per_row, seq, name):
    m, kdim = a.shape
    d = w.shape[1]
    tm = m if per_row else 512
    tk = 512
    nk = kdim // tk
    tiles_per_b = 1 if per_row else seq // tm
    if per_row:
        ms = lambda l, c: _mod_spec_sample(d, tm, l, c, 2)
    else:
        ms = lambda l, c: _mod_spec_prompt(d, l, c, 2)
    row_spec = pl.BlockSpec((1, d), lambda i, k: (0, 0))
    in_specs = [pl.BlockSpec((tm, tk), lambda i, k: (i, k)),
                pl.BlockSpec((tk, d), lambda i, k: (k, 0)),
                pl.BlockSpec((tm, d), lambda i, k: (i, 0)),
                ms(layer, gate_chunk)]
    args = [a, w, x, mods]
    out_specs = [pl.BlockSpec((tm, d), lambda i, k: (i, 0))]
    out_shape = [jax.ShapeDtypeStruct((m, d), F32)]
    has_next = nxt is not None
    next_dtype = None
    if has_next:
        nw_next, nlayer, sh_chunk, sc_chunk, next_dtype = nxt
        in_specs += [ms(nlayer, sh_chunk), ms(nlayer, sc_chunk), row_spec, row_spec]
        args += [mods, mods, nw_post.reshape(1, d), nw_next.reshape(1, d)]
        out_specs.append(pl.BlockSpec((tm, d), lambda i, k: (i, 0)))
        out_shape.append(jax.ShapeDtypeStruct((m, d), next_dtype))
    else:
        in_specs += [row_spec]
        args += [nw_post.reshape(1, d)]
    res = pl.pallas_call(
        functools.partial(_mm_epi_kernel, nk=nk, per_row=per_row, tiles_per_b=tiles_per_b,
                          has_next=has_next, next_dtype=next_dtype),
        grid=(m // tm, nk),
        in_specs=in_specs,
        out_specs=out_specs,
        out_shape=out_shape,
        scratch_shapes=[pltpu.VMEM((tm, d), F32)],
        compiler_params=_params(2),
        name=name,
    )(*args)
    return res if has_next else (res[0], None)


def _proj_kernel(h_ref, w_ref, o_ref, wb):
    @pl.when(pl.program_id(1) == 0)
    def _():
        wb[...] = w_ref[...].astype(BF16)

    o_ref[...] = jnp.dot(h_ref[...], wb[...], preferred_element_type=F32)


def _proj(h, w_in, n_out, name):
    m, d = h.shape
    tm = min(m, 1024)
    tn = 512
    return pl.pallas_call(
        _proj_kernel,
        grid=(n_out // tn, m // tm),
        in_specs=[pl.BlockSpec((tm, d), lambda j, i: (i, 0)),
                  pl.BlockSpec((d, tn), lambda j, i: (0, j))],
        out_specs=pl.BlockSpec((tm, tn), lambda j, i: (i, j)),
        out_shape=jax.ShapeDtypeStruct((m, n_out), F32),
        scratch_shapes=[pltpu.VMEM((d, tn), BF16)],
        compiler_params=_params(2),
        name=name,
    )(h, w_in)


def _gates_kernel(h_ref, w_ref, al_ref, dt_ref, beta_ref, gc_ref, *, tm, chunk):
    nh = al_ref.shape[1]
    ba = jnp.dot(h_ref[...], w_ref[...].astype(BF16), preferred_element_type=F32)
    beta_ref[...] = jax.nn.sigmoid(ba[:, :nh])
    xx = ba[:, nh:] + dt_ref[...]
    softplus = jnp.maximum(xx, 0.0) + jnp.log1p(jnp.exp(-jnp.abs(xx)))
    g = -jnp.exp(al_ref[...]) * softplus
    shift = int(math.log2(chunk))
    r = lax.broadcasted_iota(jnp.int32, (tm, tm), 0)
    c = lax.broadcasted_iota(jnp.int32, (tm, tm), 1)
    tri = jnp.where(((r >> shift) == (c >> shift)) & (c <= r), 1.0, 0.0).astype(F32)
    gc_ref[...] = jnp.dot(tri, g, precision=lax.Precision.HIGHEST, preferred_element_type=F32)


def _gates(h, w_ba, a_log, dt_bias, chunk, name):
    m, d = h.shape
    nh = a_log.shape[0]
    tm = min(m, 256)
    return pl.pallas_call(
        functools.partial(_gates_kernel, tm=tm, chunk=chunk),
        grid=(m // tm,),
        in_specs=[pl.BlockSpec((tm, d), lambda i: (i, 0)),
                  pl.BlockSpec(w_ba.shape, lambda i: (0, 0)),
                  pl.BlockSpec((1, nh), lambda i: (0, 0)),
                  pl.BlockSpec((1, nh), lambda i: (0, 0))],
        out_specs=[pl.BlockSpec((tm, nh), lambda i: (i, 0)),
                   pl.BlockSpec((tm, nh), lambda i: (i, 0))],
        out_shape=[jax.ShapeDtypeStruct((m, nh), F32), jax.ShapeDtypeStruct((m, nh), F32)],
        compiler_params=_params(1),
        name=name,
    )(h, w_ba, a_log.reshape(1, nh), dt_bias.reshape(1, nh))


def _qkv_kernel(p_ref, cw_ref, o_ref, st_ref, buf, *, tm, tiles_per_b, l2, scale):
    t = pl.program_id(1) % tiles_per_b
    p = p_ref[...]
    width = p.shape[1]

    @pl.when(t == 0)
    def _():
        buf[0:8, :] = jnp.zeros((8, width), F32)

    buf[8:8 + tm, :] = p
    conv = (cw_ref[3:4, :] * p + cw_ref[2:3, :] * buf[pl.ds(7, tm), :]
            + cw_ref[1:2, :] * buf[pl.ds(6, tm), :] + cw_ref[0:1, :] * buf[pl.ds(5, tm), :])
    tail = buf[tm:tm + 8, :]
    st_ref[...] = tail
    buf[0:8, :] = tail
    act = _silu(conv)
    if l2:
        outs = []
        for hh in range(width // HEAD):
            xs = act[:, hh * HEAD:(hh + 1) * HEAD]
            outs.append(xs * lax.rsqrt(jnp.sum(xs * xs, axis=-1, keepdims=True) + EPS) * scale)
        act = jnp.concatenate(outs, axis=1)
    o_ref[...] = act.astype(BF16)


def _qkv(proj, conv_w, col0, width, seq, l2, scale, name):
    m = proj.shape[0]
    tm, tcw = 512, 512
    tiles_per_b = seq // tm
    nb = m // seq
    cb = col0 // tcw
    return pl.pallas_call(
        functools.partial(_qkv_kernel, tm=tm, tiles_per_b=tiles_per_b, l2=l2, scale=scale),
        grid=(width // tcw, m // tm),
        in_specs=[pl.BlockSpec((tm, tcw), lambda j, i: (i, cb + j)),
                  pl.BlockSpec((conv_w.shape[0], tcw), lambda j, i: (0, cb + j))],
        out_specs=[pl.BlockSpec((tm, tcw), lambda j, i: (i, j)),
                   pl.BlockSpec((None, 8, tcw), lambda j, i: (i // tiles_per_b, 0, j))],
        out_shape=[jax.ShapeDtypeStruct((m, width), BF16),
                   jax.ShapeDtypeStruct((nb, 8, width), F32)],
        scratch_shapes=[pltpu.VMEM((tm + 8, tcw), F32)],
        compiler_params=_params(2),
        name=name,
    )(proj, conv_w)


def _delta_kernel(q_ref, k_ref, v_ref, z_ref, bt_ref, gc_ref, gct_ref, nw_ref,
                  o_ref, so_ref, s_scr, *, tc, hps, chunk):
    hg = pl.program_id(1)
    t = pl.program_id(2)
    nt = pl.num_programs(2)
    nh = bt_ref.shape[1]

    @pl.when(t == 0)
    def _():
        s_scr[...] = jnp.zeros(s_scr.shape, F32)

    lane = lax.broadcasted_iota(jnp.int32, (chunk, nh), 1)
    ri = lax.broadcasted_iota(jnp.int32, (chunk, chunk), 0)
    ci = lax.broadcasted_iota(jnp.int32, (chunk, chunk), 1)
    tril = ci <= ri
    strict = ci < ri
    eye = jnp.where(ci == ri, 1.0, 0.0).astype(F32)
    n_factors = int(math.log2(chunk))

    def body(c, carry):
        r0 = pl.multiple_of(c * chunk, chunk)
        rows = pl.ds(r0, chunk)
        btb = bt_ref[rows, :]
        gcb = gc_ref[rows, :]
        for jq in range(hps // 2):
            qcols = slice(jq * HEAD, (jq + 1) * HEAD)
            qc = q_ref[rows, qcols].astype(F32)
            kc = k_ref[rows, qcols].astype(F32)
            kkqk = lax.dot_general(jnp.concatenate([kc, qc], axis=0), kc,
                                   (((1,), (1,)), ((), ())), preferred_element_type=F32)
            kk = kkqk[:chunk]
            qk = kkqk[chunk:]
            for r in range(2):
                hh = jq * 2 + r
                hidx = hg * hps + hh
                vcols = slice(hh * HEAD, (hh + 1) * HEAD)
                pick = lane == hidx
                bcol = jnp.sum(jnp.where(pick, btb, 0.0), axis=1, keepdims=True)
                gcol = jnp.sum(jnp.where(pick, gcb, 0.0), axis=1, keepdims=True)
                grow = gct_ref[c, pl.ds(hidx, 1), :]
                decay = jnp.exp(jnp.where(tril, gcol - grow, -jnp.inf))
                low = jnp.where(strict, kk * bcol * decay, 0.0)
                tinv = eye - low
                pw = jnp.dot(low, low, preferred_element_type=F32)
                for f in range(n_factors - 1):
                    tinv = tinv + jnp.dot(tinv, pw, preferred_element_type=F32)
                    if f < n_factors - 2:
                        pw = jnp.dot(pw, pw, preferred_element_type=F32)
                ecol = jnp.exp(gcol)
                vc = v_ref[rows, vcols].astype(F32)
                rhs = jnp.concatenate([vc * bcol, kc * (bcol * ecol)], axis=1)
                uw = jnp.dot(tinv, rhs, preferred_element_type=F32)
                s = s_scr[hh]
                wq = jnp.concatenate([uw[:, HEAD:], qc * ecol], axis=0)
                ws_qs = jnp.dot(wq, s, preferred_element_type=F32)
                vnew = uw[:, :HEAD] - ws_qs[:chunk]
                qkm = jnp.where(tril, qk * decay, 0.0)
                o = ws_qs[chunk:] + jnp.dot(qkm, vnew, preferred_element_type=F32)
                glast = gcol[chunk - 1:chunk, :]
                kd = kc * jnp.exp(glast - gcol)
                s_scr[hh] = s * jnp.exp(glast) + lax.dot_general(
                    kd, vnew, (((0,), (0,)), ((), ())), preferred_element_type=F32)
                og = (o * lax.rsqrt(jnp.mean(o * o, axis=-1, keepdims=True) + EPS) * nw_ref[...]
                      * _silu(z_ref[rows, vcols]))
                o_ref[rows, vcols] = og.astype(BF16)
        return carry

    lax.fori_loop(0, tc // chunk, body, 0)

    @pl.when(t == nt - 1)
    def _():
        so_ref[...] = s_scr[...]


def _delta_prompt(q, k, v, proj, z_col0, beta, gc, norm_w, nb, seq):
    m = q.shape[0]
    chunk = DELTA_CHUNK
    tc = 256
    hps = 4
    nhg = N_V_HEADS // hps
    nt = seq // tc
    vw = hps * HEAD
    qw = vw // 2
    zb = z_col0 // vw
    gct = gc.reshape(m // chunk, chunk, N_V_HEADS).transpose(0, 2, 1)
    return pl.pallas_call(
        functools.partial(_delta_kernel, tc=tc, hps=hps, chunk=chunk),
        grid=(nb, nhg, nt),
        in_specs=[pl.BlockSpec((tc, qw), lambda b, g, t: (b * nt + t, g)),
                  pl.BlockSpec((tc, qw), lambda b, g, t: (b * nt + t, g)),
                  pl.BlockSpec((tc, vw), lambda b, g, t: (b * nt + t, g)),
                  pl.BlockSpec((tc, vw), lambda b, g, t: (b * nt + t, zb + g)),
                  pl.BlockSpec((tc, N_V_HEADS), lambda b, g, t: (b * nt + t, 0)),
                  pl.BlockSpec((tc, N_V_HEADS), lambda b, g, t: (b * nt + t, 0)),
                  pl.BlockSpec((tc // chunk, N_V_HEADS, chunk), lambda b, g, t: (b * nt + t, 0, 0)),
                  pl.BlockSpec((1, HEAD), lambda b, g, t: (0, 0))],
        out_specs=[pl.BlockSpec((tc, vw), lambda b, g, t: (b * nt + t, g)),
                   pl.BlockSpec((None, hps, HEAD, HEAD), lambda b, g, t: (b, g, 0, 0))],
        out_shape=[jax.ShapeDtypeStruct((m, N_V_HEADS * HEAD), BF16),
                   jax.ShapeDtypeStruct((nb, N_V_HEADS, HEAD, HEAD), F32)],
        scratch_shapes=[pltpu.VMEM((hps, HEAD, HEAD), F32)],
        compiler_params=_params(3),
        name="delta_prompt",
    )(q, k, v, proj, beta, gc, gct, norm_w.reshape(1, HEAD))


def _delta_sample_kernel(p_ref, st_ref, cw_ref, bt_ref, g_ref, nw_ref, s_ref,
                         o_ref, sto_ref, so_ref, qk_scr):
    cdim = st_ref.shape[1]
    key_dim = N_QK_HEADS * HEAD
    xrow = p_ref[:, 0:cdim]
    conv = (cw_ref[3:4, :] * xrow + cw_ref[2:3, :] * st_ref[2:3, :]
            + cw_ref[1:2, :] * st_ref[1:2, :] + cw_ref[0:1, :] * st_ref[0:1, :])
    sto_ref[0:1, :] = st_ref[1:2, :]
    sto_ref[1:2, :] = st_ref[2:3, :]
    sto_ref[2:3, :] = xrow
    act = _silu(conv)
    qk_scr[...] = jnp.zeros(qk_scr.shape, F32)
    for j in range(2 * N_QK_HEADS):
        xs = act[:, j * HEAD:(j + 1) * HEAD]
        xs = xs * lax.rsqrt(jnp.sum(xs * xs, axis=-1, keepdims=True) + EPS)
        if j < N_QK_HEADS:
            xs = xs * (HEAD ** -0.5)
        qk_scr[j:j + 1, :] = xs
    qkt = qk_scr[...].T
    a_row = jnp.exp(g_ref[...])
    b_row = bt_ref[...]
    rep = N_V_HEADS // N_QK_HEADS
    for h in range(N_V_HEADS):
        j = h // rep
        qcol = qkt[:, j:j + 1]
        kcol = qkt[:, N_QK_HEADS + j:N_QK_HEADS + j + 1]
        a = a_row[:, h:h + 1]
        bt = b_row[:, h:h + 1]
        vrow = act[:, 2 * key_dim + h * HEAD:2 * key_dim + (h + 1) * HEAD]
        s = s_ref[h]
        sk = jnp.sum(s * kcol, axis=0, keepdims=True)
        vnew = bt * (vrow - a * sk)
        s_new = a * s + kcol * vnew
        so_ref[h] = s_new
        o = jnp.sum(s_new * qcol, axis=0, keepdims=True)
        z = p_ref[:, cdim + h * HEAD:cdim + (h + 1) * HEAD]
        og = o * lax.rsqrt(jnp.mean(o * o, axis=-1, keepdims=True) + EPS) * nw_ref[...] * _silu(z)
        o_ref[:, h * HEAD:(h + 1) * HEAD] = og.astype(BF16)


def _delta_sample(proj, conv_state, conv_w, beta, g, norm_w, s0):
    nb, width = proj.shape
    cdim = conv_state.shape[2]
    vdim = N_V_HEADS * HEAD
    return pl.pallas_call(
        _delta_sample_kernel,
        grid=(nb,),
        in_specs=[pl.BlockSpec((None, 1, width), lambda b: (b, 0, 0)),
                  pl.BlockSpec((None, conv_state.shape[1], cdim), lambda b: (b, 0, 0)),
                  pl.BlockSpec(conv_w.shape, lambda b: (0, 0)),
                  pl.BlockSpec((None, 1, N_V_HEADS), lambda b: (b, 0, 0)),
                  pl.BlockSpec((None, 1, N_V_HEADS), lambda b: (b, 0, 0)),
                  pl.BlockSpec((1, HEAD), lambda b: (0, 0)),
                  pl.BlockSpec((None, N_V_HEADS, HEAD, HEAD), lambda b: (b, 0, 0, 0))],
        out_specs=[pl.BlockSpec((None, 1, vdim), lambda b: (b, 0, 0)),
                   pl.BlockSpec((None, conv_state.shape[1], cdim), lambda b: (b, 0, 0)),
                   pl.BlockSpec((None, N_V_HEADS, HEAD, HEAD), lambda b: (b, 0, 0, 0))],
        out_shape=[jax.ShapeDtypeStruct((nb, 1, vdim), BF16),
                   jax.ShapeDtypeStruct(conv_state.shape, F32),
                   jax.ShapeDtypeStruct(s0.shape, F32)],
        scratch_shapes=[pltpu.VMEM((HEAD, HEAD), F32)],
        compiler_params=_params(1),
        name="delta_sample",
    )(proj.reshape(nb, 1, width), conv_state, conv_w, beta.reshape(nb, 1, N_V_HEADS),
      g.reshape(nb, 1, N_V_HEADS), norm_w.reshape(1, HEAD), s0)


def kernel(x_prompt, x_sample, c_prompt, c_sample, cache_pool, state_conv, state_rec, cache_ffn_conv,
           norm_w, ada_w, ada_b, pool_w, pool_scale, dn_w_in, dn_conv_w, dn_a_log, dn_dt_bias, dn_norm_w,
           dn_w_out, ffn_w_gate, ffn_w_up, ffn_conv_w, ffn_w_down):
    bp, seq, d = x_prompt.shape
    bs = x_sample.shape[0]
    key_dim = N_QK_HEADS * HEAD
    val_dim = N_V_HEADS * HEAD
    conv_dim = 2 * key_dim + val_dim
    assert bs == 128 and bp <= 8 and x_sample.shape[1] == 1

    c_all = jnp.concatenate([c_sample, c_prompt, jnp.zeros((MOD_ROWS - bs - bp, d), F32)], axis=0)
    mods = _ada(c_all, ada_w, ada_b)

    xp = x_prompt.reshape(bp * seq, d)
    xs = x_sample.reshape(bs, d)

    xp, hp, pool_p16 = _pool_prompt(xp, mods, 0, norm_w[0], pool_w[0], pool_scale[0], seq)
    xs, hs, pool_st = _pool_sample(xs, mods, 0, norm_w[0], pool_w[0], pool_scale[0],
                                   cache_pool[0].transpose(1, 0, 2))
    pool_p = pool_p16[:, 16 - POOL_STATE:][None]
    pool_s = pool_st.transpose(1, 0, 2)[None]

    ffn_p, ffn_s = [], []

    def ffn(layer, xp, hp, xs, hs, nxt_p, nxt_s):
        ap, stp = _ffn_a_prompt(hp, ffn_w_gate[layer], ffn_w_up[layer], ffn_conv_w[layer], seq)
        ffn_p.append(stp[:, 6:8])
        a_s, sts = _ffn_a_sample(hs, ffn_w_gate[layer], ffn_w_up[layer], ffn_conv_w[layer],
                                 cache_ffn_conv[layer].transpose(1, 0, 2))
        ffn_s.append(sts.transpose(1, 0, 2))
        xp, hp = _mm_epi(ap, ffn_w_down[layer], xp, mods, layer, 5, norm_w[layer, 3], nxt_p,
                         per_row=False, seq=seq, name="ffn_b_prompt")
        xs, hs = _mm_epi(a_s, ffn_w_down[layer], xs, mods, layer, 5, norm_w[layer, 3], nxt_s,
                         per_row=True, seq=1, name="ffn_b_sample")
        return xp, hp, xs, hs

    nxt = (norm_w[1, 0], 1, 0, 1, BF16)
    xp, hp, xs, hs = ffn(0, xp, hp, xs, hs, nxt, nxt)

    w_in = dn_w_in[0]
    w_ba = w_in[:, conv_dim + val_dim:]
    proj_p = _proj(hp, w_in, conv_dim + val_dim, "proj_prompt")
    proj_s = _proj(hs, w_in, conv_dim + val_dim, "proj_sample")
    beta_p, gc_p = _gates(hp, w_ba, dn_a_log[0], dn_dt_bias[0], DELTA_CHUNK, "gates_prompt")
    beta_s, g_s = _gates(hs, w_ba, dn_a_log[0], dn_dt_bias[0], 1, "gates_sample")
    qn, st_q = _qkv(proj_p, dn_conv_w[0], 0, key_dim, seq, True, HEAD ** -0.5, "conv_q")
    kn, st_k = _qkv(proj_p, dn_conv_w[0], key_dim, key_dim, seq, True, 1.0, "conv_k")
    vv, st_v = _qkv(proj_p, dn_conv_w[0], 2 * key_dim, val_dim, seq, False, 1.0, "conv_v")
    conv_p = jnp.concatenate([st_q, st_k, st_v], axis=2)[:, 5:8][None]
    o_p, rec_p = _delta_prompt(qn, kn, vv, proj_p, conv_dim, beta_p, gc_p, dn_norm_w[0], bp, seq)
    o_s, conv_s, rec_s = _delta_sample(proj_s, state_conv[0], dn_conv_w[0], beta_s, g_s,
                                       dn_norm_w[0], state_rec[0])
    nxt = (norm_w[1, 2], 1, 3, 4, BF16)
    xp, hp = _mm_epi(o_p, dn_w_out[0], xp, mods, 1, 2, norm_w[1, 1], nxt,
                     per_row=False, seq=seq, name="out_prompt")
    xs, hs = _mm_epi(o_s.reshape(bs, val_dim), dn_w_out[0], xs, mods, 1, 2, norm_w[1, 1], nxt,
                     per_row=True, seq=1, name="out_sample")
    xp, _, xs, _ = ffn(1, xp, hp, xs, hs, None, None)

    return (xp.reshape(bp, seq, d), xs.reshape(bs, 1, d), pool_p, pool_s, conv_p, conv_s[None],
            rec_p[None], rec_s[None], jnp.stack(ffn_p), jnp.stack(ffn_s))
```

```python
import functools
import math

import jax
import jax.numpy as jnp
from jax import lax
from jax.experimental import pallas as pl
from jax.experimental.pallas import tpu as pltpu

F32 = jnp.float32
BF16 = jnp.bfloat16
EPS = 1e-6
POOL_WINDOWS = (2, 4, 8, 16)
POOL_STATE = max(POOL_WINDOWS) - 1
HEAD = 128
N_QK_HEADS = 16
N_V_HEADS = 32
DELTA_CHUNK = 128
PAST_LEN = 16384
VMEM_LIMIT = 56 * 1024 * 1024
MOD_ROWS = 136
PROMPT_MOD_BLOCK = 16
NT_DIMS = (((1,), (1,)), ((), ()))
TN_DIMS = (((0,), (0,)), ((), ()))


def _params(n_axes, vmem=VMEM_LIMIT):
    return pltpu.CompilerParams(dimension_semantics=("arbitrary",) * n_axes, vmem_limit_bytes=vmem)


def _rms(x, w):
    return x * lax.rsqrt(jnp.mean(x * x, axis=-1, keepdims=True) + EPS) * w


def _silu(x):
    return x * jax.nn.sigmoid(x)


def _dot(a, b):
    return jnp.dot(a, b, preferred_element_type=F32)


def _ada_kernel(c_ref, w_ref, b_ref, o_ref):
    o_ref[...] = _dot(_silu(c_ref[...]), w_ref[...]) + b_ref[...]


def _ada(c_all, ada_w, ada_b):
    depth, d, n = ada_w.shape
    tn = 1024
    return pl.pallas_call(
        _ada_kernel,
        grid=(depth, n // tn),
        in_specs=[pl.BlockSpec((MOD_ROWS, d), lambda l, j: (0, 0)),
                  pl.BlockSpec((None, d, tn), lambda l, j: (l, 0, j)),
                  pl.BlockSpec((None, 1, tn), lambda l, j: (l, 0, j))],
        out_specs=pl.BlockSpec((None, MOD_ROWS, tn), lambda l, j: (l, 0, j)),
        out_shape=jax.ShapeDtypeStruct((depth, MOD_ROWS, n), F32),
        compiler_params=_params(2),
        name="ada_mod",
    )(c_all, ada_w, ada_b.reshape(depth, 1, n))


def _const_spec(block, idx, n_axes):
    if n_axes == 1:
        return pl.BlockSpec(block, lambda i: idx)
    if n_axes == 2:
        return pl.BlockSpec(block, lambda i, j: idx)
    return pl.BlockSpec(block, lambda i, j, k: idx)


def _mod_spec_prompt(d, layer, chunk, n_axes):
    return _const_spec((None, 8, d), (layer, PROMPT_MOD_BLOCK, chunk), n_axes)


def _mod_spec_sample(d, rows, layer, chunk, n_axes):
    if n_axes == 1:
        return pl.BlockSpec((None, rows, d), lambda i: (layer, i, chunk))
    return pl.BlockSpec((None, rows, d), lambda i, j: (layer, i, chunk))


def _pool_groups(h, past_rows, recip_cnt, pw_ref, ps_ref, gd):
    ys = []
    for gi, w in enumerate(POOL_WINDOWS):
        c0 = gi * gd
        hg = h[:, c0:c0 + gd]
        acc = hg
        for j in range(1, w):
            acc = acc + past_rows(j, c0)
        dm = acc * recip_cnt(w) - hg
        ys.append(_dot(dm, pw_ref[gi]))
    return jnp.concatenate(ys, axis=1) * ps_ref[...]


def _pool_prompt_kernel(x_ref, shm, scm, gtm, shf, scf, nw_ref, pw_ref, ps_ref,
                        xo_ref, hf_ref, st_ref, hp, *, tm, tiles_per_b, n_past):
    i = pl.program_id(0)
    b = i // tiles_per_b
    t = i % tiles_per_b
    d = x_ref.shape[1]
    gd = d // len(POOL_WINDOWS)

    def row(r):
        return r[pl.ds(b, 1), :]

    x = x_ref[...]
    h = _rms(x, nw_ref[0:1, :]) * (1.0 + row(scm)) + row(shm)

    @pl.when(t == 0)
    def _():
        hp[0:16, :] = jnp.zeros((16, d), F32)

    hp[16:16 + tm, :] = h
    pos = t * tm + lax.broadcasted_iota(jnp.int32, (tm, 1), 0) + n_past

    def past_rows(j, c0):
        return hp[pl.ds(16 - j, tm), c0:c0 + gd]

    def recip_cnt(w):
        return 1.0 / jnp.minimum(pos + 1, w).astype(F32)

    y = _pool_groups(h, past_rows, recip_cnt, pw_ref, ps_ref, gd)
    tail = hp[tm:tm + 16, :]
    st_ref[...] = tail
    hp[0:16, :] = tail
    xn = x + row(gtm) * _rms(y, nw_ref[1:2, :])
    xo_ref[...] = xn
    hf_ref[...] = (_rms(xn, nw_ref[2:3, :]) * (1.0 + row(scf)) + row(shf)).astype(BF16)


def _pool_prompt(x2, mods, layer, norm_w, pool_w, pool_scale, seq):
    m, d = x2.shape
    j = layer // 2
    tm = 256
    tiles_per_b = seq // tm
    nb = m // seq
    ms = lambda c: _mod_spec_prompt(d, layer, c, 1)
    return pl.pallas_call(
        functools.partial(_pool_prompt_kernel, tm=tm, tiles_per_b=tiles_per_b, n_past=0),
        grid=(m // tm,),
        in_specs=[pl.BlockSpec((tm, d), lambda i: (i, 0)),
                  ms(0), ms(1), ms(2), ms(3), ms(4),
                  _const_spec((None,) + norm_w.shape[1:], (layer, 0, 0), 1),
                  _const_spec((None,) + pool_w.shape[1:], (j, 0, 0, 0), 1),
                  _const_spec((None, 1, d), (j, 0, 0), 1)],
        out_specs=[pl.BlockSpec((tm, d), lambda i: (i, 0)),
                   pl.BlockSpec((tm, d), lambda i: (i, 0)),
                   pl.BlockSpec((None, 16, d), lambda i: (i // tiles_per_b, 0, 0))],
        out_shape=[jax.ShapeDtypeStruct((m, d), F32),
                   jax.ShapeDtypeStruct((m, d), BF16),
                   jax.ShapeDtypeStruct((nb, 16, d), F32)],
        scratch_shapes=[pltpu.VMEM((tm + 16, d), F32)],
        compiler_params=_params(1),
        name="pool_prompt",
    )(x2, mods, mods, mods, mods, mods, norm_w, pool_w, pool_scale.reshape(-1, 1, d))


def _pool_sample_kernel(x_ref, shm, scm, gtm, shf, scf, nw_ref, pw_ref, ps_ref, c_ref,
                        xo_ref, hf_ref, co_ref, *, n_past):
    d = x_ref.shape[1]
    gd = d // len(POOL_WINDOWS)
    x = x_ref[...]
    h = _rms(x, nw_ref[0:1, :]) * (1.0 + scm[...]) + shm[...]

    def past_rows(j, c0):
        return c_ref[POOL_STATE - j, :, c0:c0 + gd]

    def recip_cnt(w):
        return 1.0 / float(min(n_past + 1, w))

    y = _pool_groups(h, past_rows, recip_cnt, pw_ref, ps_ref, gd)
    for r in range(POOL_STATE - 1):
        co_ref[r] = c_ref[r + 1]
    co_ref[POOL_STATE - 1] = h
    xn = x + gtm[...] * _rms(y, nw_ref[1:2, :])
    xo_ref[...] = xn
    hf_ref[...] = (_rms(xn, nw_ref[2:3, :]) * (1.0 + scf[...]) + shf[...]).astype(BF16)


def _pool_sample(x2, mods, layer, norm_w, pool_w, pool_scale, cache_t):
    m, d = x2.shape
    j = layer // 2
    bb = 32
    ms = lambda c: _mod_spec_sample(d, bb, layer, c, 1)
    return pl.pallas_call(
        functools.partial(_pool_sample_kernel, n_past=PAST_LEN),
        grid=(m // bb,),
        in_specs=[pl.BlockSpec((bb, d), lambda i: (i, 0)),
                  ms(0), ms(1), ms(2), ms(3), ms(4),
                  _const_spec((None,) + norm_w.shape[1:], (layer, 0, 0), 1),
                  _const_spec((None,) + pool_w.shape[1:], (j, 0, 0, 0), 1),
                  _const_spec((None, 1, d), (j, 0, 0), 1),
                  pl.BlockSpec((None, POOL_STATE, bb, d), lambda i: (j, 0, i, 0))],
        out_specs=[pl.BlockSpec((bb, d), lambda i: (i, 0)),
                   pl.BlockSpec((bb, d), lambda i: (i, 0)),
                   pl.BlockSpec((POOL_STATE, bb, d), lambda i: (0, i, 0))],
        out_shape=[jax.ShapeDtypeStruct((m, d), F32),
                   jax.ShapeDtypeStruct((m, d), BF16),
                   jax.ShapeDtypeStruct((POOL_STATE, m, d), F32)],
        compiler_params=_params(1),
        name="pool_sample",
    )(x2, mods, mods, mods, mods, mods, norm_w, pool_w, pool_scale.reshape(-1, 1, d), cache_t)


def _ffn_a_prompt_kernel(h_ref, wg_ref, wu_ref, cw_ref, a_ref, st_ref, wgb, wub, gbuf,
                         *, tm, tiles_per_b):
    i = pl.program_id(1)
    t = i % tiles_per_b

    @pl.when(i == 0)
    def _():
        wgb[...] = wg_ref[...].astype(BF16)
        wub[...] = wu_ref[...].astype(BF16)

    h = h_ref[...]
    g = _dot(h, wgb[...])
    u = _dot(h, wub[...])

    @pl.when(t == 0)
    def _():
        gbuf[0:8, :] = jnp.zeros((8, gbuf.shape[1]), F32)

    gbuf[8:8 + tm, :] = g
    conv = (cw_ref[2:3, :] * g + cw_ref[1:2, :] * gbuf[pl.ds(7, tm), :]
            + cw_ref[0:1, :] * gbuf[pl.ds(6, tm), :])
    tail = gbuf[tm:tm + 8, :]
    st_ref[...] = tail
    gbuf[0:8, :] = tail
    a_ref[...] = (_silu(conv) * u).astype(BF16)


def _ffn_a_prompt(h, layer, wg, wu, cw, seq):
    m, d = h.shape
    f = wg.shape[2]
    tm, tf = 1024, 512
    tiles_per_b = seq // tm
    nb = m // seq
    return pl.pallas_call(
        functools.partial(_ffn_a_prompt_kernel, tm=tm, tiles_per_b=tiles_per_b),
        grid=(f // tf, m // tm),
        in_specs=[pl.BlockSpec((tm, d), lambda j, i: (i, 0)),
                  pl.BlockSpec((None, d, tf), lambda j, i: (layer, 0, j)),
                  pl.BlockSpec((None, d, tf), lambda j, i: (layer, 0, j)),
                  pl.BlockSpec((None, cw.shape[1], tf), lambda j, i: (layer, 0, j))],
        out_specs=[pl.BlockSpec((tm, tf), lambda j, i: (i, j)),
                   pl.BlockSpec((None, 8, tf), lambda j, i: (i // tiles_per_b, 0, j))],
        out_shape=[jax.ShapeDtypeStruct((m, f), BF16),
                   jax.ShapeDtypeStruct((nb, 8, f), F32)],
        scratch_shapes=[pltpu.VMEM((d, tf), BF16), pltpu.VMEM((d, tf), BF16),
                        pltpu.VMEM((tm + 8, tf), F32)],
        compiler_params=_params(2),
        name="ffn_a_prompt",
    )(h, wg, wu, cw)


def _ffn_a_sample_kernel(h_ref, wg_ref, wu_ref, cw_ref, c_ref, a_ref, co_ref):
    h = h_ref[...]
    g = _dot(h, wg_ref[...].astype(BF16))
    u = _dot(h, wu_ref[...].astype(BF16))
    conv = cw_ref[2:3, :] * g + cw_ref[1:2, :] * c_ref[1] + cw_ref[0:1, :] * c_ref[0]
    co_ref[0] = c_ref[1]
    co_ref[1] = g
    a_ref[...] = (_silu(conv) * u).astype(BF16)


def _ffn_a_sample(h, layer, wg, wu, cw, cache_t):
    m, d = h.shape
    f = wg.shape[2]
    tf = 512
    return pl.pallas_call(
        _ffn_a_sample_kernel,
        grid=(f // tf,),
        in_specs=[pl.BlockSpec((m, d), lambda j: (0, 0)),
                  pl.BlockSpec((None, d, tf), lambda j: (layer, 0, j)),
                  pl.BlockSpec((None, d, tf), lambda j: (layer, 0, j)),
                  pl.BlockSpec((None, cw.shape[1], tf), lambda j: (layer, 0, j)),
                  pl.BlockSpec((None, 2, m, tf), lambda j: (layer, 0, 0, j))],
        out_specs=[pl.BlockSpec((m, tf), lambda j: (0, j)),
                   pl.BlockSpec((2, m, tf), lambda j: (0, 0, j))],
        out_shape=[jax.ShapeDtypeStruct((m, f), BF16),
                   jax.ShapeDtypeStruct((2, m, f), F32)],
        compiler_params=_params(1),
        name="ffn_a_sample",
    )(h, wg, wu, cw, cache_t)


def _mm_epi_kernel(*refs, nk, per_row, tiles_per_b, post_idx, next_idx, next_dtype, slab):
    has_next = next_idx is not None
    if has_next:
        a_ref, w_ref, x_ref, gt, nwp, shn, scn, nwn, xo_ref, ho_ref = refs
    else:
        a_ref, w_ref, x_ref, gt, nwp, xo_ref = refs
    k = pl.program_id(1)
    b = pl.program_id(0) // tiles_per_b
    tm, d = xo_ref.shape
    half = d // 2
    a = a_ref[...]

    for n in range(2):
        cols = slice(n * half, (n + 1) * half)
        part = _dot(a, w_ref[:, cols].astype(BF16))

        @pl.when(k == 0)
        def _():
            xo_ref[:, cols] = part

        @pl.when(k > 0)
        def _():
            xo_ref[:, cols] += part

    @pl.when(k == nk - 1)
    def _():
        def sel(r, rows):
            return r[rows, :] if per_row else r[pl.ds(b, 1), :]

        def body(s, carry):
            rows = pl.ds(pl.multiple_of(s * slab, slab), slab)
            xn = x_ref[rows, :] + sel(gt, rows) * _rms(xo_ref[rows, :], nwp[post_idx:post_idx + 1, :])
            xo_ref[rows, :] = xn
            if has_next:
                ho_ref[rows, :] = (_rms(xn, nwn[next_idx:next_idx + 1, :]) * (1.0 + sel(scn, rows))
                                   + sel(shn, rows)).astype(next_dtype)
            return carry

        lax.fori_loop(0, tm // slab, body, 0)


def _mm_epi(a, w, wl, x, mods, norm_w, layer, gate_chunk, post_idx, nxt, *, per_row, seq, name):
    m, kdim = a.shape
    d = w.shape[2]
    tm = m if per_row else 1024
    tk = 512
    nk = kdim // tk
    tiles_per_b = 1 if per_row else seq // tm
    if per_row:
        ms = lambda l, c: _mod_spec_sample(d, tm, l, c, 2)
    else:
        ms = lambda l, c: _mod_spec_prompt(d, l, c, 2)
    nw_spec = lambda l: _const_spec((None,) + norm_w.shape[1:], (l, 0, 0), 2)
    once = pl.Buffered(1)
    in_specs = [pl.BlockSpec((tm, tk), lambda i, k: (i, k)),
                pl.BlockSpec((None, tk, d), lambda i, k: (wl, k, 0)),
                pl.BlockSpec((tm, d), lambda i, k: (i, 0), pipeline_mode=once),
                ms(layer, gate_chunk), nw_spec(layer)]
    args = [a, w, x, mods, norm_w]
    out_specs = [pl.BlockSpec((tm, d), lambda i, k: (i, 0))]
    out_shape = [jax.ShapeDtypeStruct((m, d), F32)]
    next_idx = next_dtype = None
    if nxt is not None:
        nlayer, next_idx, sh_chunk, sc_chunk, next_dtype = nxt
        in_specs += [ms(nlayer, sh_chunk), ms(nlayer, sc_chunk), nw_spec(nlayer)]
        args += [mods, mods, norm_w]
        out_specs.append(pl.BlockSpec((tm, d), lambda i, k: (i, 0)))
        out_shape.append(jax.ShapeDtypeStruct((m, d), next_dtype))
    res = pl.pallas_call(
        functools.partial(_mm_epi_kernel, nk=nk, per_row=per_row, tiles_per_b=tiles_per_b,
                          post_idx=post_idx, next_idx=next_idx, next_dtype=next_dtype,
                          slab=min(tm, 256)),
        grid=(m // tm, nk),
        in_specs=in_specs,
        out_specs=out_specs,
        out_shape=out_shape,
        compiler_params=_params(2, 60 * 1024 * 1024),
        name=name,
    )(*args)
    return res if nxt is not None else (res[0], None)


def _proj_kernel(h_ref, w_ref, o_ref, wb):
    @pl.when(pl.program_id(1) == 0)
    def _():
        wb[...] = w_ref[...].astype(BF16)

    o_ref[...] = _dot(h_ref[...], wb[...])


def _proj(h, w_in, n_out, name):
    m, d = h.shape
    tm = min(m, 1024)
    tn = 512
    return pl.pallas_call(
        _proj_kernel,
        grid=(n_out // tn, m // tm),
        in_specs=[pl.BlockSpec((tm, d), lambda j, i: (i, 0)),
                  pl.BlockSpec((None, d, tn), lambda j, i: (0, 0, j))],
        out_specs=pl.BlockSpec((tm, tn), lambda j, i: (i, j)),
        out_shape=jax.ShapeDtypeStruct((m, n_out), F32),
        scratch_shapes=[pltpu.VMEM((d, tn), BF16)],
        compiler_params=_params(2),
        name=name,
    )(h, w_in)


def _gates_kernel(h_ref, w_ref, al_ref, dt_ref, beta_ref, gc_ref, *, tm, chunk):
    nh = al_ref.shape[1]
    ba = _dot(h_ref[...], w_ref[:, 0:2 * nh].astype(BF16))
    beta_ref[...] = jax.nn.sigmoid(ba[:, :nh])
    xx = ba[:, nh:] + dt_ref[...]
    softplus = jnp.maximum(xx, 0.0) + jnp.log1p(jnp.exp(-jnp.abs(xx)))
    g = -jnp.exp(al_ref[...]) * softplus
    shift = int(math.log2(chunk))
    r = lax.broadcasted_iota(jnp.int32, (tm, tm), 0)
    c = lax.broadcasted_iota(jnp.int32, (tm, tm), 1)
    tri = jnp.where(((r >> shift) == (c >> shift)) & (c <= r), 1.0, 0.0).astype(F32)
    gc_ref[...] = jnp.dot(tri, g, precision=lax.Precision.HIGHEST, preferred_element_type=F32)


def _gates(h, w_in, col0, a_log, dt_bias, chunk, name):
    m, d = h.shape
    nh = a_log.shape[1]
    tm = min(m, 256)
    assert col0 % 128 == 0 and w_in.shape[2] - col0 == 2 * nh
    return pl.pallas_call(
        functools.partial(_gates_kernel, tm=tm, chunk=chunk),
        grid=(m // tm,),
        in_specs=[pl.BlockSpec((tm, d), lambda i: (i, 0)),
                  pl.BlockSpec((None, d, 128), lambda i: (0, 0, col0 // 128)),
                  pl.BlockSpec((None, 1, nh), lambda i: (0, 0, 0)),
                  pl.BlockSpec((None, 1, nh), lambda i: (0, 0, 0))],
        out_specs=[pl.BlockSpec((tm, nh), lambda i: (i, 0)),
                   pl.BlockSpec((tm, nh), lambda i: (i, 0))],
        out_shape=[jax.ShapeDtypeStruct((m, nh), F32), jax.ShapeDtypeStruct((m, nh), F32)],
        compiler_params=_params(1),
        name=name,
    )(h, w_in, a_log.reshape(-1, 1, nh), dt_bias.reshape(-1, 1, nh))


def _qkv_kernel(p_ref, cw_ref, o_ref, st_ref, buf, *, tm, tiles_per_b, l2, scale):
    t = pl.program_id(1) % tiles_per_b
    p = p_ref[...]
    width = p.shape[1]

    @pl.when(t == 0)
    def _():
        buf[0:8, :] = jnp.zeros((8, width), F32)

    buf[8:8 + tm, :] = p
    conv = (cw_ref[3:4, :] * p + cw_ref[2:3, :] * buf[pl.ds(7, tm), :]
            + cw_ref[1:2, :] * buf[pl.ds(6, tm), :] + cw_ref[0:1, :] * buf[pl.ds(5, tm), :])
    tail = buf[tm:tm + 8, :]
    st_ref[...] = tail
    buf[0:8, :] = tail
    act = _silu(conv)
    if l2:
        outs = []
        for hh in range(width // HEAD):
            xs = act[:, hh * HEAD:(hh + 1) * HEAD]
            outs.append(xs * lax.rsqrt(jnp.sum(xs * xs, axis=-1, keepdims=True) + EPS) * scale)
        act = jnp.concatenate(outs, axis=1)
    o_ref[...] = act.astype(BF16)


def _qkv(proj, conv_w, col0, width, seq, l2, scale, name):
    m = proj.shape[0]
    tm, tcw = 512, 512
    tiles_per_b = seq // tm
    nb = m // seq
    cb = col0 // tcw
    return pl.pallas_call(
        functools.partial(_qkv_kernel, tm=tm, tiles_per_b=tiles_per_b, l2=l2, scale=scale),
        grid=(width // tcw, m // tm),
        in_specs=[pl.BlockSpec((tm, tcw), lambda j, i: (i, cb + j)),
                  pl.BlockSpec((None, conv_w.shape[1], tcw), lambda j, i: (0, 0, cb + j))],
        out_specs=[pl.BlockSpec((tm, tcw), lambda j, i: (i, j)),
                   pl.BlockSpec((None, 8, tcw), lambda j, i: (i // tiles_per_b, 0, j))],
        out_shape=[jax.ShapeDtypeStruct((m, width), BF16),
                   jax.ShapeDtypeStruct((nb, 8, width), F32)],
        scratch_shapes=[pltpu.VMEM((tm + 8, tcw), F32)],
        compiler_params=_params(2),
        name=name,
    )(proj, conv_w)


def _delta_kernel(q_ref, k_ref, v_ref, z_ref, bt_ref, gc_ref, gct_ref, nw_ref,
                  o_ref, so_ref, s_scr, *, tc, hps, chunk):
    hg = pl.program_id(1)
    t = pl.program_id(2)
    nt = pl.num_programs(2)
    nh = bt_ref.shape[1]
    nck = tc // chunk
    n_factors = int(math.log2(chunk))

    @pl.when(t == 0)
    def _():
        s_scr[...] = jnp.zeros(s_scr.shape, F32)

    lane = lax.broadcasted_iota(jnp.int32, (chunk, nh), 1)
    ri = lax.broadcasted_iota(jnp.int32, (chunk, chunk), 0)
    ci = lax.broadcasted_iota(jnp.int32, (chunk, chunk), 1)
    tril = ci <= ri
    strict = ci < ri
    ri2 = lax.broadcasted_iota(jnp.int32, (chunk, 2 * chunk), 0)
    ci2 = lax.broadcasted_iota(jnp.int32, (chunk, 2 * chunk), 1)
    eye2 = jnp.where((ci2 == ri2) | (ci2 == ri2 + chunk), 1.0, 0.0).astype(F32)
    left = ci2 < chunk

    def blockdiag(xp):
        return jnp.concatenate([jnp.where(left, xp, 0.0), jnp.where(left, 0.0, xp)], axis=0)

    pairs = [(c, jq) for c in range(nck) for jq in range(hps // 2)]
    heads = [(c, hh) for c in range(nck) for hh in range(hps)]
    kc, qc, lowp, qkm, bcol, gcol = {}, {}, {}, {}, {}, {}
    for c, jq in pairs:
        rows = slice(c * chunk, (c + 1) * chunk)
        qcols = slice(jq * HEAD, (jq + 1) * HEAD)
        qc[c, jq] = q_ref[rows, qcols].astype(F32)
        kc[c, jq] = k_ref[rows, qcols].astype(F32)
        kkqk = lax.dot_general(jnp.concatenate([kc[c, jq], qc[c, jq]], axis=0), kc[c, jq],
                               NT_DIMS, preferred_element_type=F32)
        btb = bt_ref[rows, :]
        gcb = gc_ref[rows, :]
        lows = []
        for r in range(2):
            hh = 2 * jq + r
            hidx = hg * hps + hh
            pick = lane == hidx
            bcol[c, hh] = jnp.sum(jnp.where(pick, btb, 0.0), axis=1, keepdims=True)
            gcol[c, hh] = jnp.sum(jnp.where(pick, gcb, 0.0), axis=1, keepdims=True)
            grow = gct_ref[c, pl.ds(hidx, 1), :]
            decay = jnp.exp(jnp.where(tril, gcol[c, hh] - grow, -jnp.inf))
            lows.append(jnp.where(strict, kkqk[:chunk] * bcol[c, hh] * decay, 0.0))
            qkm[c, hh] = jnp.where(tril, kkqk[chunk:] * decay, 0.0)
        lowp[c, jq] = jnp.concatenate(lows, axis=1)

    tinv = {p: eye2 - lowp[p] for p in pairs}
    pw = {p: _dot(lowp[p], blockdiag(lowp[p])) for p in pairs}
    for f in range(n_factors - 1):
        for p in pairs:
            bd = blockdiag(pw[p])
            if f == n_factors - 2:
                tinv[p] = tinv[p] + _dot(tinv[p], bd)
            else:
                both = _dot(jnp.concatenate([tinv[p], pw[p]], axis=0), bd)
                tinv[p] = tinv[p] + both[:chunk]
                pw[p] = both[chunk:]

    uw, ecol = {}, {}
    for c, hh in heads:
        rows = slice(c * chunk, (c + 1) * chunk)
        vcols = slice(hh * HEAD, (hh + 1) * HEAD)
        ecol[c, hh] = jnp.exp(gcol[c, hh])
        vc = v_ref[rows, vcols].astype(F32)
        rhs = jnp.concatenate([vc * bcol[c, hh], kc[c, hh // 2] * (bcol[c, hh] * ecol[c, hh])], axis=1)
        r = hh % 2
        uw[c, hh] = _dot(tinv[c, hh // 2][:, r * chunk:(r + 1) * chunk], rhs)

    s = [s_scr[hh] for hh in range(hps)]
    for c in range(nck):
        rows = slice(c * chunk, (c + 1) * chunk)
        ws_qs = [_dot(jnp.concatenate([uw[c, hh][:, HEAD:], qc[c, hh // 2] * ecol[c, hh]], axis=0), s[hh])
                 for hh in range(hps)]
        vnew = [uw[c, hh][:, :HEAD] - ws_qs[hh][:chunk] for hh in range(hps)]
        outs = [ws_qs[hh][chunk:] + _dot(qkm[c, hh], vnew[hh]) for hh in range(hps)]
        for hh in range(hps):
            glast = gcol[c, hh][chunk - 1:chunk, :]
            kd = kc[c, hh // 2] * jnp.exp(glast - gcol[c, hh])
            s[hh] = s[hh] * jnp.exp(glast) + lax.dot_general(kd, vnew[hh], TN_DIMS,
                                                             preferred_element_type=F32)
        for hh in range(hps):
            vcols = slice(hh * HEAD, (hh + 1) * HEAD)
            o = outs[hh]
            og = (o * lax.rsqrt(jnp.mean(o * o, axis=-1, keepdims=True) + EPS) * nw_ref[...]
                  * _silu(z_ref[rows, vcols]))
            o_ref[rows, vcols] = og.astype(BF16)
    for hh in range(hps):
        s_scr[hh] = s[hh]

    @pl.when(t == nt - 1)
    def _():
        so_ref[...] = s_scr[...]


def _delta_prompt(q, k, v, proj, z_col0, beta, gc, norm_w, nb, seq):
    m = q.shape[0]
    chunk = DELTA_CHUNK
    tc = 256
    hps = 4
    nhg = N_V_HEADS // hps
    nt = seq // tc
    vw = hps * HEAD
    qw = vw // 2
    zb = z_col0 // vw
    gct = gc.reshape(m // chunk, chunk, N_V_HEADS).transpose(0, 2, 1)
    return pl.pallas_call(
        functools.partial(_delta_kernel, tc=tc, hps=hps, chunk=chunk),
        grid=(nb, nhg, nt),
        in_specs=[pl.BlockSpec((tc, qw), lambda b, g, t: (b * nt + t, g)),
                  pl.BlockSpec((tc, qw), lambda b, g, t: (b * nt + t, g)),
                  pl.BlockSpec((tc, vw), lambda b, g, t: (b * nt + t, g)),
                  pl.BlockSpec((tc, vw), lambda b, g, t: (b * nt + t, zb + g)),
                  pl.BlockSpec((tc, N_V_HEADS), lambda b, g, t: (b * nt + t, 0)),
                  pl.BlockSpec((tc, N_V_HEADS), lambda b, g, t: (b * nt + t, 0)),
                  pl.BlockSpec((tc // chunk, N_V_HEADS, chunk), lambda b, g, t: (b * nt + t, 0, 0)),
                  _const_spec((None, 1, HEAD), (0, 0, 0), 3)],
        out_specs=[pl.BlockSpec((tc, vw), lambda b, g, t: (b * nt + t, g)),
                   pl.BlockSpec((None, None, hps, HEAD, HEAD), lambda b, g, t: (0, b, g, 0, 0))],
        out_shape=[jax.ShapeDtypeStruct((m, N_V_HEADS * HEAD), BF16),
                   jax.ShapeDtypeStruct((1, nb, N_V_HEADS, HEAD, HEAD), F32)],
        scratch_shapes=[pltpu.VMEM((hps, HEAD, HEAD), F32)],
        compiler_params=_params(3),
        name="delta_prompt",
    )(q, k, v, proj, beta, gc, gct, norm_w.reshape(-1, 1, HEAD))


def _delta_sample_kernel(p_ref, st_ref, cw_ref, bt_ref, g_ref, nw_ref, s_ref,
                         o_ref, sto_ref, so_ref, qk_scr):
    cdim = st_ref.shape[1]
    key_dim = N_QK_HEADS * HEAD
    rep = N_V_HEADS // N_QK_HEADS
    xrow = p_ref[:, 0:cdim]
    conv = (cw_ref[3:4, :] * xrow + cw_ref[2:3, :] * st_ref[2:3, :]
            + cw_ref[1:2, :] * st_ref[1:2, :] + cw_ref[0:1, :] * st_ref[0:1, :])
    sto_ref[0:1, :] = st_ref[1:2, :]
    sto_ref[1:2, :] = st_ref[2:3, :]
    sto_ref[2:3, :] = xrow
    act = _silu(conv)
    qk_scr[...] = jnp.zeros(qk_scr.shape, F32)
    for j in range(2 * N_QK_HEADS):
        xs = act[:, j * HEAD:(j + 1) * HEAD]
        xs = xs * lax.rsqrt(jnp.sum(xs * xs, axis=-1, keepdims=True) + EPS)
        if j < N_QK_HEADS:
            xs = xs * (HEAD ** -0.5)
        qk_scr[j:j + 1, :] = xs
    qkt = qk_scr[...].T
    a_row = jnp.exp(g_ref[...])
    b_row = bt_ref[...]
    hs = range(N_V_HEADS)
    kcols = [qkt[:, N_QK_HEADS + h // rep:N_QK_HEADS + h // rep + 1] for h in hs]
    sk = [jnp.sum(s_ref[h] * kcols[h], axis=0, keepdims=True) for h in hs]
    vnew = [b_row[:, h:h + 1] * (act[:, 2 * key_dim + h * HEAD:2 * key_dim + (h + 1) * HEAD]
                                  - a_row[:, h:h + 1] * sk[h]) for h in hs]
    outs = []
    for h in hs:
        s_new = a_row[:, h:h + 1] * s_ref[h] + kcols[h] * vnew[h]
        so_ref[h] = s_new
        outs.append(jnp.sum(s_new * qkt[:, h // rep:h // rep + 1], axis=0, keepdims=True))
    for h in hs:
        o = outs[h]
        z = p_ref[:, cdim + h * HEAD:cdim + (h + 1) * HEAD]
        og = o * lax.rsqrt(jnp.mean(o * o, axis=-1, keepdims=True) + EPS) * nw_ref[...] * _silu(z)
        o_ref[:, h * HEAD:(h + 1) * HEAD] = og.astype(BF16)


def _delta_sample(proj, conv_state, conv_w, beta, g, norm_w, s0):
    nb, width = proj.shape
    npast, cdim = conv_state.shape[2:]
    vdim = N_V_HEADS * HEAD
    return pl.pallas_call(
        _delta_sample_kernel,
        grid=(nb,),
        in_specs=[pl.BlockSpec((None, 1, width), lambda b: (b, 0, 0)),
                  pl.BlockSpec((None, None, npast, cdim), lambda b: (0, b, 0, 0)),
                  _const_spec((None,) + conv_w.shape[1:], (0, 0, 0), 1),
                  pl.BlockSpec((None, 1, N_V_HEADS), lambda b: (b, 0, 0)),
                  pl.BlockSpec((None, 1, N_V_HEADS), lambda b: (b, 0, 0)),
                  _const_spec((None, 1, HEAD), (0, 0, 0), 1),
                  pl.BlockSpec((None, None, N_V_HEADS, HEAD, HEAD), lambda b: (0, b, 0, 0, 0))],
        out_specs=[pl.BlockSpec((None, 1, vdim), lambda b: (b, 0, 0)),
                   pl.BlockSpec((None, None, npast, cdim), lambda b: (0, b, 0, 0)),
                   pl.BlockSpec((None, None, N_V_HEADS, HEAD, HEAD), lambda b: (0, b, 0, 0, 0))],
        out_shape=[jax.ShapeDtypeStruct((nb, 1, vdim), BF16),
                   jax.ShapeDtypeStruct((1,) + conv_state.shape[1:], F32),
                   jax.ShapeDtypeStruct((1,) + s0.shape[1:], F32)],
        scratch_shapes=[pltpu.VMEM((HEAD, HEAD), F32)],
        compiler_params=_params(1),
        name="delta_sample",
    )(proj.reshape(nb, 1, width), conv_state, conv_w, beta.reshape(nb, 1, N_V_HEADS),
      g.reshape(nb, 1, N_V_HEADS), norm_w.reshape(-1, 1, HEAD), s0)


def kernel(x_prompt, x_sample, c_prompt, c_sample, cache_pool, state_conv, state_rec, cache_ffn_conv,
           norm_w, ada_w, ada_b, pool_w, pool_scale, dn_w_in, dn_conv_w, dn_a_log, dn_dt_bias, dn_norm_w,
           dn_w_out, ffn_w_gate, ffn_w_up, ffn_conv_w, ffn_w_down):
    bp, seq, d = x_prompt.shape
    bs = x_sample.shape[0]
    key_dim = N_QK_HEADS * HEAD
    val_dim = N_V_HEADS * HEAD
    conv_dim = 2 * key_dim + val_dim
    assert bs == 128 and bp <= 8 and x_sample.shape[1] == 1
    assert ada_w.shape[0] == 2 and dn_w_in.shape[0] == 1 and pool_w.shape[0] == 1

    c_all = jnp.concatenate([c_sample, c_prompt, jnp.zeros((MOD_ROWS - bs - bp, d), F32)], axis=0)
    mods = _ada(c_all, ada_w, ada_b)

    xp = x_prompt.reshape(bp * seq, d)
    xs = x_sample.reshape(bs, d)

    xp, hp, pool_p16 = _pool_prompt(xp, mods, 0, norm_w, pool_w, pool_scale, seq)
    xs, hs, pool_st = _pool_sample(xs, mods, 0, norm_w, pool_w, pool_scale,
                                   cache_pool.transpose(0, 2, 1, 3))
    pool_p = pool_p16[:, 16 - POOL_STATE:][None]
    pool_s = pool_st.transpose(1, 0, 2)[None]

    ffn_p, ffn_s = [], []
    ffn_cache_t = cache_ffn_conv.transpose(0, 2, 1, 3)

    def ffn(layer, xp, hp, xs, hs, nxt):
        ap, stp = _ffn_a_prompt(hp, layer, ffn_w_gate, ffn_w_up, ffn_conv_w, seq)
        ffn_p.append(stp[:, 6:8])
        a_s, sts = _ffn_a_sample(hs, layer, ffn_w_gate, ffn_w_up, ffn_conv_w, ffn_cache_t)
        ffn_s.append(sts.transpose(1, 0, 2))
        xp, hp = _mm_epi(ap, ffn_w_down, layer, xp, mods, norm_w, layer, 5, 3, nxt,
                         per_row=False, seq=seq, name="ffn_b_prompt")
        xs, hs = _mm_epi(a_s, ffn_w_down, layer, xs, mods, norm_w, layer, 5, 3, nxt,
                         per_row=True, seq=1, name="ffn_b_sample")
        return xp, hp, xs, hs

    xp, hp, xs, hs = ffn(0, xp, hp, xs, hs, (1, 0, 0, 1, BF16))

    n_main = conv_dim + val_dim
    proj_p = _proj(hp, dn_w_in, n_main, "proj_prompt")
    proj_s = _proj(hs, dn_w_in, n_main, "proj_sample")
    beta_p, gc_p = _gates(hp, dn_w_in, n_main, dn_a_log, dn_dt_bias, DELTA_CHUNK, "gates_prompt")
    beta_s, g_s = _gates(hs, dn_w_in, n_main, dn_a_log, dn_dt_bias, 1, "gates_sample")
    qn, st_q = _qkv(proj_p, dn_conv_w, 0, key_dim, seq, True, HEAD ** -0.5, "conv_q")
    kn, st_k = _qkv(proj_p, dn_conv_w, key_dim, key_dim, seq, True, 1.0, "conv_k")
    vv, st_v = _qkv(proj_p, dn_conv_w, 2 * key_dim, val_dim, seq, False, 1.0, "conv_v")
    conv_p = jnp.concatenate([st_q, st_k, st_v], axis=2)[:, 5:8][None]
    o_p, rec_p = _delta_prompt(qn, kn, vv, proj_p, conv_dim, beta_p, gc_p, dn_norm_w, bp, seq)
    o_s, conv_s, rec_s = _delta_sample(proj_s, state_conv, dn_conv_w, beta_s, g_s, dn_norm_w, state_rec)
    nxt = (1, 2, 3, 4, BF16)
    xp, hp = _mm_epi(o_p, dn_w_out, 0, xp, mods, norm_w, 1, 2, 1, nxt,
                     per_row=False, seq=seq, name="out_prompt")
    xs, hs = _mm_epi(o_s.reshape(bs, val_dim), dn_w_out, 0, xs, mods, norm_w, 1, 2, 1, nxt,
                     per_row=True, seq=1, name="out_sample")
    xp, _, xs, _ = ffn(1, xp, hp, xs, hs, None)

    return (xp.reshape(bp, seq, d), xs.reshape(bs, 1, d), pool_p, pool_s, conv_p, conv_s,
            rec_p, rec_s, jnp.stack(ffn_p), jnp.stack(ffn_s))
```

```python
import functools
import math

import jax
import jax.numpy as jnp
from jax import lax
from jax.experimental import pallas as pl
from jax.experimental.pallas import tpu as pltpu

F32 = jnp.float32
BF16 = jnp.bfloat16
EPS = 1e-6
POOL_WINDOWS = (2, 4, 8, 16)
POOL_STATE = max(POOL_WINDOWS) - 1
HEAD = 128
N_QK_HEADS = 16
N_V_HEADS = 32
DELTA_CHUNK = 128
PAST_LEN = 16384
VMEM_LIMIT = 56 * 1024 * 1024
MOD_ROWS = 136
PROMPT_MOD_BLOCK = 16
NT_DIMS = (((1,), (1,)), ((), ()))
TN_DIMS = (((0,), (0,)), ((), ()))


def _params(n_axes, vmem=VMEM_LIMIT):
    return pltpu.CompilerParams(dimension_semantics=("arbitrary",) * n_axes, vmem_limit_bytes=vmem)


def _rms(x, w):
    return x * lax.rsqrt(jnp.mean(x * x, axis=-1, keepdims=True) + EPS) * w


def _silu(x):
    return x * jax.nn.sigmoid(x)


def _dot(a, b):
    return jnp.dot(a, b, preferred_element_type=F32)


def _ada_kernel(c_ref, w_ref, b_ref, o_ref):
    o_ref[...] = _dot(_silu(c_ref[...]), w_ref[...]) + b_ref[...]


def _ada(c_all, ada_w, ada_b):
    depth, d, n = ada_w.shape
    tn = 1024
    return pl.pallas_call(
        _ada_kernel,
        grid=(depth, n // tn),
        in_specs=[pl.BlockSpec((MOD_ROWS, d), lambda l, j: (0, 0)),
                  pl.BlockSpec((None, d, tn), lambda l, j: (l, 0, j)),
                  pl.BlockSpec((None, 1, tn), lambda l, j: (l, 0, j))],
        out_specs=pl.BlockSpec((None, MOD_ROWS, tn), lambda l, j: (l, 0, j)),
        out_shape=jax.ShapeDtypeStruct((depth, MOD_ROWS, n), F32),
        compiler_params=_params(2),
        name="ada_mod",
    )(c_all, ada_w, ada_b.reshape(depth, 1, n))


def _const_spec(block, idx, n_axes):
    if n_axes == 1:
        return pl.BlockSpec(block, lambda i: idx)
    if n_axes == 2:
        return pl.BlockSpec(block, lambda i, j: idx)
    return pl.BlockSpec(block, lambda i, j, k: idx)


def _mod_spec_prompt(d, layer, chunk, n_axes):
    return _const_spec((None, 8, d), (layer, PROMPT_MOD_BLOCK, chunk), n_axes)


def _mod_spec_sample(d, rows, layer, chunk, n_axes):
    if n_axes == 1:
        return pl.BlockSpec((None, rows, d), lambda i: (layer, i, chunk))
    return pl.BlockSpec((None, rows, d), lambda i, j: (layer, i, chunk))


def _pool_groups(h, past_rows, recip_cnt, pw_ref, ps_ref, gd):
    ys = []
    for gi, w in enumerate(POOL_WINDOWS):
        c0 = gi * gd
        hg = h[:, c0:c0 + gd]
        acc = hg
        for j in range(1, w):
            acc = acc + past_rows(j, c0)
        dm = acc * recip_cnt(w) - hg
        ys.append(_dot(dm, pw_ref[gi]))
    return jnp.concatenate(ys, axis=1) * ps_ref[...]


def _pool_prompt_kernel(x_ref, shm, scm, gtm, shf, scf, nw_ref, pw_ref, ps_ref,
                        xo_ref, hf_ref, st_ref, hp, *, tm, tiles_per_b, n_past):
    i = pl.program_id(0)
    b = i // tiles_per_b
    t = i % tiles_per_b
    d = x_ref.shape[1]
    gd = d // len(POOL_WINDOWS)

    def row(r):
        return r[pl.ds(b, 1), :]

    x = x_ref[...]
    h = _rms(x, nw_ref[0:1, :]) * (1.0 + row(scm)) + row(shm)

    @pl.when(t == 0)
    def _():
        hp[0:16, :] = jnp.zeros((16, d), F32)

    hp[16:16 + tm, :] = h
    pos = t * tm + lax.broadcasted_iota(jnp.int32, (tm, 1), 0) + n_past

    def past_rows(j, c0):
        return hp[pl.ds(16 - j, tm), c0:c0 + gd]

    def recip_cnt(w):
        return 1.0 / jnp.minimum(pos + 1, w).astype(F32)

    y = _pool_groups(h, past_rows, recip_cnt, pw_ref, ps_ref, gd)
    tail = hp[tm:tm + 16, :]
    st_ref[...] = tail
    hp[0:16, :] = tail
    xn = x + row(gtm) * _rms(y, nw_ref[1:2, :])
    xo_ref[...] = xn
    hf_ref[...] = (_rms(xn, nw_ref[2:3, :]) * (1.0 + row(scf)) + row(shf)).astype(BF16)


def _pool_prompt(x2, mods, layer, norm_w, pool_w, pool_scale, seq):
    m, d = x2.shape
    j = layer // 2
    tm = 256
    tiles_per_b = seq // tm
    nb = m // seq
    ms = lambda c: _mod_spec_prompt(d, layer, c, 1)
    return pl.pallas_call(
        functools.partial(_pool_prompt_kernel, tm=tm, tiles_per_b=tiles_per_b, n_past=0),
        grid=(m // tm,),
        in_specs=[pl.BlockSpec((tm, d), lambda i: (i, 0)),
                  ms(0), ms(1), ms(2), ms(3), ms(4),
                  _const_spec((None,) + norm_w.shape[1:], (layer, 0, 0), 1),
                  _const_spec((None,) + pool_w.shape[1:], (j, 0, 0, 0), 1),
                  _const_spec((None, 1, d), (j, 0, 0), 1)],
        out_specs=[pl.BlockSpec((tm, d), lambda i: (i, 0)),
                   pl.BlockSpec((tm, d), lambda i: (i, 0)),
                   pl.BlockSpec((None, 16, d), lambda i: (i // tiles_per_b, 0, 0))],
        out_shape=[jax.ShapeDtypeStruct((m, d), F32),
                   jax.ShapeDtypeStruct((m, d), BF16),
                   jax.ShapeDtypeStruct((nb, 16, d), F32)],
        scratch_shapes=[pltpu.VMEM((tm + 16, d), F32)],
        compiler_params=_params(1),
        name="pool_prompt",
    )(x2, mods, mods, mods, mods, mods, norm_w, pool_w, pool_scale.reshape(-1, 1, d))


def _pool_sample_kernel(x_ref, shm, scm, gtm, shf, scf, nw_ref, pw_ref, ps_ref, c_ref,
                        xo_ref, hf_ref, co_ref, *, n_past):
    d = x_ref.shape[1]
    gd = d // len(POOL_WINDOWS)
    x = x_ref[...]
    h = _rms(x, nw_ref[0:1, :]) * (1.0 + scm[...]) + shm[...]

    def past_rows(j, c0):
        return c_ref[POOL_STATE - j, :, c0:c0 + gd]

    def recip_cnt(w):
        return 1.0 / float(min(n_past + 1, w))

    y = _pool_groups(h, past_rows, recip_cnt, pw_ref, ps_ref, gd)
    for r in range(POOL_STATE - 1):
        co_ref[r] = c_ref[r + 1]
    co_ref[POOL_STATE - 1] = h
    xn = x + gtm[...] * _rms(y, nw_ref[1:2, :])
    xo_ref[...] = xn
    hf_ref[...] = (_rms(xn, nw_ref[2:3, :]) * (1.0 + scf[...]) + shf[...]).astype(BF16)


def _pool_sample(x2, mods, layer, norm_w, pool_w, pool_scale, cache_t):
    m, d = x2.shape
    j = layer // 2
    bb = 32
    ms = lambda c: _mod_spec_sample(d, bb, layer, c, 1)
    return pl.pallas_call(
        functools.partial(_pool_sample_kernel, n_past=PAST_LEN),
        grid=(m // bb,),
        in_specs=[pl.BlockSpec((bb, d), lambda i: (i, 0)),
                  ms(0), ms(1), ms(2), ms(3), ms(4),
                  _const_spec((None,) + norm_w.shape[1:], (layer, 0, 0), 1),
                  _const_spec((None,) + pool_w.shape[1:], (j, 0, 0, 0), 1),
                  _const_spec((None, 1, d), (j, 0, 0), 1),
                  pl.BlockSpec((None, POOL_STATE, bb, d), lambda i: (j, 0, i, 0))],
        out_specs=[pl.BlockSpec((bb, d), lambda i: (i, 0)),
                   pl.BlockSpec((bb, d), lambda i: (i, 0)),
                   pl.BlockSpec((POOL_STATE, bb, d), lambda i: (0, i, 0))],
        out_shape=[jax.ShapeDtypeStruct((m, d), F32),
                   jax.ShapeDtypeStruct((m, d), BF16),
                   jax.ShapeDtypeStruct((POOL_STATE, m, d), F32)],
        compiler_params=_params(1),
        name="pool_sample",
    )(x2, mods, mods, mods, mods, mods, norm_w, pool_w, pool_scale.reshape(-1, 1, d), cache_t)


def _ffn_a_kernel(h_ref, hs_ref, wg_ref, wu_ref, cw_ref, cs_ref,
                  a_ref, st_ref, as_ref, cso_ref, wgb, wub, gbuf, ubuf, *, tm, tiles_per_b, nsub):
    i = pl.program_id(1)
    t = i % tiles_per_b
    sub = tm // nsub
    cw0, cw1, cw2 = cw_ref[0:1, :], cw_ref[1:2, :], cw_ref[2:3, :]

    @pl.when(i == 0)
    def _():
        wgb[...] = wg_ref[...].astype(BF16)
        wub[...] = wu_ref[...].astype(BF16)
        hs = hs_ref[...]
        gs = _dot(hs, wgb[...])
        conv = cw2 * gs + cw1 * cs_ref[1] + cw0 * cs_ref[0]
        cso_ref[0] = cs_ref[1]
        cso_ref[1] = gs
        as_ref[...] = (_silu(conv) * _dot(hs, wub[...])).astype(BF16)

    @pl.when(t == 0)
    def _():
        gbuf[0:8, :] = jnp.zeros((8, gbuf.shape[1]), F32)

    def matmuls(s):
        rows = slice(s * sub, (s + 1) * sub)
        h = h_ref[rows, :]
        gbuf[8 + s * sub:8 + (s + 1) * sub, :] = _dot(h, wgb[...])
        ubuf[rows, :] = _dot(h, wub[...])

    def epilogue(s):
        r0 = 8 + s * sub
        conv = (cw2 * gbuf[r0:r0 + sub, :] + cw1 * gbuf[pl.ds(r0 - 1, sub), :]
                + cw0 * gbuf[pl.ds(r0 - 2, sub), :])
        a_ref[s * sub:(s + 1) * sub, :] = (_silu(conv) * ubuf[s * sub:(s + 1) * sub, :]).astype(BF16)

    matmuls(0)
    for s in range(nsub):
        if s + 1 < nsub:
            matmuls(s + 1)
        epilogue(s)
    tail = gbuf[tm:tm + 8, :]
    st_ref[...] = tail
    gbuf[0:8, :] = tail


def _ffn_a(h, hs, layer, wg, wu, cw, cache_t, seq):
    m, d = h.shape
    ms = hs.shape[0]
    f = wg.shape[2]
    tm, tf = 1024, 512
    tiles_per_b = seq // tm
    nb = m // seq
    return pl.pallas_call(
        functools.partial(_ffn_a_kernel, tm=tm, tiles_per_b=tiles_per_b, nsub=4),
        grid=(f // tf, m // tm),
        in_specs=[pl.BlockSpec((tm, d), lambda j, i: (i, 0)),
                  pl.BlockSpec((ms, d), lambda j, i: (0, 0)),
                  pl.BlockSpec((None, d, tf), lambda j, i: (layer, 0, j)),
                  pl.BlockSpec((None, d, tf), lambda j, i: (layer, 0, j)),
                  pl.BlockSpec((None, cw.shape[1], tf), lambda j, i: (layer, 0, j)),
                  pl.BlockSpec((None, 2, ms, tf), lambda j, i: (layer, 0, 0, j))],
        out_specs=[pl.BlockSpec((tm, tf), lambda j, i: (i, j)),
                   pl.BlockSpec((None, 8, tf), lambda j, i: (i // tiles_per_b, 0, j)),
                   pl.BlockSpec((ms, tf), lambda j, i: (0, j)),
                   pl.BlockSpec((2, ms, tf), lambda j, i: (0, 0, j))],
        out_shape=[jax.ShapeDtypeStruct((m, f), BF16),
                   jax.ShapeDtypeStruct((nb, 8, f), F32),
                   jax.ShapeDtypeStruct((ms, f), BF16),
                   jax.ShapeDtypeStruct((2, ms, f), F32)],
        scratch_shapes=[pltpu.VMEM((d, tf), BF16), pltpu.VMEM((d, tf), BF16),
                        pltpu.VMEM((tm + 8, tf), F32), pltpu.VMEM((tm, tf), F32)],
        compiler_params=_params(2),
        name="ffn_a",
    )(h, hs, wg, wu, cw, cache_t)


def _mm_epi_kernel(*refs, nk, per_row, tiles_per_b, post_idx, next_idx, next_dtype, slab):
    has_next = next_idx is not None
    if has_next:
        a_ref, w_ref, x_ref, gt, nwp, shn, scn, nwn, xo_ref, ho_ref = refs
    else:
        a_ref, w_ref, x_ref, gt, nwp, xo_ref = refs
    k = pl.program_id(1)
    b = pl.program_id(0) // tiles_per_b
    tm, d = xo_ref.shape
    half = d // 2
    a = a_ref[...]

    @pl.when(k == 0)
    def _():
        xo_ref[...] = jnp.zeros((tm, d), F32)

    for n in range(2):
        cols = slice(n * half, (n + 1) * half)
        xo_ref[:, cols] += _dot(a, w_ref[:, cols].astype(BF16))

    @pl.when(k == nk - 1)
    def _():
        def sel(r, rows):
            return r[rows, :] if per_row else r[pl.ds(b, 1), :]

        def body(s, carry):
            rows = pl.ds(pl.multiple_of(s * slab, slab), slab)
            xn = x_ref[rows, :] + sel(gt, rows) * _rms(xo_ref[rows, :], nwp[post_idx:post_idx + 1, :])
            xo_ref[rows, :] = xn
            if has_next:
                ho_ref[rows, :] = (_rms(xn, nwn[next_idx:next_idx + 1, :]) * (1.0 + sel(scn, rows))
                                   + sel(shn, rows)).astype(next_dtype)
            return carry

        lax.fori_loop(0, tm // slab, body, 0)


def _mm_epi(a, w, wl, x, mods, norm_w, layer, gate_chunk, post_idx, nxt, *, per_row, seq, name):
    m, kdim = a.shape
    d = w.shape[2]
    tm = m if per_row else 1024
    tk = 512
    nk = kdim // tk
    tiles_per_b = 1 if per_row else seq // tm
    if per_row:
        ms = lambda l, c: _mod_spec_sample(d, tm, l, c, 2)
    else:
        ms = lambda l, c: _mod_spec_prompt(d, l, c, 2)
    nw_spec = lambda l: _const_spec((None,) + norm_w.shape[1:], (l, 0, 0), 2)
    once = pl.Buffered(1)
    in_specs = [pl.BlockSpec((tm, tk), lambda i, k: (i, k)),
                pl.BlockSpec((None, tk, d), lambda i, k: (wl, k, 0)),
                pl.BlockSpec((tm, d), lambda i, k: (i, 0), pipeline_mode=once),
                ms(layer, gate_chunk), nw_spec(layer)]
    args = [a, w, x, mods, norm_w]
    out_specs = [pl.BlockSpec((tm, d), lambda i, k: (i, 0))]
    out_shape = [jax.ShapeDtypeStruct((m, d), F32)]
    next_idx = next_dtype = None
    if nxt is not None:
        nlayer, next_idx, sh_chunk, sc_chunk, next_dtype = nxt
        in_specs += [ms(nlayer, sh_chunk), ms(nlayer, sc_chunk), nw_spec(nlayer)]
        args += [mods, mods, norm_w]
        out_specs.append(pl.BlockSpec((tm, d), lambda i, k: (i, 0)))
        out_shape.append(jax.ShapeDtypeStruct((m, d), next_dtype))
    res = pl.pallas_call(
        functools.partial(_mm_epi_kernel, nk=nk, per_row=per_row, tiles_per_b=tiles_per_b,
                          post_idx=post_idx, next_idx=next_idx, next_dtype=next_dtype,
                          slab=min(tm, 256)),
        grid=(m // tm, nk),
        in_specs=in_specs,
        out_specs=out_specs,
        out_shape=out_shape,
        compiler_params=_params(2, 60 * 1024 * 1024),
        name=name,
    )(*args)
    return res if nxt is not None else (res[0], None)


def _proj_kernel(*refs, tm, tiles_per_b, nsub, mode, scale):
    if mode == "z":
        h_ref, hs_ref, w_ref, o_ref, ps_ref, wb = refs
    else:
        h_ref, hs_ref, w_ref, cw_ref, o_ref, st_ref, ps_ref, wb, buf = refs
    i = pl.program_id(1)
    t = i % tiles_per_b
    sub = tm // nsub
    tn = wb.shape[1]

    @pl.when(i == 0)
    def _():
        wb[...] = w_ref[...].T.astype(BF16)
        ps_ref[...] = _dot(hs_ref[...], wb[...])

    if mode == "z":
        o_ref[...] = _dot(h_ref[...], wb[...]).astype(BF16)
        return

    @pl.when(t == 0)
    def _():
        buf[0:8, :] = jnp.zeros((8, tn), F32)

    def matmul(s):
        buf[8 + s * sub:8 + (s + 1) * sub, :] = _dot(h_ref[s * sub:(s + 1) * sub, :], wb[...])

    def epilogue(s):
        r0 = 8 + s * sub
        conv = (cw_ref[3:4, :] * buf[r0:r0 + sub, :] + cw_ref[2:3, :] * buf[pl.ds(r0 - 1, sub), :]
                + cw_ref[1:2, :] * buf[pl.ds(r0 - 2, sub), :] + cw_ref[0:1, :] * buf[pl.ds(r0 - 3, sub), :])
        act = _silu(conv)
        if mode == "qk":
            outs = []
            for hh in range(tn // HEAD):
                xs = act[:, hh * HEAD:(hh + 1) * HEAD]
                outs.append(xs * (lax.rsqrt(jnp.sum(xs * xs, axis=-1, keepdims=True) + EPS) * scale))
            act = jnp.concatenate(outs, axis=1)
        o_ref[s * sub:(s + 1) * sub, :] = act.astype(BF16)

    matmul(0)
    for s in range(nsub):
        if s + 1 < nsub:
            matmul(s + 1)
        epilogue(s)
    tail = buf[tm:tm + 8, :]
    st_ref[...] = tail
    buf[0:8, :] = tail


def _proj(h, hs, w_t, conv_w, col0, width, seq, mode, scale, name):
    m, d = h.shape
    ms = hs.shape[0]
    tm, tn = 1024, 1024
    tiles_per_b = seq // tm
    nb = m // seq
    cb = col0 // tn
    conv = mode != "z"
    in_specs = [pl.BlockSpec((tm, d), lambda j, i: (i, 0)),
                pl.BlockSpec((ms, d), lambda j, i: (0, 0)),
                pl.BlockSpec((None, tn, d), lambda j, i: (0, cb + j, 0))]
    args = [h, hs, w_t]
    out_specs = [pl.BlockSpec((tm, tn), lambda j, i: (i, j))]
    out_shape = [jax.ShapeDtypeStruct((m, width), BF16)]
    scratch = [pltpu.VMEM((d, tn), BF16)]
    if conv:
        in_specs.append(pl.BlockSpec((None, conv_w.shape[1], tn), lambda j, i: (0, 0, cb + j)))
        args.append(conv_w)
        out_specs.append(pl.BlockSpec((None, 8, tn), lambda j, i: (i // tiles_per_b, 0, j)))
        out_shape.append(jax.ShapeDtypeStruct((nb, 8, width), F32))
        scratch.append(pltpu.VMEM((tm + 8, tn), F32))
    out_specs.append(pl.BlockSpec((ms, tn), lambda j, i: (0, j)))
    out_shape.append(jax.ShapeDtypeStruct((ms, width), F32))
    return pl.pallas_call(
        functools.partial(_proj_kernel, tm=tm, tiles_per_b=tiles_per_b, nsub=4, mode=mode, scale=scale),
        grid=(width // tn, m // tm),
        in_specs=in_specs,
        out_specs=out_specs,
        out_shape=out_shape,
        scratch_shapes=scratch,
        compiler_params=_params(2),
        name=name,
    )(*args)


def _gates_kernel(h_ref, w_ref, al_ref, dt_ref, beta_ref, gc_ref, *, tm, chunk):
    nh = al_ref.shape[1]
    ba = lax.dot_general(h_ref[...], w_ref[...].astype(BF16), NT_DIMS, preferred_element_type=F32)
    beta_ref[...] = jax.nn.sigmoid(ba[:, :nh])
    xx = ba[:, nh:] + dt_ref[...]
    softplus = jnp.maximum(xx, 0.0) + jnp.log1p(jnp.exp(-jnp.abs(xx)))
    g = -jnp.exp(al_ref[...]) * softplus
    shift = int(math.log2(chunk))
    r = lax.broadcasted_iota(jnp.int32, (tm, tm), 0)
    c = lax.broadcasted_iota(jnp.int32, (tm, tm), 1)
    tri = jnp.where(((r >> shift) == (c >> shift)) & (c <= r), 1.0, 0.0).astype(F32)
    gc_ref[...] = jnp.dot(tri, g, precision=lax.Precision.HIGHEST, preferred_element_type=F32)


def _gates(h, w_t, row0, a_log, dt_bias, chunk, name):
    m, d = h.shape
    nh = a_log.shape[1]
    tm = min(m, 256)
    assert row0 % (2 * nh) == 0 and w_t.shape[1] - row0 == 2 * nh
    return pl.pallas_call(
        functools.partial(_gates_kernel, tm=tm, chunk=chunk),
        grid=(m // tm,),
        in_specs=[pl.BlockSpec((tm, d), lambda i: (i, 0)),
                  pl.BlockSpec((None, 2 * nh, d), lambda i: (0, row0 // (2 * nh), 0)),
                  pl.BlockSpec((None, 1, nh), lambda i: (0, 0, 0)),
                  pl.BlockSpec((None, 1, nh), lambda i: (0, 0, 0))],
        out_specs=[pl.BlockSpec((tm, nh), lambda i: (i, 0)),
                   pl.BlockSpec((tm, nh), lambda i: (i, 0))],
        out_shape=[jax.ShapeDtypeStruct((m, nh), F32), jax.ShapeDtypeStruct((m, nh), F32)],
        compiler_params=_params(1),
        name=name,
    )(h, w_t, a_log.reshape(-1, 1, nh), dt_bias.reshape(-1, 1, nh))


def _delta_kernel(q_ref, k_ref, v_ref, z_ref, bt_ref, gc_ref, gct_ref, nw_ref,
                  o_ref, so_ref, s_scr, *, tc, hps, chunk):
    hg = pl.program_id(1)
    t = pl.program_id(2)
    nt = pl.num_programs(2)
    nh = bt_ref.shape[1]
    nck = tc // chunk
    n_factors = int(math.log2(chunk))

    @pl.when(t == 0)
    def _():
        s_scr[...] = jnp.zeros(s_scr.shape, F32)

    lane = lax.broadcasted_iota(jnp.int32, (chunk, nh), 1)
    ri = lax.broadcasted_iota(jnp.int32, (chunk, chunk), 0)
    ci = lax.broadcasted_iota(jnp.int32, (chunk, chunk), 1)
    tril = ci <= ri
    strict = ci < ri
    ri2 = lax.broadcasted_iota(jnp.int32, (chunk, 2 * chunk), 0)
    ci2 = lax.broadcasted_iota(jnp.int32, (chunk, 2 * chunk), 1)
    eye2 = jnp.where((ci2 == ri2) | (ci2 == ri2 + chunk), 1.0, 0.0).astype(F32)
    left = ci2 < chunk

    def blockdiag(xp):
        return jnp.concatenate([jnp.where(left, xp, 0.0), jnp.where(left, 0.0, xp)], axis=0)

    pairs = [(c, jq) for c in range(nck) for jq in range(hps // 2)]
    heads = [(c, hh) for c in range(nck) for hh in range(hps)]
    kc, qc, lowp, qkm, bcol, gcol = {}, {}, {}, {}, {}, {}
    for c, jq in pairs:
        rows = slice(c * chunk, (c + 1) * chunk)
        qcols = slice(jq * HEAD, (jq + 1) * HEAD)
        qc[c, jq] = q_ref[rows, qcols].astype(F32)
        kc[c, jq] = k_ref[rows, qcols].astype(F32)
        kkqk = lax.dot_general(jnp.concatenate([kc[c, jq], qc[c, jq]], axis=0), kc[c, jq],
                               NT_DIMS, preferred_element_type=F32)
        btb = bt_ref[rows, :]
        gcb = gc_ref[rows, :]
        lows = []
        for r in range(2):
            hh = 2 * jq + r
            hidx = hg * hps + hh
            pick = lane == hidx
            bcol[c, hh] = jnp.sum(jnp.where(pick, btb, 0.0), axis=1, keepdims=True)
            gcol[c, hh] = jnp.sum(jnp.where(pick, gcb, 0.0), axis=1, keepdims=True)
            grow = gct_ref[c, pl.ds(hidx, 1), :]
            decay = jnp.exp(jnp.where(tril, gcol[c, hh] - grow, -jnp.inf))
            lows.append(jnp.where(strict, kkqk[:chunk] * bcol[c, hh] * decay, 0.0))
            qkm[c, hh] = jnp.where(tril, kkqk[chunk:] * decay, 0.0)
        lowp[c, jq] = jnp.concatenate(lows, axis=1)

    tinv = {p: eye2 - lowp[p] for p in pairs}
    pw = {p: _dot(lowp[p], blockdiag(lowp[p])) for p in pairs}
    for f in range(n_factors - 1):
        for p in pairs:
            bd = blockdiag(pw[p])
            if f == n_factors - 2:
                tinv[p] = tinv[p] + _dot(tinv[p], bd)
            else:
                both = _dot(jnp.concatenate([tinv[p], pw[p]], axis=0), bd)
                tinv[p] = tinv[p] + both[:chunk]
                pw[p] = both[chunk:]

    uw, ecol = {}, {}
    for c, hh in heads:
        rows = slice(c * chunk, (c + 1) * chunk)
        vcols = slice(hh * HEAD, (hh + 1) * HEAD)
        ecol[c, hh] = jnp.exp(gcol[c, hh])
        vc = v_ref[rows, vcols].astype(F32)
        rhs = jnp.concatenate([vc * bcol[c, hh], kc[c, hh // 2] * (bcol[c, hh] * ecol[c, hh])], axis=1)
        r = hh % 2
        uw[c, hh] = _dot(tinv[c, hh // 2][:, r * chunk:(r + 1) * chunk], rhs)

    s = [s_scr[hh] for hh in range(hps)]
    for c in range(nck):
        rows = slice(c * chunk, (c + 1) * chunk)
        ws_qs = [_dot(jnp.concatenate([uw[c, hh][:, HEAD:], qc[c, hh // 2] * ecol[c, hh]], axis=0), s[hh])
                 for hh in range(hps)]
        vnew = [uw[c, hh][:, :HEAD] - ws_qs[hh][:chunk] for hh in range(hps)]
        outs = [ws_qs[hh][chunk:] + _dot(qkm[c, hh], vnew[hh]) for hh in range(hps)]
        for hh in range(hps):
            glast = gcol[c, hh][chunk - 1:chunk, :]
            kd = kc[c, hh // 2] * jnp.exp(glast - gcol[c, hh])
            s[hh] = s[hh] * jnp.exp(glast) + lax.dot_general(kd, vnew[hh], TN_DIMS,
                                                             preferred_element_type=F32)
        for hh in range(hps):
            vcols = slice(hh * HEAD, (hh + 1) * HEAD)
            o = outs[hh]
            og = (o * lax.rsqrt(jnp.mean(o * o, axis=-1, keepdims=True) + EPS) * nw_ref[...]
                  * _silu(z_ref[rows, vcols].astype(F32)))
            o_ref[rows, vcols] = og.astype(BF16)
    for hh in range(hps):
        s_scr[hh] = s[hh]

    @pl.when(t == nt - 1)
    def _():
        so_ref[...] = s_scr[...]


def _delta_prompt(q, k, v, z, beta, gc, norm_w, nb, seq):
    m = q.shape[0]
    chunk = DELTA_CHUNK
    tc = 256
    hps = 4
    nhg = N_V_HEADS // hps
    nt = seq // tc
    vw = hps * HEAD
    qw = vw // 2
    gct = gc.reshape(m // chunk, chunk, N_V_HEADS).transpose(0, 2, 1)
    return pl.pallas_call(
        functools.partial(_delta_kernel, tc=tc, hps=hps, chunk=chunk),
        grid=(nb, nhg, nt),
        in_specs=[pl.BlockSpec((tc, qw), lambda b, g, t: (b * nt + t, g)),
                  pl.BlockSpec((tc, qw), lambda b, g, t: (b * nt + t, g)),
                  pl.BlockSpec((tc, vw), lambda b, g, t: (b * nt + t, g)),
                  pl.BlockSpec((tc, vw), lambda b, g, t: (b * nt + t, g)),
                  pl.BlockSpec((tc, N_V_HEADS), lambda b, g, t: (b * nt + t, 0)),
                  pl.BlockSpec((tc, N_V_HEADS), lambda b, g, t: (b * nt + t, 0)),
                  pl.BlockSpec((tc // chunk, N_V_HEADS, chunk), lambda b, g, t: (b * nt + t, 0, 0)),
                  _const_spec((None, 1, HEAD), (0, 0, 0), 3)],
        out_specs=[pl.BlockSpec((tc, vw), lambda b, g, t: (b * nt + t, g)),
                   pl.BlockSpec((None, None, hps, HEAD, HEAD), lambda b, g, t: (0, b, g, 0, 0))],
        out_shape=[jax.ShapeDtypeStruct((m, N_V_HEADS * HEAD), BF16),
                   jax.ShapeDtypeStruct((1, nb, N_V_HEADS, HEAD, HEAD), F32)],
        scratch_shapes=[pltpu.VMEM((hps, HEAD, HEAD), F32)],
        compiler_params=_params(3),
        name="delta_prompt",
    )(q, k, v, z, beta, gc, gct, norm_w.reshape(-1, 1, HEAD))


def _delta_sample_kernel(p_ref, st_ref, cw_ref, bt_ref, g_ref, nw_ref, s_ref,
                         o_ref, sto_ref, so_ref, qk_scr):
    cdim = st_ref.shape[1]
    key_dim = N_QK_HEADS * HEAD
    rep = N_V_HEADS // N_QK_HEADS
    xrow = p_ref[:, 0:cdim]
    conv = (cw_ref[3:4, :] * xrow + cw_ref[2:3, :] * st_ref[2:3, :]
            + cw_ref[1:2, :] * st_ref[1:2, :] + cw_ref[0:1, :] * st_ref[0:1, :])
    sto_ref[0:1, :] = st_ref[1:2, :]
    sto_ref[1:2, :] = st_ref[2:3, :]
    sto_ref[2:3, :] = xrow
    act = _silu(conv)
    qk_scr[...] = jnp.zeros(qk_scr.shape, F32)
    for j in range(2 * N_QK_HEADS):
        xs = act[:, j * HEAD:(j + 1) * HEAD]
        xs = xs * lax.rsqrt(jnp.sum(xs * xs, axis=-1, keepdims=True) + EPS)
        if j < N_QK_HEADS:
            xs = xs * (HEAD ** -0.5)
        qk_scr[j:j + 1, :] = xs
    qkt = qk_scr[...].T
    a_row = jnp.exp(g_ref[...])
    b_row = bt_ref[...]
    hs = range(N_V_HEADS)
    kcols = [qkt[:, N_QK_HEADS + h // rep:N_QK_HEADS + h // rep + 1] for h in hs]
    sk = [jnp.sum(s_ref[h] * kcols[h], axis=0, keepdims=True) for h in hs]
    vnew = [b_row[:, h:h + 1] * (act[:, 2 * key_dim + h * HEAD:2 * key_dim + (h + 1) * HEAD]
                                  - a_row[:, h:h + 1] * sk[h]) for h in hs]
    outs = []
    for h in hs:
        s_new = a_row[:, h:h + 1] * s_ref[h] + kcols[h] * vnew[h]
        so_ref[h] = s_new
        outs.append(jnp.sum(s_new * qkt[:, h // rep:h // rep + 1], axis=0, keepdims=True))
    for h in hs:
        o = outs[h]
        z = p_ref[:, cdim + h * HEAD:cdim + (h + 1) * HEAD]
        og = o * lax.rsqrt(jnp.mean(o * o, axis=-1, keepdims=True) + EPS) * nw_ref[...] * _silu(z)
        o_ref[:, h * HEAD:(h + 1) * HEAD] = og.astype(BF16)


def _delta_sample(proj, conv_state, conv_w, beta, g, norm_w, s0):
    nb, width = proj.shape
    npast, cdim = conv_state.shape[2:]
    vdim = N_V_HEADS * HEAD
    return pl.pallas_call(
        _delta_sample_kernel,
        grid=(nb,),
        in_specs=[pl.BlockSpec((None, 1, width), lambda b: (b, 0, 0)),
                  pl.BlockSpec((None, None, npast, cdim), lambda b: (0, b, 0, 0)),
                  _const_spec((None,) + conv_w.shape[1:], (0, 0, 0), 1),
                  pl.BlockSpec((None, 1, N_V_HEADS), lambda b: (b, 0, 0)),
                  pl.BlockSpec((None, 1, N_V_HEADS), lambda b: (b, 0, 0)),
                  _const_spec((None, 1, HEAD), (0, 0, 0), 1),
                  pl.BlockSpec((None, None, N_V_HEADS, HEAD, HEAD), lambda b: (0, b, 0, 0, 0))],
        out_specs=[pl.BlockSpec((None, 1, vdim), lambda b: (b, 0, 0)),
                   pl.BlockSpec((None, None, npast, cdim), lambda b: (0, b, 0, 0)),
                   pl.BlockSpec((None, None, N_V_HEADS, HEAD, HEAD), lambda b: (0, b, 0, 0, 0))],
        out_shape=[jax.ShapeDtypeStruct((nb, 1, vdim), BF16),
                   jax.ShapeDtypeStruct((1,) + conv_state.shape[1:], F32),
                   jax.ShapeDtypeStruct((1,) + s0.shape[1:], F32)],
        scratch_shapes=[pltpu.VMEM((HEAD, HEAD), F32)],
        compiler_params=_params(1),
        name="delta_sample",
    )(proj.reshape(nb, 1, width), conv_state, conv_w, beta.reshape(nb, 1, N_V_HEADS),
      g.reshape(nb, 1, N_V_HEADS), norm_w.reshape(-1, 1, HEAD), s0)


def kernel(x_prompt, x_sample, c_prompt, c_sample, cache_pool, state_conv, state_rec, cache_ffn_conv,
           norm_w, ada_w, ada_b, pool_w, pool_scale, dn_w_in, dn_conv_w, dn_a_log, dn_dt_bias, dn_norm_w,
           dn_w_out, ffn_w_gate, ffn_w_up, ffn_conv_w, ffn_w_down):
    bp, seq, d = x_prompt.shape
    bs = x_sample.shape[0]
    key_dim = N_QK_HEADS * HEAD
    val_dim = N_V_HEADS * HEAD
    conv_dim = 2 * key_dim + val_dim
    assert bs == 128 and bp <= 8 and x_sample.shape[1] == 1
    assert ada_w.shape[0] == 2 and dn_w_in.shape[0] == 1 and pool_w.shape[0] == 1

    c_all = jnp.concatenate([c_sample, c_prompt, jnp.zeros((MOD_ROWS - bs - bp, d), F32)], axis=0)
    mods = _ada(c_all, ada_w, ada_b)

    xp = x_prompt.reshape(bp * seq, d)
    xs = x_sample.reshape(bs, d)

    xp, hp, pool_p16 = _pool_prompt(xp, mods, 0, norm_w, pool_w, pool_scale, seq)
    xs, hs, pool_st = _pool_sample(xs, mods, 0, norm_w, pool_w, pool_scale,
                                   cache_pool.transpose(0, 2, 1, 3))
    pool_p = pool_p16[:, 16 - POOL_STATE:][None]
    pool_s = pool_st.transpose(1, 0, 2)[None]

    ffn_p, ffn_s = [], []
    ffn_cache_t = cache_ffn_conv.transpose(0, 2, 1, 3)

    def ffn(layer, xp, hp, xs, hs, nxt):
        ap, stp, a_s, sts = _ffn_a(hp, hs, layer, ffn_w_gate, ffn_w_up, ffn_conv_w, ffn_cache_t, seq)
        ffn_p.append(stp[:, 6:8])
        ffn_s.append(sts.transpose(1, 0, 2))
        xp, hp = _mm_epi(ap, ffn_w_down, layer, xp, mods, norm_w, layer, 5, 3, nxt,
                         per_row=False, seq=seq, name="ffn_b_prompt")
        xs, hs = _mm_epi(a_s, ffn_w_down, layer, xs, mods, norm_w, layer, 5, 3, nxt,
                         per_row=True, seq=1, name="ffn_b_sample")
        return xp, hp, xs, hs

    xp, hp, xs, hs = ffn(0, xp, hp, xs, hs, (1, 0, 0, 1, BF16))

    n_main = conv_dim + val_dim
    w_t = jnp.swapaxes(dn_w_in, 1, 2)
    qn, st_q, pq = _proj(hp, hs, w_t, dn_conv_w, 0, key_dim, seq, "qk", HEAD ** -0.5, "proj_q")
    kn, st_k, pk = _proj(hp, hs, w_t, dn_conv_w, key_dim, key_dim, seq, "qk", 1.0, "proj_k")
    vv, st_v, pv = _proj(hp, hs, w_t, dn_conv_w, 2 * key_dim, val_dim, seq, "v", 1.0, "proj_v")
    zz, pz = _proj(hp, hs, w_t, dn_conv_w, conv_dim, val_dim, seq, "z", 1.0, "proj_z")
    proj_s = jnp.concatenate([pq, pk, pv, pz], axis=1)
    beta_p, gc_p = _gates(hp, w_t, n_main, dn_a_log, dn_dt_bias, DELTA_CHUNK, "gates_prompt")
    beta_s, g_s = _gates(hs, w_t, n_main, dn_a_log, dn_dt_bias, 1, "gates_sample")
    conv_p = jnp.concatenate([st_q, st_k, st_v], axis=2)[:, 5:8][None]
    o_p, rec_p = _delta_prompt(qn, kn, vv, zz, beta_p, gc_p, dn_norm_w, bp, seq)
    o_s, conv_s, rec_s = _delta_sample(proj_s, state_conv, dn_conv_w, beta_s, g_s, dn_norm_w, state_rec)
    nxt = (1, 2, 3, 4, BF16)
    xp, hp = _mm_epi(o_p, dn_w_out, 0, xp, mods, norm_w, 1, 2, 1, nxt,
                     per_row=False, seq=seq, name="out_prompt")
    xs, hs = _mm_epi(o_s.reshape(bs, val_dim), dn_w_out, 0, xs, mods, norm_w, 1, 2, 1, nxt,
                     per_row=True, seq=1, name="out_sample")
    xp, _, xs, _ = ffn(1, xp, hp, xs, hs, None)

    return (xp.reshape(bp, seq, d), xs.reshape(bs, 1, d), pool_p, pool_s, conv_p, conv_s,
            rec_p, rec_s, jnp.stack(ffn_p), jnp.stack(ffn_s))
```

```python
import functools
import math

import jax
import jax.numpy as jnp
from jax import lax
from jax.experimental import pallas as pl
from jax.experimental.pallas import tpu as pltpu

F32 = jnp.float32
BF16 = jnp.bfloat16
EPS = 1e-6
POOL_WINDOWS = (2, 4, 8, 16)
POOL_STATE = max(POOL_WINDOWS) - 1
HEAD = 128
N_QK_HEADS = 16
N_V_HEADS = 32
DELTA_CHUNK = 128
PAST_LEN = 16384
VMEM_LIMIT = 56 * 1024 * 1024
MOD_ROWS = 136
PROMPT_MOD_BLOCK = 16
NT_DIMS = (((1,), (1,)), ((), ()))
TN_DIMS = (((0,), (0,)), ((), ()))


def _params(n_axes, vmem=VMEM_LIMIT):
    return pltpu.CompilerParams(dimension_semantics=("arbitrary",) * n_axes, vmem_limit_bytes=vmem)


def _rms(x, w):
    return x * lax.rsqrt(jnp.mean(x * x, axis=-1, keepdims=True) + EPS) * w


def _silu(x):
    return x * jax.nn.sigmoid(x)


def _dot(a, b):
    return jnp.dot(a, b, preferred_element_type=F32)


def _ada_kernel(c_ref, w_ref, b_ref, o_ref):
    o_ref[...] = _dot(_silu(c_ref[...]), w_ref[...]) + b_ref[...]


def _ada(c_all, ada_w, ada_b):
    depth, d, n = ada_w.shape
    tn = 1024
    return pl.pallas_call(
        _ada_kernel,
        grid=(depth, n // tn),
        in_specs=[pl.BlockSpec((MOD_ROWS, d), lambda l, j: (0, 0)),
                  pl.BlockSpec((None, d, tn), lambda l, j: (l, 0, j)),
                  pl.BlockSpec((None, 1, tn), lambda l, j: (l, 0, j))],
        out_specs=pl.BlockSpec((None, MOD_ROWS, tn), lambda l, j: (l, 0, j)),
        out_shape=jax.ShapeDtypeStruct((depth, MOD_ROWS, n), F32),
        compiler_params=_params(2),
        name="ada_mod",
    )(c_all, ada_w, ada_b.reshape(depth, 1, n))


def _const_spec(block, idx, n_axes):
    if n_axes == 1:
        return pl.BlockSpec(block, lambda i: idx)
    if n_axes == 2:
        return pl.BlockSpec(block, lambda i, j: idx)
    return pl.BlockSpec(block, lambda i, j, k: idx)


def _mod_spec_prompt(d, layer, chunk, n_axes):
    return _const_spec((None, 8, d), (layer, PROMPT_MOD_BLOCK, chunk), n_axes)


def _mod_spec_sample(d, rows, layer, chunk, n_axes):
    if n_axes == 1:
        return pl.BlockSpec((None, rows, d), lambda i: (layer, i, chunk))
    return pl.BlockSpec((None, rows, d), lambda i, j: (layer, i, chunk))


def _pool_groups(h, window_sum, recip_cnt, pw_ref, ps_ref, gd):
    ys = []
    for gi, w in enumerate(POOL_WINDOWS):
        c0 = gi * gd
        dm = window_sum(w, c0) * recip_cnt(w) - h[:, c0:c0 + gd]
        ys.append(_dot(dm, pw_ref[gi]))
    return jnp.concatenate(ys, axis=1) * ps_ref[...]


def _pool_prompt_kernel(x_ref, shm, scm, gtm, shf, scf, nw_ref, pw_ref, ps_ref,
                        xo_ref, hf_ref, st_ref, hp, *, tm, tiles_per_b, n_past):
    i = pl.program_id(0)
    b = i // tiles_per_b
    t = i % tiles_per_b
    d = x_ref.shape[1]
    gd = d // len(POOL_WINDOWS)

    def row(r):
        return r[pl.ds(b, 1), :]

    x = x_ref[...]
    h = _rms(x, nw_ref[0:1, :]) * (1.0 + row(scm)) + row(shm)

    @pl.when(t == 0)
    def _():
        hp[0:16, :] = jnp.zeros((16, d), F32)

    hp[16:16 + tm, :] = h
    pos = t * tm + lax.broadcasted_iota(jnp.int32, (tm, 1), 0) + n_past

    def window_sum(w, c0):
        acc = hp[:, c0:c0 + gd]
        span = 1
        while span < w:
            acc = acc + pltpu.roll(acc, span, 0)
            span *= 2
        return acc[16:16 + tm]

    def recip_cnt(w):
        return 1.0 / jnp.minimum(pos + 1, w).astype(F32)

    y = _pool_groups(h, window_sum, recip_cnt, pw_ref, ps_ref, gd)
    tail = hp[tm:tm + 16, :]
    st_ref[...] = tail
    hp[0:16, :] = tail
    xn = x + row(gtm) * _rms(y, nw_ref[1:2, :])
    xo_ref[...] = xn
    hf_ref[...] = (_rms(xn, nw_ref[2:3, :]) * (1.0 + row(scf)) + row(shf)).astype(BF16)


def _pool_prompt(x2, mods, layer, norm_w, pool_w, pool_scale, seq):
    m, d = x2.shape
    j = layer // 2
    tm = 256
    tiles_per_b = seq // tm
    nb = m // seq
    ms = lambda c: _mod_spec_prompt(d, layer, c, 1)
    return pl.pallas_call(
        functools.partial(_pool_prompt_kernel, tm=tm, tiles_per_b=tiles_per_b, n_past=0),
        grid=(m // tm,),
        in_specs=[pl.BlockSpec((tm, d), lambda i: (i, 0)),
                  ms(0), ms(1), ms(2), ms(3), ms(4),
                  _const_spec((None,) + norm_w.shape[1:], (layer, 0, 0), 1),
                  _const_spec((None,) + pool_w.shape[1:], (j, 0, 0, 0), 1),
                  _const_spec((None, 1, d), (j, 0, 0), 1)],
        out_specs=[pl.BlockSpec((tm, d), lambda i: (i, 0)),
                   pl.BlockSpec((tm, d), lambda i: (i, 0)),
                   pl.BlockSpec((None, 16, d), lambda i: (i // tiles_per_b, 0, 0))],
        out_shape=[jax.ShapeDtypeStruct((m, d), F32),
                   jax.ShapeDtypeStruct((m, d), BF16),
                   jax.ShapeDtypeStruct((nb, 16, d), F32)],
        scratch_shapes=[pltpu.VMEM((tm + 16, d), F32)],
        compiler_params=_params(1),
        name="pool_prompt",
    )(x2, mods, mods, mods, mods, mods, norm_w, pool_w, pool_scale.reshape(-1, 1, d))


def _pool_sample_kernel(x_ref, shm, scm, gtm, shf, scf, nw_ref, pw_ref, ps_ref, c_ref,
                        xo_ref, hf_ref, co_ref, *, n_past):
    d = x_ref.shape[1]
    gd = d // len(POOL_WINDOWS)
    x = x_ref[...]
    h = _rms(x, nw_ref[0:1, :]) * (1.0 + scm[...]) + shm[...]

    def window_sum(w, c0):
        acc = h[:, c0:c0 + gd]
        for j in range(1, w):
            acc = acc + c_ref[POOL_STATE - j, :, c0:c0 + gd]
        return acc

    def recip_cnt(w):
        return 1.0 / float(min(n_past + 1, w))

    y = _pool_groups(h, window_sum, recip_cnt, pw_ref, ps_ref, gd)
    for r in range(POOL_STATE - 1):
        co_ref[r] = c_ref[r + 1]
    co_ref[POOL_STATE - 1] = h
    xn = x + gtm[...] * _rms(y, nw_ref[1:2, :])
    xo_ref[...] = xn
    hf_ref[...] = (_rms(xn, nw_ref[2:3, :]) * (1.0 + scf[...]) + shf[...]).astype(BF16)


def _pool_sample(x2, mods, layer, norm_w, pool_w, pool_scale, cache_t):
    m, d = x2.shape
    j = layer // 2
    bb = 32
    ms = lambda c: _mod_spec_sample(d, bb, layer, c, 1)
    return pl.pallas_call(
        functools.partial(_pool_sample_kernel, n_past=PAST_LEN),
        grid=(m // bb,),
        in_specs=[pl.BlockSpec((bb, d), lambda i: (i, 0)),
                  ms(0), ms(1), ms(2), ms(3), ms(4),
                  _const_spec((None,) + norm_w.shape[1:], (layer, 0, 0), 1),
                  _const_spec((None,) + pool_w.shape[1:], (j, 0, 0, 0), 1),
                  _const_spec((None, 1, d), (j, 0, 0), 1),
                  pl.BlockSpec((None, POOL_STATE, bb, d), lambda i: (j, 0, i, 0))],
        out_specs=[pl.BlockSpec((bb, d), lambda i: (i, 0)),
                   pl.BlockSpec((bb, d), lambda i: (i, 0)),
                   pl.BlockSpec((POOL_STATE, bb, d), lambda i: (0, i, 0))],
        out_shape=[jax.ShapeDtypeStruct((m, d), F32),
                   jax.ShapeDtypeStruct((m, d), BF16),
                   jax.ShapeDtypeStruct((POOL_STATE, m, d), F32)],
        compiler_params=_params(1),
        name="pool_sample",
    )(x2, mods, mods, mods, mods, mods, norm_w, pool_w, pool_scale.reshape(-1, 1, d), cache_t)


def _ffn_a_kernel(h_ref, hs_ref, wg_ref, wu_ref, cw_ref, cs_ref,
                  a_ref, st_ref, as_ref, cso_ref, wgb, wub, gbuf, ubuf, *, tm, tiles_per_b, nsub):
    i = pl.program_id(1)
    t = i % tiles_per_b
    sub = tm // nsub
    cw0, cw1, cw2 = cw_ref[0:1, :], cw_ref[1:2, :], cw_ref[2:3, :]

    @pl.when(i == 0)
    def _():
        wgb[...] = wg_ref[...].astype(BF16)
        wub[...] = wu_ref[...].astype(BF16)
        hs = hs_ref[...]
        gs = _dot(hs, wgb[...])
        conv = cw2 * gs + cw1 * cs_ref[1] + cw0 * cs_ref[0]
        cso_ref[0] = cs_ref[1]
        cso_ref[1] = gs
        as_ref[...] = (_silu(conv) * _dot(hs, wub[...])).astype(BF16)

    @pl.when(t == 0)
    def _():
        gbuf[0:8, :] = jnp.zeros((8, gbuf.shape[1]), F32)

    def matmuls(s):
        rows = slice(s * sub, (s + 1) * sub)
        h = h_ref[rows, :]
        gbuf[8 + s * sub:8 + (s + 1) * sub, :] = _dot(h, wgb[...])
        ubuf[rows, :] = _dot(h, wub[...])

    def epilogue(s):
        r0 = 8 + s * sub
        conv = (cw2 * gbuf[r0:r0 + sub, :] + cw1 * gbuf[pl.ds(r0 - 1, sub), :]
                + cw0 * gbuf[pl.ds(r0 - 2, sub), :])
        a_ref[s * sub:(s + 1) * sub, :] = (_silu(conv) * ubuf[s * sub:(s + 1) * sub, :]).astype(BF16)

    matmuls(0)
    for s in range(nsub):
        if s + 1 < nsub:
            matmuls(s + 1)
        epilogue(s)
    tail = gbuf[tm:tm + 8, :]
    st_ref[...] = tail
    gbuf[0:8, :] = tail


def _ffn_a(h, hs, layer, wg, wu, cw, cache_t, seq):
    m, d = h.shape
    ms = hs.shape[0]
    f = wg.shape[2]
    tm, tf = 1024, 512
    tiles_per_b = seq // tm
    nb = m // seq
    return pl.pallas_call(
        functools.partial(_ffn_a_kernel, tm=tm, tiles_per_b=tiles_per_b, nsub=4),
        grid=(f // tf, m // tm),
        in_specs=[pl.BlockSpec((tm, d), lambda j, i: (i, 0)),
                  pl.BlockSpec((ms, d), lambda j, i: (0, 0)),
                  pl.BlockSpec((None, d, tf), lambda j, i: (layer, 0, j)),
                  pl.BlockSpec((None, d, tf), lambda j, i: (layer, 0, j)),
                  pl.BlockSpec((None, cw.shape[1], tf), lambda j, i: (layer, 0, j)),
                  pl.BlockSpec((None, 2, ms, tf), lambda j, i: (layer, 0, 0, j))],
        out_specs=[pl.BlockSpec((tm, tf), lambda j, i: (i, j)),
                   pl.BlockSpec((None, 8, tf), lambda j, i: (i // tiles_per_b, 0, j)),
                   pl.BlockSpec((ms, tf), lambda j, i: (0, j)),
                   pl.BlockSpec((2, ms, tf), lambda j, i: (0, 0, j))],
        out_shape=[jax.ShapeDtypeStruct((m, f), BF16),
                   jax.ShapeDtypeStruct((nb, 8, f), F32),
                   jax.ShapeDtypeStruct((ms, f), BF16),
                   jax.ShapeDtypeStruct((2, ms, f), F32)],
        scratch_shapes=[pltpu.VMEM((d, tf), BF16), pltpu.VMEM((d, tf), BF16),
                        pltpu.VMEM((tm + 8, tf), F32), pltpu.VMEM((tm, tf), F32)],
        compiler_params=_params(2),
        name="ffn_a",
    )(h, hs, wg, wu, cw, cache_t)


def _cast_kernel(x_ref, o_ref):
    o_ref[...] = x_ref[...].astype(o_ref.dtype)


def _to_bf16(w, name):
    nl, k, n = w.shape
    tk = 512
    return pl.pallas_call(
        _cast_kernel,
        grid=(nl, k // tk),
        in_specs=[pl.BlockSpec((None, tk, n), lambda l, i: (l, i, 0))],
        out_specs=pl.BlockSpec((None, tk, n), lambda l, i: (l, i, 0)),
        out_shape=jax.ShapeDtypeStruct(w.shape, BF16),
        compiler_params=_params(2),
        name=name,
    )(w)


def _mm_epi_kernel(*refs, nk, per_row, tiles_per_b, post_idx, next_idx, next_dtype, slab):
    has_next = next_idx is not None
    if has_next:
        a_ref, w_ref, x_ref, gt, nwp, shn, scn, nwn, xo_ref, ho_ref = refs
    else:
        a_ref, w_ref, x_ref, gt, nwp, xo_ref = refs
    k = pl.program_id(1)
    b = pl.program_id(0) // tiles_per_b
    tm, d = xo_ref.shape
    half = d // 2
    a = a_ref[...]

    @pl.when(k == 0)
    def _():
        xo_ref[...] = jnp.zeros((tm, d), F32)

    for n in range(2):
        cols = slice(n * half, (n + 1) * half)
        xo_ref[:, cols] += _dot(a, w_ref[:, cols])

    @pl.when(k == nk - 1)
    def _():
        def sel(r, rows):
            return r[rows, :] if per_row else r[pl.ds(b, 1), :]

        def body(s, carry):
            rows = pl.ds(pl.multiple_of(s * slab, slab), slab)
            xn = x_ref[rows, :] + sel(gt, rows) * _rms(xo_ref[rows, :], nwp[post_idx:post_idx + 1, :])
            xo_ref[rows, :] = xn
            if has_next:
                ho_ref[rows, :] = (_rms(xn, nwn[next_idx:next_idx + 1, :]) * (1.0 + sel(scn, rows))
                                   + sel(shn, rows)).astype(next_dtype)
            return carry

        lax.fori_loop(0, tm // slab, body, 0)


def _mm_epi(a, w, wl, x, mods, norm_w, layer, gate_chunk, post_idx, nxt, *, per_row, seq, name):
    m, kdim = a.shape
    d = w.shape[2]
    tm = m if per_row else 1024
    nk = 4
    tk = kdim // nk
    assert tk % 128 == 0 and w.dtype == BF16
    tiles_per_b = 1 if per_row else seq // tm
    if per_row:
        ms = lambda l, c: _mod_spec_sample(d, tm, l, c, 2)
    else:
        ms = lambda l, c: _mod_spec_prompt(d, l, c, 2)
    nw_spec = lambda l: _const_spec((None,) + norm_w.shape[1:], (l, 0, 0), 2)
    once = pl.Buffered(1)
    in_specs = [pl.BlockSpec((tm, tk), lambda i, k: (i, k)),
                pl.BlockSpec((None, tk, d), lambda i, k: (wl, k, 0)),
                pl.BlockSpec((tm, d), lambda i, k: (i, 0), pipeline_mode=once),
                ms(layer, gate_chunk), nw_spec(layer)]
    args = [a, w, x, mods, norm_w]
    out_specs = [pl.BlockSpec((tm, d), lambda i, k: (i, 0))]
    out_shape = [jax.ShapeDtypeStruct((m, d), F32)]
    next_idx = next_dtype = None
    if nxt is not None:
        nlayer, next_idx, sh_chunk, sc_chunk, next_dtype = nxt
        in_specs += [ms(nlayer, sh_chunk), ms(nlayer, sc_chunk), nw_spec(nlayer)]
        args += [mods, mods, norm_w]
        out_specs.append(pl.BlockSpec((tm, d), lambda i, k: (i, 0)))
        out_shape.append(jax.ShapeDtypeStruct((m, d), next_dtype))
    res = pl.pallas_call(
        functools.partial(_mm_epi_kernel, nk=nk, per_row=per_row, tiles_per_b=tiles_per_b,
                          post_idx=post_idx, next_idx=next_idx, next_dtype=next_dtype,
                          slab=min(tm, 256)),
        grid=(m // tm, nk),
        in_specs=in_specs,
        out_specs=out_specs,
        out_shape=out_shape,
        compiler_params=_params(2, 60 * 1024 * 1024),
        name=name,
    )(*args)
    return res if nxt is not None else (res[0], None)


def _proj_kernel(*refs, tm, tiles_per_b, nsub, mode, scale):
    if mode == "z":
        h_ref, hs_ref, w_ref, o_ref, ps_ref, wb = refs
    else:
        h_ref, hs_ref, w_ref, cw_ref, o_ref, st_ref, ps_ref, wb, buf = refs
    i = pl.program_id(1)
    t = i % tiles_per_b
    sub = tm // nsub
    tn = wb.shape[1]

    @pl.when(i == 0)
    def _():
        wb[...] = w_ref[...].T.astype(BF16)
        ps_ref[...] = _dot(hs_ref[...], wb[...])

    if mode == "z":
        o_ref[...] = _dot(h_ref[...], wb[...]).astype(BF16)
        return

    @pl.when(t == 0)
    def _():
        buf[0:8, :] = jnp.zeros((8, tn), F32)

    def matmul(s):
        buf[8 + s * sub:8 + (s + 1) * sub, :] = _dot(h_ref[s * sub:(s + 1) * sub, :], wb[...])

    def epilogue(s):
        r0 = 8 + s * sub
        conv = (cw_ref[3:4, :] * buf[r0:r0 + sub, :] + cw_ref[2:3, :] * buf[pl.ds(r0 - 1, sub), :]
                + cw_ref[1:2, :] * buf[pl.ds(r0 - 2, sub), :] + cw_ref[0:1, :] * buf[pl.ds(r0 - 3, sub), :])
        act = _silu(conv)
        if mode == "qk":
            outs = []
            for hh in range(tn // HEAD):
                xs = act[:, hh * HEAD:(hh + 1) * HEAD]
                outs.append(xs * (lax.rsqrt(jnp.sum(xs * xs, axis=-1, keepdims=True) + EPS) * scale))
            act = jnp.concatenate(outs, axis=1)
        o_ref[s * sub:(s + 1) * sub, :] = act.astype(BF16)

    matmul(0)
    for s in range(nsub):
        if s + 1 < nsub:
            matmul(s + 1)
        epilogue(s)
    tail = buf[tm:tm + 8, :]
    st_ref[...] = tail
    buf[0:8, :] = tail


def _proj(h, hs, w_t, conv_w, col0, width, seq, mode, scale, name):
    m, d = h.shape
    ms = hs.shape[0]
    tm, tn = 1024, 1024
    tiles_per_b = seq // tm
    nb = m // seq
    cb = col0 // tn
    conv = mode != "z"
    in_specs = [pl.BlockSpec((tm, d), lambda j, i: (i, 0)),
                pl.BlockSpec((ms, d), lambda j, i: (0, 0)),
                pl.BlockSpec((None, tn, d), lambda j, i: (0, cb + j, 0))]
    args = [h, hs, w_t]
    out_specs = [pl.BlockSpec((tm, tn), lambda j, i: (i, j))]
    out_shape = [jax.ShapeDtypeStruct((m, width), BF16)]
    scratch = [pltpu.VMEM((d, tn), BF16)]
    if conv:
        in_specs.append(pl.BlockSpec((None, conv_w.shape[1], tn), lambda j, i: (0, 0, cb + j)))
        args.append(conv_w)
        out_specs.append(pl.BlockSpec((None, 8, tn), lambda j, i: (i // tiles_per_b, 0, j)))
        out_shape.append(jax.ShapeDtypeStruct((nb, 8, width), F32))
        scratch.append(pltpu.VMEM((tm + 8, tn), F32))
    out_specs.append(pl.BlockSpec((ms, tn), lambda j, i: (0, j)))
    out_shape.append(jax.ShapeDtypeStruct((ms, width), F32))
    return pl.pallas_call(
        functools.partial(_proj_kernel, tm=tm, tiles_per_b=tiles_per_b, nsub=4, mode=mode, scale=scale),
        grid=(width // tn, m // tm),
        in_specs=in_specs,
        out_specs=out_specs,
        out_shape=out_shape,
        scratch_shapes=scratch,
        compiler_params=_params(2),
        name=name,
    )(*args)


def _gates_kernel(h_ref, w_ref, al_ref, dt_ref, beta_ref, gc_ref, *, tm, chunk):
    nh = al_ref.shape[1]
    ba = lax.dot_general(h_ref[...], w_ref[...].astype(BF16), NT_DIMS, preferred_element_type=F32)
    beta_ref[...] = jax.nn.sigmoid(ba[:, :nh])
    xx = ba[:, nh:] + dt_ref[...]
    softplus = jnp.maximum(xx, 0.0) + jnp.log1p(jnp.exp(-jnp.abs(xx)))
    g = -jnp.exp(al_ref[...]) * softplus
    shift = int(math.log2(chunk))
    r = lax.broadcasted_iota(jnp.int32, (tm, tm), 0)
    c = lax.broadcasted_iota(jnp.int32, (tm, tm), 1)
    tri = jnp.where(((r >> shift) == (c >> shift)) & (c <= r), 1.0, 0.0).astype(F32)
    gc_ref[...] = jnp.dot(tri, g, precision=lax.Precision.HIGHEST, preferred_element_type=F32)


def _gates(h, w_t, row0, a_log, dt_bias, chunk, name):
    m, d = h.shape
    nh = a_log.shape[1]
    tm = min(m, 256)
    assert row0 % (2 * nh) == 0 and w_t.shape[1] - row0 == 2 * nh
    return pl.pallas_call(
        functools.partial(_gates_kernel, tm=tm, chunk=chunk),
        grid=(m // tm,),
        in_specs=[pl.BlockSpec((tm, d), lambda i: (i, 0)),
                  pl.BlockSpec((None, 2 * nh, d), lambda i: (0, row0 // (2 * nh), 0)),
                  pl.BlockSpec((None, 1, nh), lambda i: (0, 0, 0)),
                  pl.BlockSpec((None, 1, nh), lambda i: (0, 0, 0))],
        out_specs=[pl.BlockSpec((tm, nh), lambda i: (i, 0)),
                   pl.BlockSpec((tm, nh), lambda i: (i, 0))],
        out_shape=[jax.ShapeDtypeStruct((m, nh), F32), jax.ShapeDtypeStruct((m, nh), F32)],
        compiler_params=_params(1),
        name=name,
    )(h, w_t, a_log.reshape(-1, 1, nh), dt_bias.reshape(-1, 1, nh))


def _delta_kernel(q_ref, k_ref, v_ref, z_ref, bt_ref, gc_ref, gct_ref, nw_ref,
                  o_ref, so_ref, s_scr, *, tc, hps, chunk):
    hg = pl.program_id(1)
    t = pl.program_id(2)
    nt = pl.num_programs(2)
    nh = bt_ref.shape[1]
    nck = tc // chunk
    n_factors = int(math.log2(chunk))

    @pl.when(t == 0)
    def _():
        s_scr[...] = jnp.zeros(s_scr.shape, F32)

    lane = lax.broadcasted_iota(jnp.int32, (chunk, nh), 1)
    ri = lax.broadcasted_iota(jnp.int32, (chunk, chunk), 0)
    ci = lax.broadcasted_iota(jnp.int32, (chunk, chunk), 1)
    tril = ci <= ri
    strict = ci < ri
    ri2 = lax.broadcasted_iota(jnp.int32, (chunk, 2 * chunk), 0)
    ci2 = lax.broadcasted_iota(jnp.int32, (chunk, 2 * chunk), 1)
    eye2 = jnp.where((ci2 == ri2) | (ci2 == ri2 + chunk), 1.0, 0.0).astype(F32)
    left = ci2 < chunk

    def blockdiag(xp):
        return jnp.concatenate([jnp.where(left, xp, 0.0), jnp.where(left, 0.0, xp)], axis=0)

    pairs = [(c, jq) for c in range(nck) for jq in range(hps // 2)]
    heads = [(c, hh) for c in range(nck) for hh in range(hps)]
    kc, qc, lowp, qkm, bcol, gcol = {}, {}, {}, {}, {}, {}
    for c, jq in pairs:
        rows = slice(c * chunk, (c + 1) * chunk)
        qcols = slice(jq * HEAD, (jq + 1) * HEAD)
        qc[c, jq] = q_ref[rows, qcols].astype(F32)
        kc[c, jq] = k_ref[rows, qcols].astype(F32)
        kkqk = lax.dot_general(jnp.concatenate([kc[c, jq], qc[c, jq]], axis=0), kc[c, jq],
                               NT_DIMS, preferred_element_type=F32)
        btb = bt_ref[rows, :]
        gcb = gc_ref[rows, :]
        lows = []
        for r in range(2):
            hh = 2 * jq + r
            hidx = hg * hps + hh
            pick = lane == hidx
            bcol[c, hh] = jnp.sum(jnp.where(pick, btb, 0.0), axis=1, keepdims=True)
            gcol[c, hh] = jnp.sum(jnp.where(pick, gcb, 0.0), axis=1, keepdims=True)
            grow = gct_ref[c, pl.ds(hidx, 1), :]
            decay = jnp.exp(jnp.where(tril, gcol[c, hh] - grow, -jnp.inf))
            lows.append(jnp.where(strict, kkqk[:chunk] * bcol[c, hh] * decay, 0.0))
            qkm[c, hh] = jnp.where(tril, kkqk[chunk:] * decay, 0.0)
        lowp[c, jq] = jnp.concatenate(lows, axis=1)

    tinv = {p: eye2 - lowp[p] for p in pairs}
    pw = {p: _dot(lowp[p], blockdiag(lowp[p])) for p in pairs}
    for f in range(n_factors - 1):
        for p in pairs:
            bd = blockdiag(pw[p])
            if f == n_factors - 2:
                tinv[p] = tinv[p] + _dot(tinv[p], bd)
            else:
                both = _dot(jnp.concatenate([tinv[p], pw[p]], axis=0), bd)
                tinv[p] = tinv[p] + both[:chunk]
                pw[p] = both[chunk:]

    uw, ecol = {}, {}
    for c, hh in heads:
        rows = slice(c * chunk, (c + 1) * chunk)
        vcols = slice(hh * HEAD, (hh + 1) * HEAD)
        ecol[c, hh] = jnp.exp(gcol[c, hh])
        vc = v_ref[rows, vcols].astype(F32)
        rhs = jnp.concatenate([vc * bcol[c, hh], kc[c, hh // 2] * (bcol[c, hh] * ecol[c, hh])], axis=1)
        r = hh % 2
        uw[c, hh] = _dot(tinv[c, hh // 2][:, r * chunk:(r + 1) * chunk], rhs)

    s = [s_scr[hh] for hh in range(hps)]
    for c in range(nck):
        rows = slice(c * chunk, (c + 1) * chunk)
        ws_qs = [_dot(jnp.concatenate([uw[c, hh][:, HEAD:], qc[c, hh // 2] * ecol[c, hh]], axis=0), s[hh])
                 for hh in range(hps)]
        vnew = [uw[c, hh][:, :HEAD] - ws_qs[hh][:chunk] for hh in range(hps)]
        outs = [ws_qs[hh][chunk:] + _dot(qkm[c, hh], vnew[hh]) for hh in range(hps)]
        for hh in range(hps):
            glast = gcol[c, hh][chunk - 1:chunk, :]
            kd = kc[c, hh // 2] * jnp.exp(glast - gcol[c, hh])
            s[hh] = s[hh] * jnp.exp(glast) + lax.dot_general(kd, vnew[hh], TN_DIMS,
                                                             preferred_element_type=F32)
        for hh in range(hps):
            vcols = slice(hh * HEAD, (hh + 1) * HEAD)
            o = outs[hh]
            og = (o * lax.rsqrt(jnp.mean(o * o, axis=-1, keepdims=True) + EPS) * nw_ref[...]
                  * _silu(z_ref[rows, vcols].astype(F32)))
            o_ref[rows, vcols] = og.astype(BF16)
    for hh in range(hps):
        s_scr[hh] = s[hh]

    @pl.when(t == nt - 1)
    def _():
        so_ref[...] = s_scr[...]


def _delta_prompt(q, k, v, z, beta, gc, norm_w, nb, seq):
    m = q.shape[0]
    chunk = DELTA_CHUNK
    tc = 256
    hps = 8
    nhg = N_V_HEADS // hps
    nt = seq // tc
    vw = hps * HEAD
    qw = vw // 2
    gct = gc.reshape(m // chunk, chunk, N_V_HEADS).transpose(0, 2, 1)
    return pl.pallas_call(
        functools.partial(_delta_kernel, tc=tc, hps=hps, chunk=chunk),
        grid=(nb, nhg, nt),
        in_specs=[pl.BlockSpec((tc, qw), lambda b, g, t: (b * nt + t, g)),
                  pl.BlockSpec((tc, qw), lambda b, g, t: (b * nt + t, g)),
                  pl.BlockSpec((tc, vw), lambda b, g, t: (b * nt + t, g)),
                  pl.BlockSpec((tc, vw), lambda b, g, t: (b * nt + t, g)),
                  pl.BlockSpec((tc, N_V_HEADS), lambda b, g, t: (b * nt + t, 0)),
                  pl.BlockSpec((tc, N_V_HEADS), lambda b, g, t: (b * nt + t, 0)),
                  pl.BlockSpec((tc // chunk, N_V_HEADS, chunk), lambda b, g, t: (b * nt + t, 0, 0)),
                  _const_spec((None, 1, HEAD), (0, 0, 0), 3)],
        out_specs=[pl.BlockSpec((tc, vw), lambda b, g, t: (b * nt + t, g)),
                   pl.BlockSpec((None, None, hps, HEAD, HEAD), lambda b, g, t: (0, b, g, 0, 0))],
        out_shape=[jax.ShapeDtypeStruct((m, N_V_HEADS * HEAD), BF16),
                   jax.ShapeDtypeStruct((1, nb, N_V_HEADS, HEAD, HEAD), F32)],
        scratch_shapes=[pltpu.VMEM((hps, HEAD, HEAD), F32)],
        compiler_params=_params(3),
        name="delta_prompt",
    )(q, k, v, z, beta, gc, gct, norm_w.reshape(-1, 1, HEAD))


def _delta_sample_kernel(p_ref, st_ref, cw_ref, bt_ref, g_ref, nw_ref, s_ref,
                         o_ref, sto_ref, so_ref, qk_scr, *, rows_per_step):
    cdim = st_ref.shape[2]
    key_dim = N_QK_HEADS * HEAD
    rep = N_V_HEADS // N_QK_HEADS
    hs = range(N_V_HEADS)

    def one_sequence(r, carry):
        xrow = p_ref[r, :, 0:cdim]
        conv = (cw_ref[3:4, :] * xrow + cw_ref[2:3, :] * st_ref[r, 2:3, :]
                + cw_ref[1:2, :] * st_ref[r, 1:2, :] + cw_ref[0:1, :] * st_ref[r, 0:1, :])
        sto_ref[r, 0:1, :] = st_ref[r, 1:2, :]
        sto_ref[r, 1:2, :] = st_ref[r, 2:3, :]
        sto_ref[r, 2:3, :] = xrow
        act = _silu(conv)
        qk_scr[...] = jnp.zeros(qk_scr.shape, F32)
        for j in range(2 * N_QK_HEADS):
            xs = act[:, j * HEAD:(j + 1) * HEAD]
            xs = xs * lax.rsqrt(jnp.sum(xs * xs, axis=-1, keepdims=True) + EPS)
            if j < N_QK_HEADS:
                xs = xs * (HEAD ** -0.5)
            qk_scr[j:j + 1, :] = xs
        qkt = qk_scr[...].T
        a_row = jnp.exp(g_ref[r])
        b_row = bt_ref[r]
        kcols = [qkt[:, N_QK_HEADS + h // rep:N_QK_HEADS + h // rep + 1] for h in hs]
        sk = [jnp.sum(s_ref[r, h] * kcols[h], axis=0, keepdims=True) for h in hs]
        vnew = [b_row[:, h:h + 1] * (act[:, 2 * key_dim + h * HEAD:2 * key_dim + (h + 1) * HEAD]
                                      - a_row[:, h:h + 1] * sk[h]) for h in hs]
        outs = []
        for h in hs:
            s_new = a_row[:, h:h + 1] * s_ref[r, h] + kcols[h] * vnew[h]
            so_ref[r, h] = s_new
            outs.append(jnp.sum(s_new * qkt[:, h // rep:h // rep + 1], axis=0, keepdims=True))
        for h in hs:
            o = outs[h]
            z = p_ref[r, :, cdim + h * HEAD:cdim + (h + 1) * HEAD]
            og = o * lax.rsqrt(jnp.mean(o * o, axis=-1, keepdims=True) + EPS) * nw_ref[...] * _silu(z)
            o_ref[r, :, h * HEAD:(h + 1) * HEAD] = og.astype(BF16)
        return carry

    lax.fori_loop(0, rows_per_step, one_sequence, 0)


def _delta_sample(proj, conv_state, conv_w, beta, g, norm_w, s0):
    nb, width = proj.shape
    npast, cdim = conv_state.shape[2:]
    vdim = N_V_HEADS * HEAD
    rb = 4
    return pl.pallas_call(
        functools.partial(_delta_sample_kernel, rows_per_step=rb),
        grid=(nb // rb,),
        in_specs=[pl.BlockSpec((rb, 1, width), lambda b: (b, 0, 0)),
                  pl.BlockSpec((None, rb, npast, cdim), lambda b: (0, b, 0, 0)),
                  _const_spec((None,) + conv_w.shape[1:], (0, 0, 0), 1),
                  pl.BlockSpec((rb, 1, N_V_HEADS), lambda b: (b, 0, 0)),
                  pl.BlockSpec((rb, 1, N_V_HEADS), lambda b: (b, 0, 0)),
                  _const_spec((None, 1, HEAD), (0, 0, 0), 1),
                  pl.BlockSpec((None, rb, N_V_HEADS, HEAD, HEAD), lambda b: (0, b, 0, 0, 0))],
        out_specs=[pl.BlockSpec((rb, 1, vdim), lambda b: (b, 0, 0)),
                   pl.BlockSpec((None, rb, npast, cdim), lambda b: (0, b, 0, 0)),
                   pl.BlockSpec((None, rb, N_V_HEADS, HEAD, HEAD), lambda b: (0, b, 0, 0, 0))],
        out_shape=[jax.ShapeDtypeStruct((nb, 1, vdim), BF16),
                   jax.ShapeDtypeStruct((1,) + conv_state.shape[1:], F32),
                   jax.ShapeDtypeStruct((1,) + s0.shape[1:], F32)],
        scratch_shapes=[pltpu.VMEM((HEAD, HEAD), F32)],
        compiler_params=_params(1),
        name="delta_sample",
    )(proj.reshape(nb, 1, width), conv_state, conv_w, beta.reshape(nb, 1, N_V_HEADS),
      g.reshape(nb, 1, N_V_HEADS), norm_w.reshape(-1, 1, HEAD), s0)


def kernel(x_prompt, x_sample, c_prompt, c_sample, cache_pool, state_conv, state_rec, cache_ffn_conv,
           norm_w, ada_w, ada_b, pool_w, pool_scale, dn_w_in, dn_conv_w, dn_a_log, dn_dt_bias, dn_norm_w,
           dn_w_out, ffn_w_gate, ffn_w_up, ffn_conv_w, ffn_w_down):
    bp, seq, d = x_prompt.shape
    bs = x_sample.shape[0]
    key_dim = N_QK_HEADS * HEAD
    val_dim = N_V_HEADS * HEAD
    conv_dim = 2 * key_dim + val_dim
    assert bs == 128 and bp <= 8 and x_sample.shape[1] == 1
    assert ada_w.shape[0] == 2 and dn_w_in.shape[0] == 1 and pool_w.shape[0] == 1

    c_all = jnp.concatenate([c_sample, c_prompt, jnp.zeros((MOD_ROWS - bs - bp, d), F32)], axis=0)
    mods = _ada(c_all, ada_w, ada_b)

    xp = x_prompt.reshape(bp * seq, d)
    xs = x_sample.reshape(bs, d)

    xp, hp, pool_p16 = _pool_prompt(xp, mods, 0, norm_w, pool_w, pool_scale, seq)
    xs, hs, pool_st = _pool_sample(xs, mods, 0, norm_w, pool_w, pool_scale,
                                   cache_pool.transpose(0, 2, 1, 3))
    pool_p = pool_p16[:, 16 - POOL_STATE:][None]
    pool_s = pool_st.transpose(1, 0, 2)[None]

    ffn_p, ffn_s = [], []
    ffn_cache_t = cache_ffn_conv.transpose(0, 2, 1, 3)
    w_down = _to_bf16(ffn_w_down, "cast_w_down")
    w_out = _to_bf16(dn_w_out, "cast_w_out")

    def ffn(layer, xp, hp, xs, hs, nxt):
        ap, stp, a_s, sts = _ffn_a(hp, hs, layer, ffn_w_gate, ffn_w_up, ffn_conv_w, ffn_cache_t, seq)
        ffn_p.append(stp[:, 6:8])
        ffn_s.append(sts.transpose(1, 0, 2))
        xp, hp = _mm_epi(ap, w_down, layer, xp, mods, norm_w, layer, 5, 3, nxt,
                         per_row=False, seq=seq, name="ffn_b_prompt")
        xs, hs = _mm_epi(a_s, w_down, layer, xs, mods, norm_w, layer, 5, 3, nxt,
                         per_row=True, seq=1, name="ffn_b_sample")
        return xp, hp, xs, hs

    xp, hp, xs, hs = ffn(0, xp, hp, xs, hs, (1, 0, 0, 1, BF16))

    n_main = conv_dim + val_dim
    w_t = jnp.swapaxes(dn_w_in, 1, 2)
    qn, st_q, pq = _proj(hp, hs, w_t, dn_conv_w, 0, key_dim, seq, "qk", HEAD ** -0.5, "proj_q")
    kn, st_k, pk = _proj(hp, hs, w_t, dn_conv_w, key_dim, key_dim, seq, "qk", 1.0, "proj_k")
    vv, st_v, pv = _proj(hp, hs, w_t, dn_conv_w, 2 * key_dim, val_dim, seq, "v", 1.0, "proj_v")
    zz, pz = _proj(hp, hs, w_t, dn_conv_w, conv_dim, val_dim, seq, "z", 1.0, "proj_z")
    proj_s = jnp.concatenate([pq, pk, pv, pz], axis=1)
    beta_p, gc_p = _gates(hp, w_t, n_main, dn_a_log, dn_dt_bias, DELTA_CHUNK, "gates_prompt")
    beta_s, g_s = _gates(hs, w_t, n_main, dn_a_log, dn_dt_bias, 1, "gates_sample")
    conv_p = jnp.concatenate([st_q, st_k, st_v], axis=2)[:, 5:8][None]
    o_p, rec_p = _delta_prompt(qn, kn, vv, zz, beta_p, gc_p, dn_norm_w, bp, seq)
    o_s, conv_s, rec_s = _delta_sample(proj_s, state_conv, dn_conv_w, beta_s, g_s, dn_norm_w, state_rec)
    nxt = (1, 2, 3, 4, BF16)
    xp, hp = _mm_epi(o_p, w_out, 0, xp, mods, norm_w, 1, 2, 1, nxt,
                     per_row=False, seq=seq, name="out_prompt")
    xs, hs = _mm_epi(o_s.reshape(bs, val_dim), w_out, 0, xs, mods, norm_w, 1, 2, 1, nxt,
                     per_row=True, seq=1, name="out_sample")
    xp, _, xs, _ = ffn(1, xp, hp, xs, hs, None)

    return (xp.reshape(bp, seq, d), xs.reshape(bs, 1, d), pool_p, pool_s, conv_p, conv_s,
            rec_p, rec_s, jnp.stack(ffn_p), jnp.stack(ffn_s))
```

```python
import functools
import math

import jax
import jax.numpy as jnp
from jax import lax
from jax.experimental import pallas as pl
from jax.experimental.pallas import tpu as pltpu

F32 = jnp.float32
BF16 = jnp.bfloat16
EPS = 1e-6
POOL_WINDOWS = (2, 4, 8, 16)
POOL_STATE = max(POOL_WINDOWS) - 1
HEAD = 128
N_QK_HEADS = 16
N_V_HEADS = 32
DELTA_CHUNK = 128
PAST_LEN = 16384
VMEM_LIMIT = 56 * 1024 * 1024
MOD_ROWS = 136
PROMPT_MOD_BLOCK = 16
NT_DIMS = (((1,), (1,)), ((), ()))
TN_DIMS = (((0,), (0,)), ((), ()))


def _params(n_axes, vmem=VMEM_LIMIT):
    return pltpu.CompilerParams(dimension_semantics=("arbitrary",) * n_axes, vmem_limit_bytes=vmem)


def _rms(x, w):
    return x * lax.rsqrt(jnp.mean(x * x, axis=-1, keepdims=True) + EPS) * w


def _silu(x):
    return x * jax.nn.sigmoid(x)


def _dot(a, b):
    return jnp.dot(a, b, preferred_element_type=F32)


def _ada_kernel(c_ref, w_ref, b_ref, o_ref):
    o_ref[...] = _dot(_silu(c_ref[...]), w_ref[...]) + b_ref[...]


def _ada(c_all, ada_w, ada_b):
    depth, d, n = ada_w.shape
    tn = 1024
    return pl.pallas_call(
        _ada_kernel,
        grid=(depth, n // tn),
        in_specs=[pl.BlockSpec((MOD_ROWS, d), lambda l, j: (0, 0)),
                  pl.BlockSpec((None, d, tn), lambda l, j: (l, 0, j)),
                  pl.BlockSpec((None, 1, tn), lambda l, j: (l, 0, j))],
        out_specs=pl.BlockSpec((None, MOD_ROWS, tn), lambda l, j: (l, 0, j)),
        out_shape=jax.ShapeDtypeStruct((depth, MOD_ROWS, n), F32),
        compiler_params=_params(2),
        name="ada_mod",
    )(c_all, ada_w, ada_b.reshape(depth, 1, n))


def _const_spec(block, idx, n_axes):
    if n_axes == 1:
        return pl.BlockSpec(block, lambda i: idx)
    if n_axes == 2:
        return pl.BlockSpec(block, lambda i, j: idx)
    return pl.BlockSpec(block, lambda i, j, k: idx)


def _mod_spec_prompt(d, layer, chunk, n_axes):
    return _const_spec((None, 8, d), (layer, PROMPT_MOD_BLOCK, chunk), n_axes)


def _mod_spec_sample(d, rows, layer, chunk, n_axes):
    if n_axes == 1:
        return pl.BlockSpec((None, rows, d), lambda i: (layer, i, chunk))
    return pl.BlockSpec((None, rows, d), lambda i, j: (layer, i, chunk))


def _pool_groups(h, window_sum, recip_cnt, pw_ref, ps_ref, gd):
    ys = []
    for gi, w in enumerate(POOL_WINDOWS):
        c0 = gi * gd
        dm = window_sum(w, c0) * recip_cnt(w) - h[:, c0:c0 + gd]
        ys.append(_dot(dm, pw_ref[gi]))
    return jnp.concatenate(ys, axis=1) * ps_ref[...]


def _pool_prompt_kernel(x_ref, shm, scm, gtm, shf, scf, nw_ref, pw_ref, ps_ref,
                        xo_ref, hf_ref, st_ref, hp, *, tm, tiles_per_b, n_past):
    i = pl.program_id(0)
    b = i // tiles_per_b
    t = i % tiles_per_b
    d = x_ref.shape[1]
    gd = d // len(POOL_WINDOWS)

    def row(r):
        return r[pl.ds(b, 1), :]

    x = x_ref[...]
    h = _rms(x, nw_ref[0:1, :]) * (1.0 + row(scm)) + row(shm)

    @pl.when(t == 0)
    def _():
        hp[0:16, :] = jnp.zeros((16, d), F32)

    hp[16:16 + tm, :] = h
    pos = t * tm + lax.broadcasted_iota(jnp.int32, (tm, 1), 0) + n_past

    def window_sum(w, c0):
        acc = hp[:, c0:c0 + gd]
        span = 1
        while span < w:
            acc = acc + pltpu.roll(acc, span, 0)
            span *= 2
        return acc[16:16 + tm]

    def recip_cnt(w):
        return 1.0 / jnp.minimum(pos + 1, w).astype(F32)

    y = _pool_groups(h, window_sum, recip_cnt, pw_ref, ps_ref, gd)
    tail = hp[tm:tm + 16, :]
    st_ref[...] = tail
    hp[0:16, :] = tail
    xn = x + row(gtm) * _rms(y, nw_ref[1:2, :])
    xo_ref[...] = xn
    hf_ref[...] = (_rms(xn, nw_ref[2:3, :]) * (1.0 + row(scf)) + row(shf)).astype(BF16)


def _pool_prompt(x2, mods, layer, norm_w, pool_w, pool_scale, seq):
    m, d = x2.shape
    j = layer // 2
    tm = 256
    tiles_per_b = seq // tm
    nb = m // seq
    ms = lambda c: _mod_spec_prompt(d, layer, c, 1)
    return pl.pallas_call(
        functools.partial(_pool_prompt_kernel, tm=tm, tiles_per_b=tiles_per_b, n_past=0),
        grid=(m // tm,),
        in_specs=[pl.BlockSpec((tm, d), lambda i: (i, 0)),
                  ms(0), ms(1), ms(2), ms(3), ms(4),
                  _const_spec((None,) + norm_w.shape[1:], (layer, 0, 0), 1),
                  _const_spec((None,) + pool_w.shape[1:], (j, 0, 0, 0), 1),
                  _const_spec((None, 1, d), (j, 0, 0), 1)],
        out_specs=[pl.BlockSpec((tm, d), lambda i: (i, 0)),
                   pl.BlockSpec((tm, d), lambda i: (i, 0)),
                   pl.BlockSpec((None, 16, d), lambda i: (i // tiles_per_b, 0, 0))],
        out_shape=[jax.ShapeDtypeStruct((m, d), F32),
                   jax.ShapeDtypeStruct((m, d), BF16),
                   jax.ShapeDtypeStruct((nb, 16, d), F32)],
        scratch_shapes=[pltpu.VMEM((tm + 16, d), F32)],
        compiler_params=_params(1),
        name="pool_prompt",
    )(x2, mods, mods, mods, mods, mods, norm_w, pool_w, pool_scale.reshape(-1, 1, d))


def _pool_sample_kernel(x_ref, shm, scm, gtm, shf, scf, nw_ref, pw_ref, ps_ref, c_ref,
                        xo_ref, hf_ref, co_ref, *, n_past):
    d = x_ref.shape[1]
    gd = d // len(POOL_WINDOWS)
    x = x_ref[...]
    h = _rms(x, nw_ref[0:1, :]) * (1.0 + scm[...]) + shm[...]

    def window_sum(w, c0):
        acc = h[:, c0:c0 + gd]
        for j in range(1, w):
            acc = acc + c_ref[POOL_STATE - j, :, c0:c0 + gd]
        return acc

    def recip_cnt(w):
        return 1.0 / float(min(n_past + 1, w))

    y = _pool_groups(h, window_sum, recip_cnt, pw_ref, ps_ref, gd)
    for r in range(POOL_STATE - 1):
        co_ref[r] = c_ref[r + 1]
    co_ref[POOL_STATE - 1] = h
    xn = x + gtm[...] * _rms(y, nw_ref[1:2, :])
    xo_ref[...] = xn
    hf_ref[...] = (_rms(xn, nw_ref[2:3, :]) * (1.0 + scf[...]) + shf[...]).astype(BF16)


def _pool_sample(x2, mods, layer, norm_w, pool_w, pool_scale, cache_t):
    m, d = x2.shape
    j = layer // 2
    bb = 32
    ms = lambda c: _mod_spec_sample(d, bb, layer, c, 1)
    return pl.pallas_call(
        functools.partial(_pool_sample_kernel, n_past=PAST_LEN),
        grid=(m // bb,),
        in_specs=[pl.BlockSpec((bb, d), lambda i: (i, 0)),
                  ms(0), ms(1), ms(2), ms(3), ms(4),
                  _const_spec((None,) + norm_w.shape[1:], (layer, 0, 0), 1),
                  _const_spec((None,) + pool_w.shape[1:], (j, 0, 0, 0), 1),
                  _const_spec((None, 1, d), (j, 0, 0), 1),
                  pl.BlockSpec((None, POOL_STATE, bb, d), lambda i: (j, 0, i, 0))],
        out_specs=[pl.BlockSpec((bb, d), lambda i: (i, 0)),
                   pl.BlockSpec((bb, d), lambda i: (i, 0)),
                   pl.BlockSpec((POOL_STATE, bb, d), lambda i: (0, i, 0))],
        out_shape=[jax.ShapeDtypeStruct((m, d), F32),
                   jax.ShapeDtypeStruct((m, d), BF16),
                   jax.ShapeDtypeStruct((POOL_STATE, m, d), F32)],
        compiler_params=_params(1),
        name="pool_sample",
    )(x2, mods, mods, mods, mods, mods, norm_w, pool_w, pool_scale.reshape(-1, 1, d), cache_t)


def _ffn_a_kernel(h_ref, hs_ref, wg_ref, wu_ref, cw_ref, cs_ref,
                  a_ref, st_ref, as_ref, cso_ref, wgb, wub, carry, *bufs, tm, tiles_per_b, nsub):
    i = pl.program_id(1)
    t = i % tiles_per_b
    sub = tm // nsub
    gbufs, ubufs = bufs[:nsub], bufs[nsub:]
    cw0, cw1, cw2 = cw_ref[0:1, :], cw_ref[1:2, :], cw_ref[2:3, :]

    @pl.when(i == 0)
    def _():
        wgb[...] = wg_ref[...].astype(BF16)
        wub[...] = wu_ref[...].astype(BF16)
        hs = hs_ref[...]
        gs = _dot(hs, wgb[...])
        conv = cw2 * gs + cw1 * cs_ref[1] + cw0 * cs_ref[0]
        cso_ref[0] = cs_ref[1]
        cso_ref[1] = gs
        as_ref[...] = (_silu(conv) * _dot(hs, wub[...])).astype(BF16)

    @pl.when(t == 0)
    def _():
        carry[...] = jnp.zeros(carry.shape, F32)

    def matmuls(s):
        h = h_ref[s * sub:(s + 1) * sub, :]
        gbufs[s][8:8 + sub, :] = _dot(h, wgb[...])
        gbufs[s][0:8, :] = carry[...] if s == 0 else gbufs[s - 1][sub:sub + 8, :]
        ubufs[s][...] = _dot(h, wub[...])

    def epilogue(s):
        g = gbufs[s]
        conv = cw2 * g[8:8 + sub, :] + cw1 * g[pl.ds(7, sub), :] + cw0 * g[pl.ds(6, sub), :]
        a_ref[s * sub:(s + 1) * sub, :] = (_silu(conv) * ubufs[s][...]).astype(BF16)

    matmuls(0)
    for s in range(nsub):
        if s + 1 < nsub:
            matmuls(s + 1)
        epilogue(s)
    tail = gbufs[nsub - 1][sub:sub + 8, :]
    st_ref[...] = tail
    carry[...] = tail


def _ffn_a(h, hs, layer, wg, wu, cw, cache_t, seq):
    m, d = h.shape
    ms = hs.shape[0]
    f = wg.shape[2]
    tm, tf = 1024, 512
    nsub = 4
    sub = tm // nsub
    tiles_per_b = seq // tm
    nb = m // seq
    return pl.pallas_call(
        functools.partial(_ffn_a_kernel, tm=tm, tiles_per_b=tiles_per_b, nsub=nsub),
        grid=(f // tf, m // tm),
        in_specs=[pl.BlockSpec((tm, d), lambda j, i: (i, 0)),
                  pl.BlockSpec((ms, d), lambda j, i: (0, 0)),
                  pl.BlockSpec((None, d, tf), lambda j, i: (layer, 0, j)),
                  pl.BlockSpec((None, d, tf), lambda j, i: (layer, 0, j)),
                  pl.BlockSpec((None, cw.shape[1], tf), lambda j, i: (layer, 0, j)),
                  pl.BlockSpec((None, 2, ms, tf), lambda j, i: (layer, 0, 0, j))],
        out_specs=[pl.BlockSpec((tm, tf), lambda j, i: (i, j)),
                   pl.BlockSpec((None, 8, tf), lambda j, i: (i // tiles_per_b, 0, j)),
                   pl.BlockSpec((ms, tf), lambda j, i: (0, j)),
                   pl.BlockSpec((2, ms, tf), lambda j, i: (0, 0, j))],
        out_shape=[jax.ShapeDtypeStruct((m, f), BF16),
                   jax.ShapeDtypeStruct((nb, 8, f), F32),
                   jax.ShapeDtypeStruct((ms, f), BF16),
                   jax.ShapeDtypeStruct((2, ms, f), F32)],
        scratch_shapes=([pltpu.VMEM((d, tf), BF16), pltpu.VMEM((d, tf), BF16), pltpu.VMEM((8, tf), F32)]
                        + [pltpu.VMEM((8 + sub, tf), F32)] * nsub + [pltpu.VMEM((sub, tf), F32)] * nsub),
        compiler_params=_params(2),
        name="ffn_a",
    )(h, hs, wg, wu, cw, cache_t)


def _cast_kernel(x_ref, o_ref):
    o_ref[...] = x_ref[...].astype(o_ref.dtype)


def _to_bf16(w, name):
    nl, k, n = w.shape
    tk = 512
    return pl.pallas_call(
        _cast_kernel,
        grid=(nl, k // tk),
        in_specs=[pl.BlockSpec((None, tk, n), lambda l, i: (l, i, 0))],
        out_specs=pl.BlockSpec((None, tk, n), lambda l, i: (l, i, 0)),
        out_shape=jax.ShapeDtypeStruct(w.shape, BF16),
        compiler_params=_params(2),
        name=name,
    )(w)


def _mm_epi_kernel(*refs, per_row, tiles_per_b, post_idx, next_idx, next_dtype):
    has_next = next_idx is not None
    if has_next:
        a_ref, w_ref, x_ref, gt, nwp, shn, scn, nwn, xo_ref, ho_ref = refs
    else:
        a_ref, w_ref, x_ref, gt, nwp, xo_ref = refs
    b = pl.program_id(0) // tiles_per_b

    def sel(r):
        return r[...] if per_row else r[pl.ds(b, 1), :]

    def unit(v):
        return v * lax.rsqrt(jnp.mean(v * v, axis=-1, keepdims=True) + EPS)

    y = _dot(a_ref[...], w_ref[...])
    xn = x_ref[...] + unit(y) * (sel(gt) * nwp[post_idx:post_idx + 1, :])
    xo_ref[...] = xn
    if has_next:
        scale = nwn[next_idx:next_idx + 1, :] * (1.0 + sel(scn))
        ho_ref[...] = (unit(xn) * scale + sel(shn)).astype(next_dtype)


def _mm_epi(a, w, wl, x, mods, norm_w, layer, gate_chunk, post_idx, nxt, *, per_row, seq, name):
    m, kdim = a.shape
    d = w.shape[2]
    tm = m if per_row else 256
    assert w.dtype == BF16
    tiles_per_b = 1 if per_row else seq // tm
    if per_row:
        ms = lambda l, c: _mod_spec_sample(d, tm, l, c, 1)
    else:
        ms = lambda l, c: _mod_spec_prompt(d, l, c, 1)
    nw_spec = lambda l: _const_spec((None,) + norm_w.shape[1:], (l, 0, 0), 1)
    in_specs = [pl.BlockSpec((tm, kdim), lambda i: (i, 0)),
                pl.BlockSpec((None, kdim, d), lambda i: (wl, 0, 0), pipeline_mode=pl.Buffered(1)),
                pl.BlockSpec((tm, d), lambda i: (i, 0)),
                ms(layer, gate_chunk), nw_spec(layer)]
    args = [a, w, x, mods, norm_w]
    out_specs = [pl.BlockSpec((tm, d), lambda i: (i, 0))]
    out_shape = [jax.ShapeDtypeStruct((m, d), F32)]
    next_idx = next_dtype = None
    if nxt is not None:
        nlayer, next_idx, sh_chunk, sc_chunk, next_dtype = nxt
        in_specs += [ms(nlayer, sh_chunk), ms(nlayer, sc_chunk), nw_spec(nlayer)]
        args += [mods, mods, norm_w]
        out_specs.append(pl.BlockSpec((tm, d), lambda i: (i, 0)))
        out_shape.append(jax.ShapeDtypeStruct((m, d), next_dtype))
    res = pl.pallas_call(
        functools.partial(_mm_epi_kernel, per_row=per_row, tiles_per_b=tiles_per_b,
                          post_idx=post_idx, next_idx=next_idx, next_dtype=next_dtype),
        grid=(m // tm,),
        in_specs=in_specs,
        out_specs=out_specs,
        out_shape=out_shape,
        compiler_params=_params(1),
        name=name,
    )(*args)
    return res if nxt is not None else (res[0], None)


def _proj_kernel(*refs, tm, tiles_per_b, nsub, mode, scale):
    if mode == "z":
        h_ref, hs_ref, w_ref, o_ref, ps_ref, wb = refs
    else:
        h_ref, hs_ref, w_ref, cw_ref, o_ref, st_ref, ps_ref, wb, carry = refs[:9]
        bufs = refs[9:]
    i = pl.program_id(1)
    t = i % tiles_per_b
    sub = tm // nsub
    tn = wb.shape[1]

    @pl.when(i == 0)
    def _():
        wb[...] = w_ref[...].T.astype(BF16)
        ps_ref[...] = _dot(hs_ref[...], wb[...])

    if mode == "z":
        o_ref[...] = _dot(h_ref[...], wb[...]).astype(BF16)
        return

    @pl.when(t == 0)
    def _():
        carry[...] = jnp.zeros(carry.shape, F32)

    def matmul(s):
        bufs[s][8:8 + sub, :] = _dot(h_ref[s * sub:(s + 1) * sub, :], wb[...])
        bufs[s][0:8, :] = carry[...] if s == 0 else bufs[s - 1][sub:sub + 8, :]

    def epilogue(s):
        p = bufs[s]
        conv = (cw_ref[3:4, :] * p[8:8 + sub, :] + cw_ref[2:3, :] * p[pl.ds(7, sub), :]
                + cw_ref[1:2, :] * p[pl.ds(6, sub), :] + cw_ref[0:1, :] * p[pl.ds(5, sub), :])
        act = _silu(conv)
        if mode == "qk":
            outs = []
            for hh in range(tn // HEAD):
                xs = act[:, hh * HEAD:(hh + 1) * HEAD]
                outs.append(xs * (lax.rsqrt(jnp.sum(xs * xs, axis=-1, keepdims=True) + EPS) * scale))
            act = jnp.concatenate(outs, axis=1)
        o_ref[s * sub:(s + 1) * sub, :] = act.astype(BF16)

    matmul(0)
    for s in range(nsub):
        if s + 1 < nsub:
            matmul(s + 1)
        epilogue(s)
    tail = bufs[nsub - 1][sub:sub + 8, :]
    st_ref[...] = tail
    carry[...] = tail


def _proj(h, hs, w_t, conv_w, col0, width, seq, mode, scale, name):
    m, d = h.shape
    ms = hs.shape[0]
    tm, tn = 1024, 1024
    nsub = 8
    tiles_per_b = seq // tm
    nb = m // seq
    cb = col0 // tn
    conv = mode != "z"
    in_specs = [pl.BlockSpec((tm, d), lambda j, i: (i, 0)),
                pl.BlockSpec((ms, d), lambda j, i: (0, 0)),
                pl.BlockSpec((None, tn, d), lambda j, i: (0, cb + j, 0))]
    args = [h, hs, w_t]
    out_specs = [pl.BlockSpec((tm, tn), lambda j, i: (i, j))]
    out_shape = [jax.ShapeDtypeStruct((m, width), BF16)]
    scratch = [pltpu.VMEM((d, tn), BF16)]
    if conv:
        in_specs.append(pl.BlockSpec((None, conv_w.shape[1], tn), lambda j, i: (0, 0, cb + j)))
        args.append(conv_w)
        out_specs.append(pl.BlockSpec((None, 8, tn), lambda j, i: (i // tiles_per_b, 0, j)))
        out_shape.append(jax.ShapeDtypeStruct((nb, 8, width), F32))
        scratch += [pltpu.VMEM((8, tn), F32)] + [pltpu.VMEM((8 + tm // nsub, tn), F32)] * nsub
    out_specs.append(pl.BlockSpec((ms, tn), lambda j, i: (0, j)))
    out_shape.append(jax.ShapeDtypeStruct((ms, width), F32))
    return pl.pallas_call(
        functools.partial(_proj_kernel, tm=tm, tiles_per_b=tiles_per_b, nsub=nsub, mode=mode, scale=scale),
        grid=(width // tn, m // tm),
        in_specs=in_specs,
        out_specs=out_specs,
        out_shape=out_shape,
        scratch_shapes=scratch,
        compiler_params=_params(2),
        name=name,
    )(*args)


def _gates_kernel(h_ref, w_ref, al_ref, dt_ref, beta_ref, gc_ref, *, tm, chunk):
    nh = al_ref.shape[1]
    ba = lax.dot_general(h_ref[...], w_ref[...].astype(BF16), NT_DIMS, preferred_element_type=F32)
    beta_ref[...] = jax.nn.sigmoid(ba[:, :nh])
    xx = ba[:, nh:] + dt_ref[...]
    softplus = jnp.maximum(xx, 0.0) + jnp.log1p(jnp.exp(-jnp.abs(xx)))
    g = -jnp.exp(al_ref[...]) * softplus
    shift = int(math.log2(chunk))
    r = lax.broadcasted_iota(jnp.int32, (tm, tm), 0)
    c = lax.broadcasted_iota(jnp.int32, (tm, tm), 1)
    tri = jnp.where(((r >> shift) == (c >> shift)) & (c <= r), 1.0, 0.0).astype(F32)
    gc_ref[...] = jnp.dot(tri, g, precision=lax.Precision.HIGHEST, preferred_element_type=F32)


def _gates(h, w_t, row0, a_log, dt_bias, chunk, name):
    m, d = h.shape
    nh = a_log.shape[1]
    tm = min(m, 256)
    assert row0 % (2 * nh) == 0 and w_t.shape[1] - row0 == 2 * nh
    return pl.pallas_call(
        functools.partial(_gates_kernel, tm=tm, chunk=chunk),
        grid=(m // tm,),
        in_specs=[pl.BlockSpec((tm, d), lambda i: (i, 0)),
                  pl.BlockSpec((None, 2 * nh, d), lambda i: (0, row0 // (2 * nh), 0)),
                  pl.BlockSpec((None, 1, nh), lambda i: (0, 0, 0)),
                  pl.BlockSpec((None, 1, nh), lambda i: (0, 0, 0))],
        out_specs=[pl.BlockSpec((tm, nh), lambda i: (i, 0)),
                   pl.BlockSpec((tm, nh), lambda i: (i, 0))],
        out_shape=[jax.ShapeDtypeStruct((m, nh), F32), jax.ShapeDtypeStruct((m, nh), F32)],
        compiler_params=_params(1),
        name=name,
    )(h, w_t, a_log.reshape(-1, 1, nh), dt_bias.reshape(-1, 1, nh))


def _delta_kernel(q_ref, k_ref, v_ref, z_ref, bt_ref, gc_ref, gct_ref, nw_ref,
                  o_ref, so_ref, s_scr, *, tc, hps, chunk):
    hg = pl.program_id(1)
    t = pl.program_id(2)
    nt = pl.num_programs(2)
    nh = bt_ref.shape[1]
    nck = tc // chunk
    n_factors = int(math.log2(chunk))

    @pl.when(t == 0)
    def _():
        s_scr[...] = jnp.zeros(s_scr.shape, F32)

    lane = lax.broadcasted_iota(jnp.int32, (chunk, nh), 1)
    ri = lax.broadcasted_iota(jnp.int32, (chunk, chunk), 0)
    ci = lax.broadcasted_iota(jnp.int32, (chunk, chunk), 1)
    tril = ci <= ri
    strict = ci < ri
    ri2 = lax.broadcasted_iota(jnp.int32, (chunk, 2 * chunk), 0)
    ci2 = lax.broadcasted_iota(jnp.int32, (chunk, 2 * chunk), 1)
    eye2 = jnp.where((ci2 == ri2) | (ci2 == ri2 + chunk), 1.0, 0.0).astype(F32)
    left = ci2 < chunk

    def blockdiag(xp):
        return jnp.concatenate([jnp.where(left, xp, 0.0), jnp.where(left, 0.0, xp)], axis=0)

    pairs = [(c, jq) for c in range(nck) for jq in range(hps // 2)]
    heads = [(c, hh) for c in range(nck) for hh in range(hps)]
    kc, qc, lowp, qkm, bcol, gcol = {}, {}, {}, {}, {}, {}
    for c, jq in pairs:
        rows = slice(c * chunk, (c + 1) * chunk)
        qcols = slice(jq * HEAD, (jq + 1) * HEAD)
        qc[c, jq] = q_ref[rows, qcols].astype(F32)
        kc[c, jq] = k_ref[rows, qcols].astype(F32)
        kkqk = lax.dot_general(jnp.concatenate([kc[c, jq], qc[c, jq]], axis=0), kc[c, jq],
                               NT_DIMS, preferred_element_type=F32)
        btb = bt_ref[rows, :]
        gcb = gc_ref[rows, :]
        lows = []
        for r in range(2):
            hh = 2 * jq + r
            hidx = hg * hps + hh
            pick = lane == hidx
            bcol[c, hh] = jnp.sum(jnp.where(pick, btb, 0.0), axis=1, keepdims=True)
            gcol[c, hh] = jnp.sum(jnp.where(pick, gcb, 0.0), axis=1, keepdims=True)
            grow = gct_ref[c, pl.ds(hidx, 1), :]
            decay = jnp.exp(jnp.where(tril, gcol[c, hh] - grow, -jnp.inf))
            lows.append(jnp.where(strict, kkqk[:chunk] * bcol[c, hh] * decay, 0.0))
            qkm[c, hh] = jnp.where(tril, kkqk[chunk:] * decay, 0.0)
        lowp[c, jq] = jnp.concatenate(lows, axis=1)

    tinv = {p: eye2 - lowp[p] for p in pairs}
    pw = {p: _dot(lowp[p], blockdiag(lowp[p])) for p in pairs}
    for f in range(n_factors - 1):
        for p in pairs:
            bd = blockdiag(pw[p])
            if f == n_factors - 2:
                tinv[p] = tinv[p] + _dot(tinv[p], bd)
            else:
                both = _dot(jnp.concatenate([tinv[p], pw[p]], axis=0), bd)
                tinv[p] = tinv[p] + both[:chunk]
                pw[p] = both[chunk:]

    uw, ecol = {}, {}
    for c, hh in heads:
        rows = slice(c * chunk, (c + 1) * chunk)
        vcols = slice(hh * HEAD, (hh + 1) * HEAD)
        ecol[c, hh] = jnp.exp(gcol[c, hh])
        vc = v_ref[rows, vcols].astype(F32)
        rhs = jnp.concatenate([vc * bcol[c, hh], kc[c, hh // 2] * (bcol[c, hh] * ecol[c, hh])], axis=1)
        r = hh % 2
        uw[c, hh] = _dot(tinv[c, hh // 2][:, r * chunk:(r + 1) * chunk], rhs)

    s = [s_scr[hh] for hh in range(hps)]
    for c in range(nck):
        rows = slice(c * chunk, (c + 1) * chunk)
        ws_qs = [_dot(jnp.concatenate([uw[c, hh][:, HEAD:], qc[c, hh // 2] * ecol[c, hh]], axis=0), s[hh])
                 for hh in range(hps)]
        vnew = [uw[c, hh][:, :HEAD] - ws_qs[hh][:chunk] for hh in range(hps)]
        outs = [ws_qs[hh][chunk:] + _dot(qkm[c, hh], vnew[hh]) for hh in range(hps)]
        for hh in range(hps):
            glast = gcol[c, hh][chunk - 1:chunk, :]
            kd = kc[c, hh // 2] * jnp.exp(glast - gcol[c, hh])
            s[hh] = s[hh] * jnp.exp(glast) + lax.dot_general(kd, vnew[hh], TN_DIMS,
                                                             preferred_element_type=F32)
        for hh in range(hps):
            vcols = slice(hh * HEAD, (hh + 1) * HEAD)
            o = outs[hh]
            og = (o * lax.rsqrt(jnp.mean(o * o, axis=-1, keepdims=True) + EPS) * nw_ref[...]
                  * _silu(z_ref[rows, vcols].astype(F32)))
            o_ref[rows, vcols] = og.astype(BF16)
    for hh in range(hps):
        s_scr[hh] = s[hh]

    @pl.when(t == nt - 1)
    def _():
        so_ref[...] = s_scr[...]


def _delta_prompt(q, k, v, z, beta, gc, norm_w, nb, seq):
    m = q.shape[0]
    chunk = DELTA_CHUNK
    tc = 256
    hps = 8
    nhg = N_V_HEADS // hps
    nt = seq // tc
    vw = hps * HEAD
    qw = vw // 2
    gct = gc.reshape(m // chunk, chunk, N_V_HEADS).transpose(0, 2, 1)
    return pl.pallas_call(
        functools.partial(_delta_kernel, tc=tc, hps=hps, chunk=chunk),
        grid=(nb, nhg, nt),
        in_specs=[pl.BlockSpec((tc, qw), lambda b, g, t: (b * nt + t, g)),
                  pl.BlockSpec((tc, qw), lambda b, g, t: (b * nt + t, g)),
                  pl.BlockSpec((tc, vw), lambda b, g, t: (b * nt + t, g)),
                  pl.BlockSpec((tc, vw), lambda b, g, t: (b * nt + t, g)),
                  pl.BlockSpec((tc, N_V_HEADS), lambda b, g, t: (b * nt + t, 0)),
                  pl.BlockSpec((tc, N_V_HEADS), lambda b, g, t: (b * nt + t, 0)),
                  pl.BlockSpec((tc // chunk, N_V_HEADS, chunk), lambda b, g, t: (b * nt + t, 0, 0)),
                  _const_spec((None, 1, HEAD), (0, 0, 0), 3)],
        out_specs=[pl.BlockSpec((tc, vw), lambda b, g, t: (b * nt + t, g)),
                   pl.BlockSpec((None, None, hps, HEAD, HEAD), lambda b, g, t: (0, b, g, 0, 0))],
        out_shape=[jax.ShapeDtypeStruct((m, N_V_HEADS * HEAD), BF16),
                   jax.ShapeDtypeStruct((1, nb, N_V_HEADS, HEAD, HEAD), F32)],
        scratch_shapes=[pltpu.VMEM((hps, HEAD, HEAD), F32)],
        compiler_params=_params(3),
        name="delta_prompt",
    )(q, k, v, z, beta, gc, gct, norm_w.reshape(-1, 1, HEAD))


def _delta_sample_kernel(p_ref, st_ref, cw_ref, bt_ref, g_ref, nw_ref, s_ref,
                         o_ref, sto_ref, so_ref, qk_scr, *, rows_per_step):
    cdim = st_ref.shape[2]
    key_dim = N_QK_HEADS * HEAD
    rep = N_V_HEADS // N_QK_HEADS
    hs = range(N_V_HEADS)

    def one_sequence(r, carry):
        xrow = p_ref[r, :, 0:cdim]
        conv = (cw_ref[3:4, :] * xrow + cw_ref[2:3, :] * st_ref[r, 2:3, :]
                + cw_ref[1:2, :] * st_ref[r, 1:2, :] + cw_ref[0:1, :] * st_ref[r, 0:1, :])
        sto_ref[r, 0:1, :] = st_ref[r, 1:2, :]
        sto_ref[r, 1:2, :] = st_ref[r, 2:3, :]
        sto_ref[r, 2:3, :] = xrow
        act = _silu(conv)
        qk_scr[...] = jnp.zeros(qk_scr.shape, F32)
        for j in range(2 * N_QK_HEADS):
            xs = act[:, j * HEAD:(j + 1) * HEAD]
            xs = xs * lax.rsqrt(jnp.sum(xs * xs, axis=-1, keepdims=True) + EPS)
            if j < N_QK_HEADS:
                xs = xs * (HEAD ** -0.5)
            qk_scr[j:j + 1, :] = xs
        qkt = qk_scr[...].T
        a_row = jnp.exp(g_ref[r])
        b_row = bt_ref[r]
        kcols = [qkt[:, N_QK_HEADS + h // rep:N_QK_HEADS + h // rep + 1] for h in hs]
        sk = [jnp.sum(s_ref[r, h] * kcols[h], axis=0, keepdims=True) for h in hs]
        vnew = [b_row[:, h:h + 1] * (act[:, 2 * key_dim + h * HEAD:2 * key_dim + (h + 1) * HEAD]
                                      - a_row[:, h:h + 1] * sk[h]) for h in hs]
        outs = []
        for h in hs:
            s_new = a_row[:, h:h + 1] * s_ref[r, h] + kcols[h] * vnew[h]
            so_ref[r, h] = s_new
            outs.append(jnp.sum(s_new * qkt[:, h // rep:h // rep + 1], axis=0, keepdims=True))
        for h in hs:
            o = outs[h]
            z = p_ref[r, :, cdim + h * HEAD:cdim + (h + 1) * HEAD]
            og = o * lax.rsqrt(jnp.mean(o * o, axis=-1, keepdims=True) + EPS) * nw_ref[...] * _silu(z)
            o_ref[r, :, h * HEAD:(h + 1) * HEAD] = og.astype(BF16)
        return carry

    lax.fori_loop(0, rows_per_step, one_sequence, 0)


def _delta_sample(proj, conv_state, conv_w, beta, g, norm_w, s0):
    nb, width = proj.shape
    npast, cdim = conv_state.shape[2:]
    vdim = N_V_HEADS * HEAD
    rb = 4
    return pl.pallas_call(
        functools.partial(_delta_sample_kernel, rows_per_step=rb),
        grid=(nb // rb,),
        in_specs=[pl.BlockSpec((rb, 1, width), lambda b: (b, 0, 0)),
                  pl.BlockSpec((None, rb, npast, cdim), lambda b: (0, b, 0, 0)),
                  _const_spec((None,) + conv_w.shape[1:], (0, 0, 0), 1),
                  pl.BlockSpec((rb, 1, N_V_HEADS), lambda b: (b, 0, 0)),
                  pl.BlockSpec((rb, 1, N_V_HEADS), lambda b: (b, 0, 0)),
                  _const_spec((None, 1, HEAD), (0, 0, 0), 1),
                  pl.BlockSpec((None, rb, N_V_HEADS, HEAD, HEAD), lambda b: (0, b, 0, 0, 0))],
        out_specs=[pl.BlockSpec((rb, 1, vdim), lambda b: (b, 0, 0)),
                   pl.BlockSpec((None, rb, npast, cdim), lambda b: (0, b, 0, 0)),
                   pl.BlockSpec((None, rb, N_V_HEADS, HEAD, HEAD), lambda b: (0, b, 0, 0, 0))],
        out_shape=[jax.ShapeDtypeStruct((nb, 1, vdim), BF16),
                   jax.ShapeDtypeStruct((1,) + conv_state.shape[1:], F32),
                   jax.ShapeDtypeStruct((1,) + s0.shape[1:], F32)],
        scratch_shapes=[pltpu.VMEM((HEAD, HEAD), F32)],
        compiler_params=_params(1),
        name="delta_sample",
    )(proj.reshape(nb, 1, width), conv_state, conv_w, beta.reshape(nb, 1, N_V_HEADS),
      g.reshape(nb, 1, N_V_HEADS), norm_w.reshape(-1, 1, HEAD), s0)


def kernel(x_prompt, x_sample, c_prompt, c_sample, cache_pool, state_conv, state_rec, cache_ffn_conv,
           norm_w, ada_w, ada_b, pool_w, pool_scale, dn_w_in, dn_conv_w, dn_a_log, dn_dt_bias, dn_norm_w,
           dn_w_out, ffn_w_gate, ffn_w_up, ffn_conv_w, ffn_w_down):
    bp, seq, d = x_prompt.shape
    bs = x_sample.shape[0]
    key_dim = N_QK_HEADS * HEAD
    val_dim = N_V_HEADS * HEAD
    conv_dim = 2 * key_dim + val_dim
    assert bs == 128 and bp <= 8 and x_sample.shape[1] == 1
    assert ada_w.shape[0] == 2 and dn_w_in.shape[0] == 1 and pool_w.shape[0] == 1

    c_all = jnp.concatenate([c_sample, c_prompt, jnp.zeros((MOD_ROWS - bs - bp, d), F32)], axis=0)
    mods = _ada(c_all, ada_w, ada_b)

    xp = x_prompt.reshape(bp * seq, d)
    xs = x_sample.reshape(bs, d)

    xp, hp, pool_p16 = _pool_prompt(xp, mods, 0, norm_w, pool_w, pool_scale, seq)
    xs, hs, pool_st = _pool_sample(xs, mods, 0, norm_w, pool_w, pool_scale,
                                   cache_pool.transpose(0, 2, 1, 3))
    pool_p = pool_p16[:, 16 - POOL_STATE:][None]
    pool_s = pool_st.transpose(1, 0, 2)[None]

    ffn_p, ffn_s = [], []
    ffn_cache_t = cache_ffn_conv.transpose(0, 2, 1, 3)
    w_down = _to_bf16(ffn_w_down, "cast_w_down")
    w_out = _to_bf16(dn_w_out, "cast_w_out")

    def ffn(layer, xp, hp, xs, hs, nxt):
        ap, stp, a_s, sts = _ffn_a(hp, hs, layer, ffn_w_gate, ffn_w_up, ffn_conv_w, ffn_cache_t, seq)
        ffn_p.append(stp[:, 6:8])
        ffn_s.append(sts.transpose(1, 0, 2))
        xp, hp = _mm_epi(ap, w_down, layer, xp, mods, norm_w, layer, 5, 3, nxt,
                         per_row=False, seq=seq, name="ffn_b_prompt")
        xs, hs = _mm_epi(a_s, w_down, layer, xs, mods, norm_w, layer, 5, 3, nxt,
                         per_row=True, seq=1, name="ffn_b_sample")
        return xp, hp, xs, hs

    xp, hp, xs, hs = ffn(0, xp, hp, xs, hs, (1, 0, 0, 1, BF16))

    n_main = conv_dim + val_dim
    w_t = jnp.swapaxes(dn_w_in, 1, 2)
    qn, st_q, pq = _proj(hp, hs, w_t, dn_conv_w, 0, key_dim, seq, "qk", HEAD ** -0.5, "proj_q")
    kn, st_k, pk = _proj(hp, hs, w_t, dn_conv_w, key_dim, key_dim, seq, "qk", 1.0, "proj_k")
    vv, st_v, pv = _proj(hp, hs, w_t, dn_conv_w, 2 * key_dim, val_dim, seq, "v", 1.0, "proj_v")
    zz, pz = _proj(hp, hs, w_t, dn_conv_w, conv_dim, val_dim, seq, "z", 1.0, "proj_z")
    proj_s = jnp.concatenate([pq, pk, pv, pz], axis=1)
    beta_p, gc_p = _gates(hp, w_t, n_main, dn_a_log, dn_dt_bias, DELTA_CHUNK, "gates_prompt")
    beta_s, g_s = _gates(hs, w_t, n_main, dn_a_log, dn_dt_bias, 1, "gates_sample")
    conv_p = jnp.concatenate([st_q, st_k, st_v], axis=2)[:, 5:8][None]
    o_p, rec_p = _delta_prompt(qn, kn, vv, zz, beta_p, gc_p, dn_norm_w, bp, seq)
    o_s, conv_s, rec_s = _delta_sample(proj_s, state_conv, dn_conv_w, beta_s, g_s, dn_norm_w, state_rec)
    nxt = (1, 2, 3, 4, BF16)
    xp, hp = _mm_epi(o_p, w_out, 0, xp, mods, norm_w, 1, 2, 1, nxt,
                     per_row=False, seq=seq, name="out_prompt")
    xs, hs = _mm_epi(o_s.reshape(bs, val_dim), w_out, 0, xs, mods, norm_w, 1, 2, 1, nxt,
                     per_row=True, seq=1, name="out_sample")
    xp, _, xs, _ = ffn(1, xp, hp, xs, hs, None)

    return (xp.reshape(bp, seq, d), xs.reshape(bs, 1, d), pool_p, pool_s, conv_p, conv_s,
            rec_p, rec_s, jnp.stack(ffn_p), jnp.stack(ffn_s))
```

```python
import functools
import math

import jax
import jax.numpy as jnp
from jax import lax
from jax.experimental import pallas as pl
from jax.experimental.pallas import tpu as pltpu

F32 = jnp.float32
BF16 = jnp.bfloat16
EPS = 1e-6
POOL_WINDOWS = (2, 4, 8, 16)
POOL_STATE = max(POOL_WINDOWS) - 1
HEAD = 128
N_QK_HEADS = 16
N_V_HEADS = 32
DELTA_CHUNK = 128
PAST_LEN = 16384
VMEM_LIMIT = 56 * 1024 * 1024
MOD_ROWS = 136
PROMPT_MOD_BLOCK = 16
NT_DIMS = (((1,), (1,)), ((), ()))
TN_DIMS = (((0,), (0,)), ((), ()))


def _params(n_axes, vmem=VMEM_LIMIT):
    return pltpu.CompilerParams(dimension_semantics=("arbitrary",) * n_axes, vmem_limit_bytes=vmem)


def _rms(x, w):
    return x * lax.rsqrt(jnp.mean(x * x, axis=-1, keepdims=True) + EPS) * w


def _silu(x):
    return x * jax.nn.sigmoid(x)


def _dot(a, b):
    return jnp.dot(a, b, preferred_element_type=F32)


def _ada_kernel(c_ref, w_ref, b_ref, o_ref):
    o_ref[...] = _dot(_silu(c_ref[...]), w_ref[...]) + b_ref[...]


def _ada(c_all, ada_w, ada_b):
    depth, d, n = ada_w.shape
    tn = 1024
    return pl.pallas_call(
        _ada_kernel,
        grid=(depth, n // tn),
        in_specs=[pl.BlockSpec((MOD_ROWS, d), lambda l, j: (0, 0)),
                  pl.BlockSpec((None, d, tn), lambda l, j: (l, 0, j)),
                  pl.BlockSpec((None, 1, tn), lambda l, j: (l, 0, j))],
        out_specs=pl.BlockSpec((None, MOD_ROWS, tn), lambda l, j: (l, 0, j)),
        out_shape=jax.ShapeDtypeStruct((depth, MOD_ROWS, n), F32),
        compiler_params=_params(2),
        name="ada_mod",
    )(c_all, ada_w, ada_b.reshape(depth, 1, n))


def _const_spec(block, idx, n_axes):
    if n_axes == 1:
        return pl.BlockSpec(block, lambda i: idx)
    if n_axes == 2:
        return pl.BlockSpec(block, lambda i, j: idx)
    return pl.BlockSpec(block, lambda i, j, k: idx)


def _mod_spec_prompt(d, layer, chunk, n_axes):
    return _const_spec((None, 8, d), (layer, PROMPT_MOD_BLOCK, chunk), n_axes)


def _mod_spec_sample(d, rows, layer, chunk, n_axes):
    if n_axes == 1:
        return pl.BlockSpec((None, rows, d), lambda i: (layer, i, chunk))
    return pl.BlockSpec((None, rows, d), lambda i, j: (layer, i, chunk))


def _pool_groups(h, window_sum, recip_cnt, pw_ref, ps_ref, gd):
    ys = []
    for gi, w in enumerate(POOL_WINDOWS):
        c0 = gi * gd
        dm = window_sum(w, c0) * recip_cnt(w) - h[:, c0:c0 + gd]
        ys.append(_dot(dm, pw_ref[gi]))
    return jnp.concatenate(ys, axis=1) * ps_ref[...]


def _pool_prompt_kernel(x_ref, shm, scm, gtm, shf, scf, nw_ref, pw_ref, ps_ref,
                        xo_ref, hf_ref, st_ref, hp, *, tm, tiles_per_b, n_past):
    i = pl.program_id(0)
    b = i // tiles_per_b
    t = i % tiles_per_b
    d = x_ref.shape[1]
    gd = d // len(POOL_WINDOWS)

    def row(r):
        return r[pl.ds(b, 1), :]

    x = x_ref[...]
    h = _rms(x, nw_ref[0:1, :]) * (1.0 + row(scm)) + row(shm)

    @pl.when(t == 0)
    def _():
        hp[0:16, :] = jnp.zeros((16, d), F32)

    hp[16:16 + tm, :] = h
    pos = t * tm + lax.broadcasted_iota(jnp.int32, (tm, 1), 0) + n_past

    def window_sum(w, c0):
        acc = hp[:, c0:c0 + gd]
        span = 1
        while span < w:
            acc = acc + pltpu.roll(acc, span, 0)
            span *= 2
        return acc[16:16 + tm]

    def recip_cnt(w):
        return 1.0 / jnp.minimum(pos + 1, w).astype(F32)

    y = _pool_groups(h, window_sum, recip_cnt, pw_ref, ps_ref, gd)
    tail = hp[tm:tm + 16, :]
    st_ref[...] = tail
    hp[0:16, :] = tail
    xn = x + row(gtm) * _rms(y, nw_ref[1:2, :])
    xo_ref[...] = xn
    hf_ref[...] = (_rms(xn, nw_ref[2:3, :]) * (1.0 + row(scf)) + row(shf)).astype(BF16)


def _pool_prompt(x2, mods, layer, norm_w, pool_w, pool_scale, seq):
    m, d = x2.shape
    j = layer // 2
    tm = 256
    tiles_per_b = seq // tm
    nb = m // seq
    ms = lambda c: _mod_spec_prompt(d, layer, c, 1)
    return pl.pallas_call(
        functools.partial(_pool_prompt_kernel, tm=tm, tiles_per_b=tiles_per_b, n_past=0),
        grid=(m // tm,),
        in_specs=[pl.BlockSpec((tm, d), lambda i: (i, 0)),
                  ms(0), ms(1), ms(2), ms(3), ms(4),
                  _const_spec((None,) + norm_w.shape[1:], (layer, 0, 0), 1),
                  _const_spec((None,) + pool_w.shape[1:], (j, 0, 0, 0), 1),
                  _const_spec((None, 1, d), (j, 0, 0), 1)],
        out_specs=[pl.BlockSpec((tm, d), lambda i: (i, 0)),
                   pl.BlockSpec((tm, d), lambda i: (i, 0)),
                   pl.BlockSpec((None, 16, d), lambda i: (i // tiles_per_b, 0, 0))],
        out_shape=[jax.ShapeDtypeStruct((m, d), F32),
                   jax.ShapeDtypeStruct((m, d), BF16),
                   jax.ShapeDtypeStruct((nb, 16, d), F32)],
        scratch_shapes=[pltpu.VMEM((tm + 16, d), F32)],
        compiler_params=_params(1),
        name="pool_prompt",
    )(x2, mods, mods, mods, mods, mods, norm_w, pool_w, pool_scale.reshape(-1, 1, d))


def _pool_sample_kernel(x_ref, shm, scm, gtm, shf, scf, nw_ref, pw_ref, ps_ref, c_ref,
                        xo_ref, hf_ref, co_ref, *, n_past):
    d = x_ref.shape[1]
    gd = d // len(POOL_WINDOWS)
    x = x_ref[...]
    h = _rms(x, nw_ref[0:1, :]) * (1.0 + scm[...]) + shm[...]

    def window_sum(w, c0):
        acc = h[:, c0:c0 + gd]
        for j in range(1, w):
            acc = acc + c_ref[POOL_STATE - j, :, c0:c0 + gd]
        return acc

    def recip_cnt(w):
        return 1.0 / float(min(n_past + 1, w))

    y = _pool_groups(h, window_sum, recip_cnt, pw_ref, ps_ref, gd)
    for r in range(POOL_STATE - 1):
        co_ref[r] = c_ref[r + 1]
    co_ref[POOL_STATE - 1] = h
    xn = x + gtm[...] * _rms(y, nw_ref[1:2, :])
    xo_ref[...] = xn
    hf_ref[...] = (_rms(xn, nw_ref[2:3, :]) * (1.0 + scf[...]) + shf[...]).astype(BF16)


def _pool_sample(x2, mods, layer, norm_w, pool_w, pool_scale, cache_t):
    m, d = x2.shape
    j = layer // 2
    bb = 32
    ms = lambda c: _mod_spec_sample(d, bb, layer, c, 1)
    return pl.pallas_call(
        functools.partial(_pool_sample_kernel, n_past=PAST_LEN),
        grid=(m // bb,),
        in_specs=[pl.BlockSpec((bb, d), lambda i: (i, 0)),
                  ms(0), ms(1), ms(2), ms(3), ms(4),
                  _const_spec((None,) + norm_w.shape[1:], (layer, 0, 0), 1),
                  _const_spec((None,) + pool_w.shape[1:], (j, 0, 0, 0), 1),
                  _const_spec((None, 1, d), (j, 0, 0), 1),
                  pl.BlockSpec((None, POOL_STATE, bb, d), lambda i: (j, 0, i, 0))],
        out_specs=[pl.BlockSpec((bb, d), lambda i: (i, 0)),
                   pl.BlockSpec((bb, d), lambda i: (i, 0)),
                   pl.BlockSpec((POOL_STATE, bb, d), lambda i: (0, i, 0))],
        out_shape=[jax.ShapeDtypeStruct((m, d), F32),
                   jax.ShapeDtypeStruct((m, d), BF16),
                   jax.ShapeDtypeStruct((POOL_STATE, m, d), F32)],
        compiler_params=_params(1),
        name="pool_sample",
    )(x2, mods, mods, mods, mods, mods, norm_w, pool_w, pool_scale.reshape(-1, 1, d), cache_t)


def _ffn_a_kernel(h_ref, hs_ref, wg_ref, wu_ref, cw_ref, cs_ref,
                  a_ref, st_ref, as_ref, cso_ref, wgb, wub, carry, *bufs, tm, tiles_per_b, nsub):
    i = pl.program_id(1)
    t = i % tiles_per_b
    sub = tm // nsub
    gbufs, ubufs = bufs[:nsub], bufs[nsub:]
    cw0, cw1, cw2 = cw_ref[0:1, :], cw_ref[1:2, :], cw_ref[2:3, :]

    @pl.when(i == 0)
    def _():
        wgb[...] = wg_ref[...].astype(BF16)
        wub[...] = wu_ref[...].astype(BF16)
        hs = hs_ref[...]
        gs = _dot(hs, wgb[...])
        conv = cw2 * gs + cw1 * cs_ref[1] + cw0 * cs_ref[0]
        cso_ref[0] = cs_ref[1]
        cso_ref[1] = gs
        as_ref[...] = (_silu(conv) * _dot(hs, wub[...])).astype(BF16)

    @pl.when(t == 0)
    def _():
        carry[...] = jnp.zeros(carry.shape, F32)

    def matmuls(s):
        h = h_ref[s * sub:(s + 1) * sub, :]
        gbufs[s][8:8 + sub, :] = _dot(h, wgb[...])
        gbufs[s][0:8, :] = carry[...] if s == 0 else gbufs[s - 1][sub:sub + 8, :]
        ubufs[s][...] = _dot(h, wub[...])

    def epilogue(s):
        g = gbufs[s]
        conv = cw2 * g[8:8 + sub, :] + cw1 * g[pl.ds(7, sub), :] + cw0 * g[pl.ds(6, sub), :]
        a_ref[s * sub:(s + 1) * sub, :] = (_silu(conv) * ubufs[s][...]).astype(BF16)

    matmuls(0)
    for s in range(nsub):
        if s + 1 < nsub:
            matmuls(s + 1)
        epilogue(s)
    tail = gbufs[nsub - 1][sub:sub + 8, :]
    st_ref[...] = tail
    carry[...] = tail


def _ffn_a(h, hs, layer, wg, wu, cw, cache_t, seq):
    m, d = h.shape
    ms = hs.shape[0]
    f = wg.shape[2]
    tm, tf = 1024, 512
    nsub = 4
    sub = tm // nsub
    tiles_per_b = seq // tm
    nb = m // seq
    return pl.pallas_call(
        functools.partial(_ffn_a_kernel, tm=tm, tiles_per_b=tiles_per_b, nsub=nsub),
        grid=(f // tf, m // tm),
        in_specs=[pl.BlockSpec((tm, d), lambda j, i: (i, 0)),
                  pl.BlockSpec((ms, d), lambda j, i: (0, 0)),
                  pl.BlockSpec((None, d, tf), lambda j, i: (layer, 0, j)),
                  pl.BlockSpec((None, d, tf), lambda j, i: (layer, 0, j)),
                  pl.BlockSpec((None, cw.shape[1], tf), lambda j, i: (layer, 0, j)),
                  pl.BlockSpec((None, 2, ms, tf), lambda j, i: (layer, 0, 0, j))],
        out_specs=[pl.BlockSpec((tm, tf), lambda j, i: (i, j)),
                   pl.BlockSpec((None, 8, tf), lambda j, i: (i // tiles_per_b, 0, j)),
                   pl.BlockSpec((ms, tf), lambda j, i: (0, j)),
                   pl.BlockSpec((2, ms, tf), lambda j, i: (0, 0, j))],
        out_shape=[jax.ShapeDtypeStruct((m, f), BF16),
                   jax.ShapeDtypeStruct((nb, 8, f), F32),
                   jax.ShapeDtypeStruct((ms, f), BF16),
                   jax.ShapeDtypeStruct((2, ms, f), F32)],
        scratch_shapes=([pltpu.VMEM((d, tf), BF16), pltpu.VMEM((d, tf), BF16), pltpu.VMEM((8, tf), F32)]
                        + [pltpu.VMEM((8 + sub, tf), F32)] * nsub + [pltpu.VMEM((sub, tf), F32)] * nsub),
        compiler_params=_params(2),
        name="ffn_a",
    )(h, hs, wg, wu, cw, cache_t)


def _cast_kernel(x_ref, o_ref):
    o_ref[...] = x_ref[...].astype(o_ref.dtype)


def _to_bf16(w, name):
    nl, k, n = w.shape
    tk = 512
    return pl.pallas_call(
        _cast_kernel,
        grid=(nl, k // tk),
        in_specs=[pl.BlockSpec((None, tk, n), lambda l, i: (l, i, 0))],
        out_specs=pl.BlockSpec((None, tk, n), lambda l, i: (l, i, 0)),
        out_shape=jax.ShapeDtypeStruct(w.shape, BF16),
        compiler_params=_params(2),
        name=name,
    )(w)


def _mm_epi_kernel(*refs, per_row, tiles_per_b, post_idx, next_idx, next_dtype):
    has_next = next_idx is not None
    if has_next:
        a_ref, w_ref, x_ref, gt, nwp, shn, scn, nwn, xo_ref, ho_ref = refs
    else:
        a_ref, w_ref, x_ref, gt, nwp, xo_ref = refs
    b = pl.program_id(0) // tiles_per_b

    def sel(r):
        return r[...] if per_row else r[pl.ds(b, 1), :]

    def unit(v):
        return v * lax.rsqrt(jnp.mean(v * v, axis=-1, keepdims=True) + EPS)

    y = _dot(a_ref[...], w_ref[...])
    xn = x_ref[...] + unit(y) * (sel(gt) * nwp[post_idx:post_idx + 1, :])
    xo_ref[...] = xn
    if has_next:
        scale = nwn[next_idx:next_idx + 1, :] * (1.0 + sel(scn))
        ho_ref[...] = (unit(xn) * scale + sel(shn)).astype(next_dtype)


def _mm_epi(a, w, wl, x, mods, norm_w, layer, gate_chunk, post_idx, nxt, *, per_row, seq, name):
    m, kdim = a.shape
    d = w.shape[2]
    tm = m if per_row else 256
    assert w.dtype == BF16
    tiles_per_b = 1 if per_row else seq // tm
    if per_row:
        ms = lambda l, c: _mod_spec_sample(d, tm, l, c, 1)
    else:
        ms = lambda l, c: _mod_spec_prompt(d, l, c, 1)
    nw_spec = lambda l: _const_spec((None,) + norm_w.shape[1:], (l, 0, 0), 1)
    in_specs = [pl.BlockSpec((tm, kdim), lambda i: (i, 0)),
                pl.BlockSpec((None, kdim, d), lambda i: (wl, 0, 0), pipeline_mode=pl.Buffered(1)),
                pl.BlockSpec((tm, d), lambda i: (i, 0)),
                ms(layer, gate_chunk), nw_spec(layer)]
    args = [a, w, x, mods, norm_w]
    out_specs = [pl.BlockSpec((tm, d), lambda i: (i, 0))]
    out_shape = [jax.ShapeDtypeStruct((m, d), F32)]
    next_idx = next_dtype = None
    if nxt is not None:
        nlayer, next_idx, sh_chunk, sc_chunk, next_dtype = nxt
        in_specs += [ms(nlayer, sh_chunk), ms(nlayer, sc_chunk), nw_spec(nlayer)]
        args += [mods, mods, norm_w]
        out_specs.append(pl.BlockSpec((tm, d), lambda i: (i, 0)))
        out_shape.append(jax.ShapeDtypeStruct((m, d), next_dtype))
    res = pl.pallas_call(
        functools.partial(_mm_epi_kernel, per_row=per_row, tiles_per_b=tiles_per_b,
                          post_idx=post_idx, next_idx=next_idx, next_dtype=next_dtype),
        grid=(m // tm,),
        in_specs=in_specs,
        out_specs=out_specs,
        out_shape=out_shape,
        compiler_params=_params(1),
        name=name,
    )(*args)
    return res if nxt is not None else (res[0], None)


def _proj_kernel(*refs, tm, tiles_per_b, nsub, mode, scale):
    if mode == "z":
        h_ref, hs_ref, w_ref, o_ref, ps_ref, wb = refs
    else:
        h_ref, hs_ref, w_ref, cw_ref, o_ref, st_ref, ps_ref, wb, buf = refs
    i = pl.program_id(1)
    t = i % tiles_per_b
    sub = tm // nsub
    tn = wb.shape[1]

    @pl.when(i == 0)
    def _():
        wb[...] = w_ref[...].T.astype(BF16)
        ps_ref[...] = _dot(hs_ref[...], wb[...])

    if mode == "z":
        o_ref[...] = _dot(h_ref[...], wb[...]).astype(BF16)
        return

    @pl.when(t == 0)
    def _():
        buf[0:8, :] = jnp.zeros((8, tn), F32)

    def matmul(s):
        buf[8 + s * sub:8 + (s + 1) * sub, :] = _dot(h_ref[s * sub:(s + 1) * sub, :], wb[...])

    def epilogue(s):
        r0 = 8 + s * sub
        conv = (cw_ref[3:4, :] * buf[r0:r0 + sub, :] + cw_ref[2:3, :] * buf[pl.ds(r0 - 1, sub), :]
                + cw_ref[1:2, :] * buf[pl.ds(r0 - 2, sub), :] + cw_ref[0:1, :] * buf[pl.ds(r0 - 3, sub), :])
        act = _silu(conv)
        if mode == "qk":
            outs = []
            for hh in range(tn // HEAD):
                xs = act[:, hh * HEAD:(hh + 1) * HEAD]
                outs.append(xs * (lax.rsqrt(jnp.sum(xs * xs, axis=-1, keepdims=True) + EPS) * scale))
            act = jnp.concatenate(outs, axis=1)
        o_ref[s * sub:(s + 1) * sub, :] = act.astype(BF16)

    matmul(0)
    for s in range(nsub):
        if s + 1 < nsub:
            matmul(s + 1)
        epilogue(s)
    tail = buf[tm:tm + 8, :]
    st_ref[...] = tail
    buf[0:8, :] = tail


def _proj(h, hs, w_t, conv_w, col0, width, seq, mode, scale, name):
    m, d = h.shape
    ms = hs.shape[0]
    tm, tn = 1024, 1024
    nsub = 4
    tiles_per_b = seq // tm
    nb = m // seq
    cb = col0 // tn
    conv = mode != "z"
    in_specs = [pl.BlockSpec((tm, d), lambda j, i: (i, 0)),
                pl.BlockSpec((ms, d), lambda j, i: (0, 0)),
                pl.BlockSpec((None, tn, d), lambda j, i: (0, cb + j, 0))]
    args = [h, hs, w_t]
    out_specs = [pl.BlockSpec((tm, tn), lambda j, i: (i, j))]
    out_shape = [jax.ShapeDtypeStruct((m, width), BF16)]
    scratch = [pltpu.VMEM((d, tn), BF16)]
    if conv:
        in_specs.append(pl.BlockSpec((None, conv_w.shape[1], tn), lambda j, i: (0, 0, cb + j)))
        args.append(conv_w)
        out_specs.append(pl.BlockSpec((None, 8, tn), lambda j, i: (i // tiles_per_b, 0, j)))
        out_shape.append(jax.ShapeDtypeStruct((nb, 8, width), F32))
        scratch.append(pltpu.VMEM((tm + 8, tn), F32))
    out_specs.append(pl.BlockSpec((ms, tn), lambda j, i: (0, j)))
    out_shape.append(jax.ShapeDtypeStruct((ms, width), F32))
    return pl.pallas_call(
        functools.partial(_proj_kernel, tm=tm, tiles_per_b=tiles_per_b, nsub=nsub, mode=mode, scale=scale),
        grid=(width // tn, m // tm),
        in_specs=in_specs,
        out_specs=out_specs,
        out_shape=out_shape,
        scratch_shapes=scratch,
        compiler_params=_params(2),
        name=name,
    )(*args)


def _gates_kernel(h_ref, w_ref, al_ref, dt_ref, beta_ref, gc_ref, *, tm, chunk):
    nh = al_ref.shape[1]
    ba = lax.dot_general(h_ref[...], w_ref[...].astype(BF16), NT_DIMS, preferred_element_type=F32)
    beta_ref[...] = jax.nn.sigmoid(ba[:, :nh])
    xx = ba[:, nh:] + dt_ref[...]
    softplus = jnp.maximum(xx, 0.0) + jnp.log1p(jnp.exp(-jnp.abs(xx)))
    g = -jnp.exp(al_ref[...]) * softplus
    shift = int(math.log2(chunk))
    r = lax.broadcasted_iota(jnp.int32, (tm, tm), 0)
    c = lax.broadcasted_iota(jnp.int32, (tm, tm), 1)
    tri = jnp.where(((r >> shift) == (c >> shift)) & (c <= r), 1.0, 0.0).astype(F32)
    gc_ref[...] = jnp.dot(tri, g, precision=lax.Precision.HIGHEST, preferred_element_type=F32)


def _gates(h, w_t, row0, a_log, dt_bias, chunk, name):
    m, d = h.shape
    nh = a_log.shape[1]
    tm = min(m, 256)
    assert row0 % (2 * nh) == 0 and w_t.shape[1] - row0 == 2 * nh
    return pl.pallas_call(
        functools.partial(_gates_kernel, tm=tm, chunk=chunk),
        grid=(m // tm,),
        in_specs=[pl.BlockSpec((tm, d), lambda i: (i, 0)),
                  pl.BlockSpec((None, 2 * nh, d), lambda i: (0, row0 // (2 * nh), 0)),
                  pl.BlockSpec((None, 1, nh), lambda i: (0, 0, 0)),
                  pl.BlockSpec((None, 1, nh), lambda i: (0, 0, 0))],
        out_specs=[pl.BlockSpec((tm, nh), lambda i: (i, 0)),
                   pl.BlockSpec((tm, nh), lambda i: (i, 0))],
        out_shape=[jax.ShapeDtypeStruct((m, nh), F32), jax.ShapeDtypeStruct((m, nh), F32)],
        compiler_params=_params(1),
        name=name,
    )(h, w_t, a_log.reshape(-1, 1, nh), dt_bias.reshape(-1, 1, nh))


def _delta_kernel(q_ref, k_ref, v_ref, z_ref, bt_ref, gc_ref, gct_ref, nw_ref,
                  o_ref, so_ref, s_scr, *, tc, hps, chunk):
    hg = pl.program_id(1)
    t = pl.program_id(2)
    nt = pl.num_programs(2)
    nh = bt_ref.shape[1]
    nck = tc // chunk
    n_factors = int(math.log2(chunk))

    @pl.when(t == 0)
    def _():
        s_scr[...] = jnp.zeros(s_scr.shape, F32)

    lane = lax.broadcasted_iota(jnp.int32, (chunk, nh), 1)
    ri = lax.broadcasted_iota(jnp.int32, (chunk, chunk), 0)
    ci = lax.broadcasted_iota(jnp.int32, (chunk, chunk), 1)
    tril = ci <= ri
    strict = ci < ri
    ri2 = lax.broadcasted_iota(jnp.int32, (chunk, 2 * chunk), 0)
    ci2 = lax.broadcasted_iota(jnp.int32, (chunk, 2 * chunk), 1)
    eye2 = jnp.where((ci2 == ri2) | (ci2 == ri2 + chunk), 1.0, 0.0).astype(F32)
    left = ci2 < chunk

    def blockdiag(xp):
        return jnp.concatenate([jnp.where(left, xp, 0.0), jnp.where(left, 0.0, xp)], axis=0)

    pairs = [(c, jq) for c in range(nck) for jq in range(hps // 2)]
    heads = [(c, hh) for c in range(nck) for hh in range(hps)]
    kc, qc, lowp, qkm, bcol, gcol = {}, {}, {}, {}, {}, {}
    for c, jq in pairs:
        rows = slice(c * chunk, (c + 1) * chunk)
        qcols = slice(jq * HEAD, (jq + 1) * HEAD)
        qc[c, jq] = q_ref[rows, qcols].astype(F32)
        kc[c, jq] = k_ref[rows, qcols].astype(F32)
        kkqk = lax.dot_general(jnp.concatenate([kc[c, jq], qc[c, jq]], axis=0), kc[c, jq],
                               NT_DIMS, preferred_element_type=F32)
        btb = bt_ref[rows, :]
        gcb = gc_ref[rows, :]
        lows = []
        for r in range(2):
            hh = 2 * jq + r
            hidx = hg * hps + hh
            pick = lane == hidx
            bcol[c, hh] = jnp.sum(jnp.where(pick, btb, 0.0), axis=1, keepdims=True)
            gcol[c, hh] = jnp.sum(jnp.where(pick, gcb, 0.0), axis=1, keepdims=True)
            grow = gct_ref[c, pl.ds(hidx, 1), :]
            decay = jnp.exp(jnp.where(tril, gcol[c, hh] - grow, -jnp.inf))
            lows.append(jnp.where(strict, kkqk[:chunk] * bcol[c, hh] * decay, 0.0))
            qkm[c, hh] = jnp.where(tril, kkqk[chunk:] * decay, 0.0)
        lowp[c, jq] = jnp.concatenate(lows, axis=1)

    tinv = {p: eye2 - lowp[p] for p in pairs}
    pw = {p: _dot(lowp[p], blockdiag(lowp[p])) for p in pairs}
    for f in range(n_factors - 1):
        for p in pairs:
            bd = blockdiag(pw[p])
            if f == n_factors - 2:
                tinv[p] = tinv[p] + _dot(tinv[p], bd)
            else:
                both = _dot(jnp.concatenate([tinv[p], pw[p]], axis=0), bd)
                tinv[p] = tinv[p] + both[:chunk]
                pw[p] = both[chunk:]

    uw, ecol = {}, {}
    for c, hh in heads:
        rows = slice(c * chunk, (c + 1) * chunk)
        vcols = slice(hh * HEAD, (hh + 1) * HEAD)
        ecol[c, hh] = jnp.exp(gcol[c, hh])
        vc = v_ref[rows, vcols].astype(F32)
        rhs = jnp.concatenate([vc * bcol[c, hh], kc[c, hh // 2] * (bcol[c, hh] * ecol[c, hh])], axis=1)
        r = hh % 2
        uw[c, hh] = _dot(tinv[c, hh // 2][:, r * chunk:(r + 1) * chunk], rhs)

    s = [s_scr[hh] for hh in range(hps)]
    for c in range(nck):
        rows = slice(c * chunk, (c + 1) * chunk)
        ws_qs = [_dot(jnp.concatenate([uw[c, hh][:, HEAD:], qc[c, hh // 2] * ecol[c, hh]], axis=0), s[hh])
                 for hh in range(hps)]
        vnew = [uw[c, hh][:, :HEAD] - ws_qs[hh][:chunk] for hh in range(hps)]
        outs = [ws_qs[hh][chunk:] + _dot(qkm[c, hh], vnew[hh]) for hh in range(hps)]
        for hh in range(hps):
            glast = gcol[c, hh][chunk - 1:chunk, :]
            kd = kc[c, hh // 2] * jnp.exp(glast - gcol[c, hh])
            s[hh] = s[hh] * jnp.exp(glast) + lax.dot_general(kd, vnew[hh], TN_DIMS,
                                                             preferred_element_type=F32)
        for hh in range(hps):
            vcols = slice(hh * HEAD, (hh + 1) * HEAD)
            o = outs[hh]
            og = (o * lax.rsqrt(jnp.mean(o * o, axis=-1, keepdims=True) + EPS) * nw_ref[...]
                  * _silu(z_ref[rows, vcols].astype(F32)))
            o_ref[rows, vcols] = og.astype(BF16)
    for hh in range(hps):
        s_scr[hh] = s[hh]

    @pl.when(t == nt - 1)
    def _():
        so_ref[...] = s_scr[...]


def _delta_prompt(q, k, v, z, beta, gc, norm_w, nb, seq):
    m = q.shape[0]
    chunk = DELTA_CHUNK
    tc = 256
    hps = 8
    nhg = N_V_HEADS // hps
    nt = seq // tc
    vw = hps * HEAD
    qw = vw // 2
    gct = gc.reshape(m // chunk, chunk, N_V_HEADS).transpose(0, 2, 1)
    return pl.pallas_call(
        functools.partial(_delta_kernel, tc=tc, hps=hps, chunk=chunk),
        grid=(nb, nhg, nt),
        in_specs=[pl.BlockSpec((tc, qw), lambda b, g, t: (b * nt + t, g)),
                  pl.BlockSpec((tc, qw), lambda b, g, t: (b * nt + t, g)),
                  pl.BlockSpec((tc, vw), lambda b, g, t: (b * nt + t, g)),
                  pl.BlockSpec((tc, vw), lambda b, g, t: (b * nt + t, g)),
                  pl.BlockSpec((tc, N_V_HEADS), lambda b, g, t: (b * nt + t, 0)),
                  pl.BlockSpec((tc, N_V_HEADS), lambda b, g, t: (b * nt + t, 0)),
                  pl.BlockSpec((tc // chunk, N_V_HEADS, chunk), lambda b, g, t: (b * nt + t, 0, 0)),
                  _const_spec((None, 1, HEAD), (0, 0, 0), 3)],
        out_specs=[pl.BlockSpec((tc, vw), lambda b, g, t: (b * nt + t, g)),
                   pl.BlockSpec((None, None, hps, HEAD, HEAD), lambda b, g, t: (0, b, g, 0, 0))],
        out_shape=[jax.ShapeDtypeStruct((m, N_V_HEADS * HEAD), BF16),
                   jax.ShapeDtypeStruct((1, nb, N_V_HEADS, HEAD, HEAD), F32)],
        scratch_shapes=[pltpu.VMEM((hps, HEAD, HEAD), F32)],
        compiler_params=_params(3),
        name="delta_prompt",
    )(q, k, v, z, beta, gc, gct, norm_w.reshape(-1, 1, HEAD))


def _delta_sample_kernel(p_ref, st_ref, cw_ref, bt_ref, g_ref, nw_ref, s_ref,
                         o_ref, sto_ref, so_ref, qk_scr, *, rows_per_step):
    cdim = st_ref.shape[2]
    key_dim = N_QK_HEADS * HEAD
    rep = N_V_HEADS // N_QK_HEADS
    hs = range(N_V_HEADS)

    def one_sequence(r, carry):
        xrow = p_ref[r, :, 0:cdim]
        conv = (cw_ref[3:4, :] * xrow + cw_ref[2:3, :] * st_ref[r, 2:3, :]
                + cw_ref[1:2, :] * st_ref[r, 1:2, :] + cw_ref[0:1, :] * st_ref[r, 0:1, :])
        sto_ref[r, 0:1, :] = st_ref[r, 1:2, :]
        sto_ref[r, 1:2, :] = st_ref[r, 2:3, :]
        sto_ref[r, 2:3, :] = xrow
        act = _silu(conv)
        qk_scr[...] = jnp.zeros(qk_scr.shape, F32)
        qdotk = []
        for j in range(N_QK_HEADS):
            qs = act[:, j * HEAD:(j + 1) * HEAD]
            qs = qs * (lax.rsqrt(jnp.sum(qs * qs, axis=-1, keepdims=True) + EPS) * (HEAD ** -0.5))
            ks = act[:, key_dim + j * HEAD:key_dim + (j + 1) * HEAD]
            ks = ks * lax.rsqrt(jnp.sum(ks * ks, axis=-1, keepdims=True) + EPS)
            qk_scr[8 * j:8 * j + 1, :] = ks
            qk_scr[8 * j + 1:8 * j + 2, :] = qs
            qdotk.append(jnp.sum(qs * ks, axis=-1, keepdims=True))
        qkt = qk_scr[...].T
        a_row = jnp.exp(g_ref[r])
        b_row = bt_ref[r]
        skq = [_dot(qk_scr[8 * (h // rep):8 * (h // rep) + 8, :], s_ref[r, h]) for h in hs]
        vnew = [b_row[:, h:h + 1] * (act[:, 2 * key_dim + h * HEAD:2 * key_dim + (h + 1) * HEAD]
                                      - a_row[:, h:h + 1] * skq[h][0:1, :]) for h in hs]
        outs = []
        for h in hs:
            j = h // rep
            so_ref[r, h] = a_row[:, h:h + 1] * s_ref[r, h] + qkt[:, 8 * j:8 * j + 1] * vnew[h]
            outs.append(a_row[:, h:h + 1] * skq[h][1:2, :] + qdotk[j] * vnew[h])
        for h in hs:
            o = outs[h]
            z = p_ref[r, :, cdim + h * HEAD:cdim + (h + 1) * HEAD]
            og = o * lax.rsqrt(jnp.mean(o * o, axis=-1, keepdims=True) + EPS) * nw_ref[...] * _silu(z)
            o_ref[r, :, h * HEAD:(h + 1) * HEAD] = og.astype(BF16)
        return carry

    lax.fori_loop(0, rows_per_step, one_sequence, 0)


def _delta_sample(proj, conv_state, conv_w, beta, g, norm_w, s0):
    nb, width = proj.shape
    npast, cdim = conv_state.shape[2:]
    vdim = N_V_HEADS * HEAD
    rb = 4
    return pl.pallas_call(
        functools.partial(_delta_sample_kernel, rows_per_step=rb),
        grid=(nb // rb,),
        in_specs=[pl.BlockSpec((rb, 1, width), lambda b: (b, 0, 0)),
                  pl.BlockSpec((None, rb, npast, cdim), lambda b: (0, b, 0, 0)),
                  _const_spec((None,) + conv_w.shape[1:], (0, 0, 0), 1),
                  pl.BlockSpec((rb, 1, N_V_HEADS), lambda b: (b, 0, 0)),
                  pl.BlockSpec((rb, 1, N_V_HEADS), lambda b: (b, 0, 0)),
                  _const_spec((None, 1, HEAD), (0, 0, 0), 1),
                  pl.BlockSpec((None, rb, N_V_HEADS, HEAD, HEAD), lambda b: (0, b, 0, 0, 0))],
        out_specs=[pl.BlockSpec((rb, 1, vdim), lambda b: (b, 0, 0)),
                   pl.BlockSpec((None, rb, npast, cdim), lambda b: (0, b, 0, 0)),
                   pl.BlockSpec((None, rb, N_V_HEADS, HEAD, HEAD), lambda b: (0, b, 0, 0, 0))],
        out_shape=[jax.ShapeDtypeStruct((nb, 1, vdim), BF16),
                   jax.ShapeDtypeStruct((1,) + conv_state.shape[1:], F32),
                   jax.ShapeDtypeStruct((1,) + s0.shape[1:], F32)],
        scratch_shapes=[pltpu.VMEM((HEAD, HEAD), F32)],
        compiler_params=_params(1),
        name="delta_sample",
    )(proj.reshape(nb, 1, width), conv_state, conv_w, beta.reshape(nb, 1, N_V_HEADS),
      g.reshape(nb, 1, N_V_HEADS), norm_w.reshape(-1, 1, HEAD), s0)


def kernel(x_prompt, x_sample, c_prompt, c_sample, cache_pool, state_conv, state_rec, cache_ffn_conv,
           norm_w, ada_w, ada_b, pool_w, pool_scale, dn_w_in, dn_conv_w, dn_a_log, dn_dt_bias, dn_norm_w,
           dn_w_out, ffn_w_gate, ffn_w_up, ffn_conv_w, ffn_w_down):
    bp, seq, d = x_prompt.shape
    bs = x_sample.shape[0]
    key_dim = N_QK_HEADS * HEAD
    val_dim = N_V_HEADS * HEAD
    conv_dim = 2 * key_dim + val_dim
    assert bs == 128 and bp <= 8 and x_sample.shape[1] == 1
    assert ada_w.shape[0] == 2 and dn_w_in.shape[0] == 1 and pool_w.shape[0] == 1

    c_all = jnp.concatenate([c_sample, c_prompt, jnp.zeros((MOD_ROWS - bs - bp, d), F32)], axis=0)
    mods = _ada(c_all, ada_w, ada_b)

    xp = x_prompt.reshape(bp * seq, d)
    xs = x_sample.reshape(bs, d)

    xp, hp, pool_p16 = _pool_prompt(xp, mods, 0, norm_w, pool_w, pool_scale, seq)
    xs, hs, pool_st = _pool_sample(xs, mods, 0, norm_w, pool_w, pool_scale,
                                   cache_pool.transpose(0, 2, 1, 3))
    pool_p = pool_p16[:, 16 - POOL_STATE:][None]
    pool_s = pool_st.transpose(1, 0, 2)[None]

    ffn_p, ffn_s = [], []
    ffn_cache_t = cache_ffn_conv.transpose(0, 2, 1, 3)
    w_down = _to_bf16(ffn_w_down, "cast_w_down")
    w_out = _to_bf16(dn_w_out, "cast_w_out")

    def ffn(layer, xp, hp, xs, hs, nxt):
        ap, stp, a_s, sts = _ffn_a(hp, hs, layer, ffn_w_gate, ffn_w_up, ffn_conv_w, ffn_cache_t, seq)
        ffn_p.append(stp[:, 6:8])
        ffn_s.append(sts.transpose(1, 0, 2))
        xp, hp = _mm_epi(ap, w_down, layer, xp, mods, norm_w, layer, 5, 3, nxt,
                         per_row=False, seq=seq, name="ffn_b_prompt")
        xs, hs = _mm_epi(a_s, w_down, layer, xs, mods, norm_w, layer, 5, 3, nxt,
                         per_row=True, seq=1, name="ffn_b_sample")
        return xp, hp, xs, hs

    xp, hp, xs, hs = ffn(0, xp, hp, xs, hs, (1, 0, 0, 1, BF16))

    n_main = conv_dim + val_dim
    w_t = jnp.swapaxes(dn_w_in, 1, 2)
    qn, st_q, pq = _proj(hp, hs, w_t, dn_conv_w, 0, key_dim, seq, "qk", HEAD ** -0.5, "proj_q")
    kn, st_k, pk = _proj(hp, hs, w_t, dn_conv_w, key_dim, key_dim, seq, "qk", 1.0, "proj_k")
    vv, st_v, pv = _proj(hp, hs, w_t, dn_conv_w, 2 * key_dim, val_dim, seq, "v", 1.0, "proj_v")
    zz, pz = _proj(hp, hs, w_t, dn_conv_w, conv_dim, val_dim, seq, "z", 1.0, "proj_z")
    proj_s = jnp.concatenate([pq, pk, pv, pz], axis=1)
    beta_p, gc_p = _gates(hp, w_t, n_main, dn_a_log, dn_dt_bias, DELTA_CHUNK, "gates_prompt")
    beta_s, g_s = _gates(hs, w_t, n_main, dn_a_log, dn_dt_bias, 1, "gates_sample")
    conv_p = jnp.concatenate([st_q, st_k, st_v], axis=2)[:, 5:8][None]
    o_p, rec_p = _delta_prompt(qn, kn, vv, zz, beta_p, gc_p, dn_norm_w, bp, seq)
    o_s, conv_s, rec_s = _delta_sample(proj_s, state_conv, dn_conv_w, beta_s, g_s, dn_norm_w, state_rec)
    nxt = (1, 2, 3, 4, BF16)
    xp, hp = _mm_epi(o_p, w_out, 0, xp, mods, norm_w, 1, 2, 1, nxt,
                     per_row=False, seq=seq, name="out_prompt")
    xs, hs = _mm_epi(o_s.reshape(bs, val_dim), w_out, 0, xs, mods, norm_w, 1, 2, 1, nxt,
                     per_row=True, seq=1, name="out_sample")
    xp, _, xs, _ = ffn(1, xp, hp, xs, hs, None)

    return (xp.reshape(bp, seq, d), xs.reshape(bs, 1, d), pool_p, pool_s, conv_p, conv_s,
            rec_p, rec_s, jnp.stack(ffn_p), jnp.stack(ffn_s))
```

```python
import functools
import math

import jax
import jax.numpy as jnp
from jax import lax
from jax.experimental import pallas as pl
from jax.experimental.pallas import tpu as pltpu

F32 = jnp.float32
BF16 = jnp.bfloat16
EPS = 1e-6
POOL_WINDOWS = (2, 4, 8, 16)
POOL_STATE = max(POOL_WINDOWS) - 1
HEAD = 128
N_QK_HEADS = 16
N_V_HEADS = 32
DELTA_CHUNK = 128
PAST_LEN = 16384
VMEM_LIMIT = 56 * 1024 * 1024
MOD_ROWS = 136
PROMPT_MOD_BLOCK = 16
NT_DIMS = (((1,), (1,)), ((), ()))
TN_DIMS = (((0,), (0,)), ((), ()))


def _params(n_axes, vmem=VMEM_LIMIT):
    return pltpu.CompilerParams(dimension_semantics=("arbitrary",) * n_axes, vmem_limit_bytes=vmem)


def _rms(x, w):
    return x * lax.rsqrt(jnp.mean(x * x, axis=-1, keepdims=True) + EPS) * w


def _silu(x):
    return x * jax.nn.sigmoid(x)


def _dot(a, b):
    return jnp.dot(a, b, preferred_element_type=F32)


def _ada_kernel(c_ref, w_ref, b_ref, o_ref):
    o_ref[...] = _dot(_silu(c_ref[...]), w_ref[...]) + b_ref[...]


def _ada(c_all, ada_w, ada_b):
    depth, d, n = ada_w.shape
    tn = 1024
    return pl.pallas_call(
        _ada_kernel,
        grid=(depth, n // tn),
        in_specs=[pl.BlockSpec((MOD_ROWS, d), lambda l, j: (0, 0)),
                  pl.BlockSpec((None, d, tn), lambda l, j: (l, 0, j)),
                  pl.BlockSpec((None, 1, tn), lambda l, j: (l, 0, j))],
        out_specs=pl.BlockSpec((None, MOD_ROWS, tn), lambda l, j: (l, 0, j)),
        out_shape=jax.ShapeDtypeStruct((depth, MOD_ROWS, n), F32),
        compiler_params=_params(2),
        name="ada_mod",
    )(c_all, ada_w, ada_b.reshape(depth, 1, n))


def _const_spec(block, idx, n_axes):
    if n_axes == 1:
        return pl.BlockSpec(block, lambda i: idx)
    if n_axes == 2:
        return pl.BlockSpec(block, lambda i, j: idx)
    return pl.BlockSpec(block, lambda i, j, k: idx)


def _mod_spec_prompt(d, layer, chunk, n_axes):
    return _const_spec((None, 8, d), (layer, PROMPT_MOD_BLOCK, chunk), n_axes)


def _mod_spec_sample(d, rows, layer, chunk, n_axes):
    if n_axes == 1:
        return pl.BlockSpec((None, rows, d), lambda i: (layer, i, chunk))
    return pl.BlockSpec((None, rows, d), lambda i, j: (layer, i, chunk))


def _pool_groups(h, window_sum, recip_cnt, pw_ref, ps_ref, gd):
    ys = []
    for gi, w in enumerate(POOL_WINDOWS):
        c0 = gi * gd
        dm = window_sum(w, c0) * recip_cnt(w) - h[:, c0:c0 + gd]
        ys.append(_dot(dm, pw_ref[gi]))
    return jnp.concatenate(ys, axis=1) * ps_ref[...]


def _pool_prompt_kernel(x_ref, shm, scm, gtm, shf, scf, nw_ref, pw_ref, ps_ref,
                        xo_ref, hf_ref, st_ref, hp, *, tm, tiles_per_b, n_past):
    i = pl.program_id(0)
    b = i // tiles_per_b
    t = i % tiles_per_b
    d = x_ref.shape[1]
    gd = d // len(POOL_WINDOWS)

    def row(r):
        return r[pl.ds(b, 1), :]

    x = x_ref[...]
    h = _rms(x, nw_ref[0:1, :]) * (1.0 + row(scm)) + row(shm)

    @pl.when(t == 0)
    def _():
        hp[0:16, :] = jnp.zeros((16, d), F32)

    hp[16:16 + tm, :] = h
    pos = t * tm + lax.broadcasted_iota(jnp.int32, (tm, 1), 0) + n_past

    def window_sum(w, c0):
        acc = hp[:, c0:c0 + gd]
        span = 1
        while span < w:
            acc = acc + pltpu.roll(acc, span, 0)
            span *= 2
        return acc[16:16 + tm]

    def recip_cnt(w):
        return 1.0 / jnp.minimum(pos + 1, w).astype(F32)

    y = _pool_groups(h, window_sum, recip_cnt, pw_ref, ps_ref, gd)
    tail = hp[tm:tm + 16, :]
    st_ref[...] = tail
    hp[0:16, :] = tail
    xn = x + row(gtm) * _rms(y, nw_ref[1:2, :])
    xo_ref[...] = xn
    hf_ref[...] = (_rms(xn, nw_ref[2:3, :]) * (1.0 + row(scf)) + row(shf)).astype(BF16)


def _pool_prompt(x2, mods, layer, norm_w, pool_w, pool_scale, seq):
    m, d = x2.shape
    j = layer // 2
    tm = 256
    tiles_per_b = seq // tm
    nb = m // seq
    ms = lambda c: _mod_spec_prompt(d, layer, c, 1)
    return pl.pallas_call(
        functools.partial(_pool_prompt_kernel, tm=tm, tiles_per_b=tiles_per_b, n_past=0),
        grid=(m // tm,),
        in_specs=[pl.BlockSpec((tm, d), lambda i: (i, 0)),
                  ms(0), ms(1), ms(2), ms(3), ms(4),
                  _const_spec((None,) + norm_w.shape[1:], (layer, 0, 0), 1),
                  _const_spec((None,) + pool_w.shape[1:], (j, 0, 0, 0), 1),
                  _const_spec((None, 1, d), (j, 0, 0), 1)],
        out_specs=[pl.BlockSpec((tm, d), lambda i: (i, 0)),
                   pl.BlockSpec((tm, d), lambda i: (i, 0)),
                   pl.BlockSpec((None, 16, d), lambda i: (i // tiles_per_b, 0, 0))],
        out_shape=[jax.ShapeDtypeStruct((m, d), F32),
                   jax.ShapeDtypeStruct((m, d), BF16),
                   jax.ShapeDtypeStruct((nb, 16, d), F32)],
        scratch_shapes=[pltpu.VMEM((tm + 16, d), F32)],
        compiler_params=_params(1),
        name="pool_prompt",
    )(x2, mods, mods, mods, mods, mods, norm_w, pool_w, pool_scale.reshape(-1, 1, d))


def _pool_sample_kernel(x_ref, shm, scm, gtm, shf, scf, nw_ref, pw_ref, ps_ref, c_ref,
                        xo_ref, hf_ref, co_ref, *, n_past):
    d = x_ref.shape[1]
    gd = d // len(POOL_WINDOWS)
    x = x_ref[...]
    h = _rms(x, nw_ref[0:1, :]) * (1.0 + scm[...]) + shm[...]

    def window_sum(w, c0):
        acc = h[:, c0:c0 + gd]
        for j in range(1, w):
            acc = acc + c_ref[POOL_STATE - j, :, c0:c0 + gd]
        return acc

    def recip_cnt(w):
        return 1.0 / float(min(n_past + 1, w))

    y = _pool_groups(h, window_sum, recip_cnt, pw_ref, ps_ref, gd)
    for r in range(POOL_STATE - 1):
        co_ref[r] = c_ref[r + 1]
    co_ref[POOL_STATE - 1] = h
    xn = x + gtm[...] * _rms(y, nw_ref[1:2, :])
    xo_ref[...] = xn
    hf_ref[...] = (_rms(xn, nw_ref[2:3, :]) * (1.0 + scf[...]) + shf[...]).astype(BF16)


def _pool_sample(x2, mods, layer, norm_w, pool_w, pool_scale, cache_t):
    m, d = x2.shape
    j = layer // 2
    bb = 32
    ms = lambda c: _mod_spec_sample(d, bb, layer, c, 1)
    return pl.pallas_call(
        functools.partial(_pool_sample_kernel, n_past=PAST_LEN),
        grid=(m // bb,),
        in_specs=[pl.BlockSpec((bb, d), lambda i: (i, 0)),
                  ms(0), ms(1), ms(2), ms(3), ms(4),
                  _const_spec((None,) + norm_w.shape[1:], (layer, 0, 0), 1),
                  _const_spec((None,) + pool_w.shape[1:], (j, 0, 0, 0), 1),
                  _const_spec((None, 1, d), (j, 0, 0), 1),
                  pl.BlockSpec((None, POOL_STATE, bb, d), lambda i: (j, 0, i, 0))],
        out_specs=[pl.BlockSpec((bb, d), lambda i: (i, 0)),
                   pl.BlockSpec((bb, d), lambda i: (i, 0)),
                   pl.BlockSpec((POOL_STATE, bb, d), lambda i: (0, i, 0))],
        out_shape=[jax.ShapeDtypeStruct((m, d), F32),
                   jax.ShapeDtypeStruct((m, d), BF16),
                   jax.ShapeDtypeStruct((POOL_STATE, m, d), F32)],
        compiler_params=_params(1),
        name="pool_sample",
    )(x2, mods, mods, mods, mods, mods, norm_w, pool_w, pool_scale.reshape(-1, 1, d), cache_t)


def _ffn_a_kernel(h_ref, hs_ref, wg_ref, wu_ref, cw_ref, cs_ref, wd_ref,
                  a_ref, st_ref, as_ref, cso_ref, wdb_ref, wgb, wub, carry, *gbufs,
                  tm, tiles_per_b, nsub):
    i = pl.program_id(1)
    t = i % tiles_per_b
    sub = tm // nsub
    cw0, cw1, cw2 = cw_ref[0:1, :], cw_ref[1:2, :], cw_ref[2:3, :]

    @pl.when(i == 0)
    def _():
        wgb[...] = wg_ref[...].astype(BF16)
        wub[...] = wu_ref[...].astype(BF16)
        hs = hs_ref[...]
        gs = _dot(hs, wgb[...])
        conv = cw2 * gs + cw1 * cs_ref[1] + cw0 * cs_ref[0]
        cso_ref[0] = cs_ref[1]
        cso_ref[1] = gs
        as_ref[...] = (_silu(conv) * _dot(hs, wub[...])).astype(BF16)

    @pl.when(t == 0)
    def _():
        carry[...] = jnp.zeros(carry.shape, F32)

    def gate_matmul(s):
        gbufs[s][8:8 + sub, :] = _dot(h_ref[s * sub:(s + 1) * sub, :], wgb[...])
        gbufs[s][0:8, :] = carry[...] if s == 0 else gbufs[s - 1][sub:sub + 8, :]

    def epilogue(s):
        g = gbufs[s]
        conv = cw2 * g[8:8 + sub, :] + cw1 * g[pl.ds(7, sub), :] + cw0 * g[pl.ds(6, sub), :]
        up = _dot(h_ref[s * sub:(s + 1) * sub, :], wub[...])
        a_ref[s * sub:(s + 1) * sub, :] = (_silu(conv) * up).astype(BF16)

    gate_matmul(0)
    for s in range(nsub):
        if s + 1 < nsub:
            gate_matmul(s + 1)
        epilogue(s)
    tail = gbufs[nsub - 1][sub:sub + 8, :]
    st_ref[...] = tail
    carry[...] = tail
    wdb_ref[...] = wd_ref[...].astype(BF16)


def _ffn_a(h, hs, layer, wg, wu, cw, wd, cache_t, seq):
    m, d = h.shape
    ms = hs.shape[0]
    f = wg.shape[2]
    tm, tf = 1024, 512
    nsub = 4
    sub = tm // nsub
    tiles_per_b = seq // tm
    nb = m // seq
    nrow = m // tm
    slab = f // ((f // tf) * nrow)
    assert slab % 16 == 0 and slab * (f // tf) * nrow == f
    return pl.pallas_call(
        functools.partial(_ffn_a_kernel, tm=tm, tiles_per_b=tiles_per_b, nsub=nsub),
        grid=(f // tf, m // tm),
        in_specs=[pl.BlockSpec((tm, d), lambda j, i: (i, 0)),
                  pl.BlockSpec((ms, d), lambda j, i: (0, 0)),
                  pl.BlockSpec((None, d, tf), lambda j, i: (layer, 0, j)),
                  pl.BlockSpec((None, d, tf), lambda j, i: (layer, 0, j)),
                  pl.BlockSpec((None, cw.shape[1], tf), lambda j, i: (layer, 0, j)),
                  pl.BlockSpec((None, 2, ms, tf), lambda j, i: (layer, 0, 0, j)),
                  pl.BlockSpec((None, slab, d), lambda j, i: (layer, j * nrow + i, 0))],
        out_specs=[pl.BlockSpec((tm, tf), lambda j, i: (i, j)),
                   pl.BlockSpec((None, 8, tf), lambda j, i: (i // tiles_per_b, 0, j)),
                   pl.BlockSpec((ms, tf), lambda j, i: (0, j)),
                   pl.BlockSpec((2, ms, tf), lambda j, i: (0, 0, j)),
                   pl.BlockSpec((None, slab, d), lambda j, i: (0, j * nrow + i, 0))],
        out_shape=[jax.ShapeDtypeStruct((m, f), BF16),
                   jax.ShapeDtypeStruct((nb, 8, f), F32),
                   jax.ShapeDtypeStruct((ms, f), BF16),
                   jax.ShapeDtypeStruct((2, ms, f), F32),
                   jax.ShapeDtypeStruct((1, f, d), BF16)],
        scratch_shapes=([pltpu.VMEM((d, tf), BF16), pltpu.VMEM((d, tf), BF16), pltpu.VMEM((8, tf), F32)]
                        + [pltpu.VMEM((8 + sub, tf), F32)] * nsub),
        compiler_params=_params(2),
        name="ffn_a",
    )(h, hs, wg, wu, cw, cache_t, wd)


def _mm_epi_kernel(*refs, per_row, tiles_per_b, post_idx, next_idx, next_dtype):
    has_next = next_idx is not None
    if has_next:
        a_ref, w_ref, x_ref, gt, nwp, shn, scn, nwn, xo_ref, ho_ref = refs
    else:
        a_ref, w_ref, x_ref, gt, nwp, xo_ref = refs
    b = pl.program_id(0) // tiles_per_b

    def sel(r):
        return r[...] if per_row else r[pl.ds(b, 1), :]

    def unit(v):
        return v * lax.rsqrt(jnp.mean(v * v, axis=-1, keepdims=True) + EPS)

    y = _dot(a_ref[...], w_ref[...])
    xn = x_ref[...] + unit(y) * (sel(gt) * nwp[post_idx:post_idx + 1, :])
    xo_ref[...] = xn
    if has_next:
        scale = nwn[next_idx:next_idx + 1, :] * (1.0 + sel(scn))
        ho_ref[...] = (unit(xn) * scale + sel(shn)).astype(next_dtype)


def _mm_epi(a, w, wl, x, mods, norm_w, layer, gate_chunk, post_idx, nxt, *, per_row, seq, name):
    m, kdim = a.shape
    d = w.shape[2]
    tm = m if per_row else 256
    assert w.dtype == BF16
    tiles_per_b = 1 if per_row else seq // tm
    if per_row:
        ms = lambda l, c: _mod_spec_sample(d, tm, l, c, 1)
    else:
        ms = lambda l, c: _mod_spec_prompt(d, l, c, 1)
    nw_spec = lambda l: _const_spec((None,) + norm_w.shape[1:], (l, 0, 0), 1)
    in_specs = [pl.BlockSpec((tm, kdim), lambda i: (i, 0)),
                pl.BlockSpec((None, kdim, d), lambda i: (wl, 0, 0), pipeline_mode=pl.Buffered(1)),
                pl.BlockSpec((tm, d), lambda i: (i, 0)),
                ms(layer, gate_chunk), nw_spec(layer)]
    args = [a, w, x, mods, norm_w]
    out_specs = [pl.BlockSpec((tm, d), lambda i: (i, 0))]
    out_shape = [jax.ShapeDtypeStruct((m, d), F32)]
    next_idx = next_dtype = None
    if nxt is not None:
        nlayer, next_idx, sh_chunk, sc_chunk, next_dtype = nxt
        in_specs += [ms(nlayer, sh_chunk), ms(nlayer, sc_chunk), nw_spec(nlayer)]
        args += [mods, mods, norm_w]
        out_specs.append(pl.BlockSpec((tm, d), lambda i: (i, 0)))
        out_shape.append(jax.ShapeDtypeStruct((m, d), next_dtype))
    res = pl.pallas_call(
        functools.partial(_mm_epi_kernel, per_row=per_row, tiles_per_b=tiles_per_b,
                          post_idx=post_idx, next_idx=next_idx, next_dtype=next_dtype),
        grid=(m // tm,),
        in_specs=in_specs,
        out_specs=out_specs,
        out_shape=out_shape,
        compiler_params=_params(1),
        name=name,
    )(*args)
    return res if nxt is not None else (res[0], None)


def _proj_kernel(*refs, tm, tiles_per_b, nsub, mode, scale):
    if mode == "z":
        h_ref, hs_ref, w_ref, wx_ref, o_ref, ps_ref, wxb_ref, wb = refs
    else:
        h_ref, hs_ref, w_ref, cw_ref, o_ref, st_ref, ps_ref, wb, buf = refs
    i = pl.program_id(1)
    t = i % tiles_per_b
    sub = tm // nsub
    tn = wb.shape[1]

    @pl.when(i == 0)
    def _():
        wb[...] = w_ref[...].T.astype(BF16)
        ps_ref[...] = _dot(hs_ref[...], wb[...])

    if mode == "z":
        o_ref[...] = _dot(h_ref[...], wb[...]).astype(BF16)
        wxb_ref[...] = wx_ref[...].astype(BF16)
        return

    @pl.when(t == 0)
    def _():
        buf[0:8, :] = jnp.zeros((8, tn), F32)

    def matmul(s):
        buf[8 + s * sub:8 + (s + 1) * sub, :] = _dot(h_ref[s * sub:(s + 1) * sub, :], wb[...])

    def epilogue(s):
        r0 = 8 + s * sub
        conv = (cw_ref[3:4, :] * buf[r0:r0 + sub, :] + cw_ref[2:3, :] * buf[pl.ds(r0 - 1, sub), :]
                + cw_ref[1:2, :] * buf[pl.ds(r0 - 2, sub), :] + cw_ref[0:1, :] * buf[pl.ds(r0 - 3, sub), :])
        act = _silu(conv)
        if mode == "qk":
            outs = []
            for hh in range(tn // HEAD):
                xs = act[:, hh * HEAD:(hh + 1) * HEAD]
                outs.append(xs * (lax.rsqrt(jnp.sum(xs * xs, axis=-1, keepdims=True) + EPS) * scale))
            act = jnp.concatenate(outs, axis=1)
        o_ref[s * sub:(s + 1) * sub, :] = act.astype(BF16)

    matmul(0)
    for s in range(nsub):
        if s + 1 < nsub:
            matmul(s + 1)
        epilogue(s)
    tail = buf[tm:tm + 8, :]
    st_ref[...] = tail
    buf[0:8, :] = tail


def _proj(h, hs, w_t, conv_w, col0, width, seq, mode, scale, name, w_extra=None):
    m, d = h.shape
    ms = hs.shape[0]
    tm, tn = 1024, 1024
    nsub = 4
    tiles_per_b = seq // tm
    nb = m // seq
    cb = col0 // tn
    conv = mode != "z"
    in_specs = [pl.BlockSpec((tm, d), lambda j, i: (i, 0)),
                pl.BlockSpec((ms, d), lambda j, i: (0, 0)),
                pl.BlockSpec((None, tn, d), lambda j, i: (0, cb + j, 0))]
    args = [h, hs, w_t]
    out_specs = [pl.BlockSpec((tm, tn), lambda j, i: (i, j))]
    out_shape = [jax.ShapeDtypeStruct((m, width), BF16)]
    scratch = [pltpu.VMEM((d, tn), BF16)]
    if conv:
        in_specs.append(pl.BlockSpec((None, conv_w.shape[1], tn), lambda j, i: (0, 0, cb + j)))
        args.append(conv_w)
        out_specs.append(pl.BlockSpec((None, 8, tn), lambda j, i: (i // tiles_per_b, 0, j)))
        out_shape.append(jax.ShapeDtypeStruct((nb, 8, width), F32))
        scratch.append(pltpu.VMEM((tm + 8, tn), F32))
    out_specs.append(pl.BlockSpec((ms, tn), lambda j, i: (0, j)))
    out_shape.append(jax.ShapeDtypeStruct((ms, width), F32))
    if not conv:
        nrow = m // tm
        xk, xn = w_extra.shape[1:]
        slab = xk // ((width // tn) * nrow)
        assert slab % 16 == 0 and slab * (width // tn) * nrow == xk
        in_specs.append(pl.BlockSpec((None, slab, xn), lambda j, i: (0, j * nrow + i, 0)))
        args.append(w_extra)
        out_specs.append(pl.BlockSpec((None, slab, xn), lambda j, i: (0, j * nrow + i, 0)))
        out_shape.append(jax.ShapeDtypeStruct(w_extra.shape, BF16))
    return pl.pallas_call(
        functools.partial(_proj_kernel, tm=tm, tiles_per_b=tiles_per_b, nsub=nsub, mode=mode, scale=scale),
        grid=(width // tn, m // tm),
        in_specs=in_specs,
        out_specs=out_specs,
        out_shape=out_shape,
        scratch_shapes=scratch,
        compiler_params=_params(2),
        name=name,
    )(*args)


def _gates_kernel(h_ref, w_ref, al_ref, dt_ref, beta_ref, gc_ref, *, tm, chunk):
    nh = al_ref.shape[1]
    ba = lax.dot_general(h_ref[...], w_ref[...].astype(BF16), NT_DIMS, preferred_element_type=F32)
    beta_ref[...] = jax.nn.sigmoid(ba[:, :nh])
    xx = ba[:, nh:] + dt_ref[...]
    softplus = jnp.maximum(xx, 0.0) + jnp.log1p(jnp.exp(-jnp.abs(xx)))
    g = -jnp.exp(al_ref[...]) * softplus
    shift = int(math.log2(chunk))
    r = lax.broadcasted_iota(jnp.int32, (tm, tm), 0)
    c = lax.broadcasted_iota(jnp.int32, (tm, tm), 1)
    tri = jnp.where(((r >> shift) == (c >> shift)) & (c <= r), 1.0, 0.0).astype(F32)
    gc_ref[...] = jnp.dot(tri, g, precision=lax.Precision.HIGHEST, preferred_element_type=F32)


def _gates(h, w_t, row0, a_log, dt_bias, chunk, name):
    m, d = h.shape
    nh = a_log.shape[1]
    tm = min(m, 256)
    assert row0 % (2 * nh) == 0 and w_t.shape[1] - row0 == 2 * nh
    return pl.pallas_call(
        functools.partial(_gates_kernel, tm=tm, chunk=chunk),
        grid=(m // tm,),
        in_specs=[pl.BlockSpec((tm, d), lambda i: (i, 0)),
                  pl.BlockSpec((None, 2 * nh, d), lambda i: (0, row0 // (2 * nh), 0)),
                  pl.BlockSpec((None, 1, nh), lambda i: (0, 0, 0)),
                  pl.BlockSpec((None, 1, nh), lambda i: (0, 0, 0))],
        out_specs=[pl.BlockSpec((tm, nh), lambda i: (i, 0)),
                   pl.BlockSpec((tm, nh), lambda i: (i, 0))],
        out_shape=[jax.ShapeDtypeStruct((m, nh), F32), jax.ShapeDtypeStruct((m, nh), F32)],
        compiler_params=_params(1),
        name=name,
    )(h, w_t, a_log.reshape(-1, 1, nh), dt_bias.reshape(-1, 1, nh))


def _delta_kernel(q_ref, k_ref, v_ref, z_ref, bt_ref, gc_ref, gct_ref, nw_ref,
                  o_ref, so_ref, s_scr, *, tc, hps, chunk):
    hg = pl.program_id(1)
    t = pl.program_id(2)
    nt = pl.num_programs(2)
    nh = bt_ref.shape[1]
    nck = tc // chunk
    n_factors = int(math.log2(chunk))
    pack = 2 * HEAD // chunk

    @pl.when(t == 0)
    def _():
        s_scr[...] = jnp.zeros(s_scr.shape, F32)

    lane = lax.broadcasted_iota(jnp.int32, (chunk, nh), 1)
    ri = lax.broadcasted_iota(jnp.int32, (chunk, chunk), 0)
    ci = lax.broadcasted_iota(jnp.int32, (chunk, chunk), 1)
    tril = ci <= ri
    strict = ci < ri
    rp = lax.broadcasted_iota(jnp.int32, (chunk, pack * chunk), 0)
    cp = lax.broadcasted_iota(jnp.int32, (chunk, pack * chunk), 1)
    eye_p = jnp.where((cp & (chunk - 1)) == rp, 1.0, 0.0).astype(F32)
    block_of_lane = cp >> int(math.log2(chunk))

    def blockdiag(xp):
        return jnp.concatenate([jnp.where(block_of_lane == b, xp, 0.0) for b in range(pack)], axis=0)

    qk_heads = [(c, jq) for c in range(nck) for jq in range(hps // 2)]
    groups = [(c, g) for c in range(nck) for g in range(hps // pack)]
    heads = [(c, hh) for c in range(nck) for hh in range(hps)]
    kc, qc, kk, qk, low, qkm, bcol, gcol = {}, {}, {}, {}, {}, {}, {}, {}
    for c, jq in qk_heads:
        rows = slice(c * chunk, (c + 1) * chunk)
        qcols = slice(jq * HEAD, (jq + 1) * HEAD)
        qc[c, jq] = q_ref[rows, qcols].astype(F32)
        kc[c, jq] = k_ref[rows, qcols].astype(F32)
        kkqk = lax.dot_general(jnp.concatenate([kc[c, jq], qc[c, jq]], axis=0), kc[c, jq],
                               NT_DIMS, preferred_element_type=F32)
        kk[c, jq] = kkqk[:chunk]
        qk[c, jq] = kkqk[chunk:]
    for c, hh in heads:
        rows = slice(c * chunk, (c + 1) * chunk)
        hidx = hg * hps + hh
        pick = lane == hidx
        bcol[c, hh] = jnp.sum(jnp.where(pick, bt_ref[rows, :], 0.0), axis=1, keepdims=True)
        gcol[c, hh] = jnp.sum(jnp.where(pick, gc_ref[rows, :], 0.0), axis=1, keepdims=True)
        grow = gct_ref[c, pl.ds(hidx, 1), :]
        decay = jnp.exp(jnp.where(tril, gcol[c, hh] - grow, -jnp.inf))
        low[c, hh] = jnp.where(strict, kk[c, hh // 2] * bcol[c, hh] * decay, 0.0)
        qkm[c, hh] = jnp.where(tril, qk[c, hh // 2] * decay, 0.0)
    lowp = {(c, g): jnp.concatenate([low[c, g * pack + r] for r in range(pack)], axis=1)
            for c, g in groups}

    tinv = {p: eye_p - lowp[p] for p in groups}
    pw = {p: _dot(lowp[p], blockdiag(lowp[p])) for p in groups}
    for f in range(n_factors - 1):
        for p in groups:
            bd = blockdiag(pw[p])
            if f == n_factors - 2:
                tinv[p] = tinv[p] + _dot(tinv[p], bd)
            else:
                both = _dot(jnp.concatenate([tinv[p], pw[p]], axis=0), bd)
                tinv[p] = tinv[p] + both[:chunk]
                pw[p] = both[chunk:]

    uw, ecol = {}, {}
    for c, hh in heads:
        rows = slice(c * chunk, (c + 1) * chunk)
        vcols = slice(hh * HEAD, (hh + 1) * HEAD)
        ecol[c, hh] = jnp.exp(gcol[c, hh])
        vc = v_ref[rows, vcols].astype(F32)
        rhs = jnp.concatenate([vc * bcol[c, hh], kc[c, hh // 2] * (bcol[c, hh] * ecol[c, hh])], axis=1)
        r = hh % pack
        uw[c, hh] = _dot(tinv[c, hh // pack][:, r * chunk:(r + 1) * chunk], rhs)

    s = [s_scr[hh] for hh in range(hps)]
    for c in range(nck):
        rows = slice(c * chunk, (c + 1) * chunk)
        ws_qs = [_dot(jnp.concatenate([uw[c, hh][:, HEAD:], qc[c, hh // 2] * ecol[c, hh]], axis=0), s[hh])
                 for hh in range(hps)]
        vnew = [uw[c, hh][:, :HEAD] - ws_qs[hh][:chunk] for hh in range(hps)]
        outs = [ws_qs[hh][chunk:] + _dot(qkm[c, hh], vnew[hh]) for hh in range(hps)]
        for hh in range(hps):
            glast = gcol[c, hh][chunk - 1:chunk, :]
            kd = kc[c, hh // 2] * jnp.exp(glast - gcol[c, hh])
            s[hh] = s[hh] * jnp.exp(glast) + lax.dot_general(kd, vnew[hh], TN_DIMS,
                                                             preferred_element_type=F32)
        for hh in range(hps):
            vcols = slice(hh * HEAD, (hh + 1) * HEAD)
            o = outs[hh]
            og = (o * lax.rsqrt(jnp.mean(o * o, axis=-1, keepdims=True) + EPS) * nw_ref[...]
                  * _silu(z_ref[rows, vcols].astype(F32)))
            o_ref[rows, vcols] = og.astype(BF16)
    for hh in range(hps):
        s_scr[hh] = s[hh]

    @pl.when(t == nt - 1)
    def _():
        so_ref[...] = s_scr[...]


def _delta_prompt(q, k, v, z, beta, gc, norm_w, nb, seq):
    m = q.shape[0]
    chunk = DELTA_CHUNK
    tc = 256
    hps = 8
    nhg = N_V_HEADS // hps
    nt = seq // tc
    vw = hps * HEAD
    qw = vw // 2
    gct = gc.reshape(m // chunk, chunk, N_V_HEADS).transpose(0, 2, 1)
    return pl.pallas_call(
        functools.partial(_delta_kernel, tc=tc, hps=hps, chunk=chunk),
        grid=(nb, nhg, nt),
        in_specs=[pl.BlockSpec((tc, qw), lambda b, g, t: (b * nt + t, g)),
                  pl.BlockSpec((tc, qw), lambda b, g, t: (b * nt + t, g)),
                  pl.BlockSpec((tc, vw), lambda b, g, t: (b * nt + t, g)),
                  pl.BlockSpec((tc, vw), lambda b, g, t: (b * nt + t, g)),
                  pl.BlockSpec((tc, N_V_HEADS), lambda b, g, t: (b * nt + t, 0)),
                  pl.BlockSpec((tc, N_V_HEADS), lambda b, g, t: (b * nt + t, 0)),
                  pl.BlockSpec((tc // chunk, N_V_HEADS, chunk), lambda b, g, t: (b * nt + t, 0, 0)),
                  _const_spec((None, 1, HEAD), (0, 0, 0), 3)],
        out_specs=[pl.BlockSpec((tc, vw), lambda b, g, t: (b * nt + t, g)),
                   pl.BlockSpec((None, None, hps, HEAD, HEAD), lambda b, g, t: (0, b, g, 0, 0))],
        out_shape=[jax.ShapeDtypeStruct((m, N_V_HEADS * HEAD), BF16),
                   jax.ShapeDtypeStruct((1, nb, N_V_HEADS, HEAD, HEAD), F32)],
        scratch_shapes=[pltpu.VMEM((hps, HEAD, HEAD), F32)],
        compiler_params=_params(3),
        name="delta_prompt",
    )(q, k, v, z, beta, gc, gct, norm_w.reshape(-1, 1, HEAD))


def _delta_sample_kernel(p_ref, st_ref, cw_ref, bt_ref, g_ref, nw_ref, s_ref,
                         o_ref, sto_ref, so_ref, qk_scr, *, rows_per_step):
    cdim = st_ref.shape[2]
    key_dim = N_QK_HEADS * HEAD
    rep = N_V_HEADS // N_QK_HEADS
    hs = range(N_V_HEADS)

    def one_sequence(r, carry):
        xrow = p_ref[r, :, 0:cdim]
        conv = (cw_ref[3:4, :] * xrow + cw_ref[2:3, :] * st_ref[r, 2:3, :]
                + cw_ref[1:2, :] * st_ref[r, 1:2, :] + cw_ref[0:1, :] * st_ref[r, 0:1, :])
        sto_ref[r, 0:1, :] = st_ref[r, 1:2, :]
        sto_ref[r, 1:2, :] = st_ref[r, 2:3, :]
        sto_ref[r, 2:3, :] = xrow
        act = _silu(conv)
        qk_scr[...] = jnp.zeros(qk_scr.shape, F32)
        qdotk = []
        for j in range(N_QK_HEADS):
            qs = act[:, j * HEAD:(j + 1) * HEAD]
            qs = qs * (lax.rsqrt(jnp.sum(qs * qs, axis=-1, keepdims=True) + EPS) * (HEAD ** -0.5))
            ks = act[:, key_dim + j * HEAD:key_dim + (j + 1) * HEAD]
            ks = ks * lax.rsqrt(jnp.sum(ks * ks, axis=-1, keepdims=True) + EPS)
            qk_scr[8 * j:8 * j + 1, :] = ks
            qk_scr[8 * j + 1:8 * j + 2, :] = qs
            qdotk.append(jnp.sum(qs * ks, axis=-1, keepdims=True))
        qkt = qk_scr[...].T
        a_row = jnp.exp(g_ref[r])
        b_row = bt_ref[r]
        skq = [_dot(qk_scr[8 * (h // rep):8 * (h // rep) + 8, :], s_ref[r, h]) for h in hs]
        vnew = [b_row[:, h:h + 1] * (act[:, 2 * key_dim + h * HEAD:2 * key_dim + (h + 1) * HEAD]
                                      - a_row[:, h:h + 1] * skq[h][0:1, :]) for h in hs]
        outs = []
        for h in hs:
            j = h // rep
            so_ref[r, h] = a_row[:, h:h + 1] * s_ref[r, h] + qkt[:, 8 * j:8 * j + 1] * vnew[h]
            outs.append(a_row[:, h:h + 1] * skq[h][1:2, :] + qdotk[j] * vnew[h])
        for h in hs:
            o = outs[h]
            z = p_ref[r, :, cdim + h * HEAD:cdim + (h + 1) * HEAD]
            og = o * lax.rsqrt(jnp.mean(o * o, axis=-1, keepdims=True) + EPS) * nw_ref[...] * _silu(z)
            o_ref[r, :, h * HEAD:(h + 1) * HEAD] = og.astype(BF16)
        return carry

    lax.fori_loop(0, rows_per_step, one_sequence, 0)


def _delta_sample(proj, conv_state, conv_w, beta, g, norm_w, s0):
    nb, width = proj.shape
    npast, cdim = conv_state.shape[2:]
    vdim = N_V_HEADS * HEAD
    rb = 4
    return pl.pallas_call(
        functools.partial(_delta_sample_kernel, rows_per_step=rb),
        grid=(nb // rb,),
        in_specs=[pl.BlockSpec((rb, 1, width), lambda b: (b, 0, 0)),
                  pl.BlockSpec((None, rb, npast, cdim), lambda b: (0, b, 0, 0)),
                  _const_spec((None,) + conv_w.shape[1:], (0, 0, 0), 1),
                  pl.BlockSpec((rb, 1, N_V_HEADS), lambda b: (b, 0, 0)),
                  pl.BlockSpec((rb, 1, N_V_HEADS), lambda b: (b, 0, 0)),
                  _const_spec((None, 1, HEAD), (0, 0, 0), 1),
                  pl.BlockSpec((None, rb, N_V_HEADS, HEAD, HEAD), lambda b: (0, b, 0, 0, 0))],
        out_specs=[pl.BlockSpec((rb, 1, vdim), lambda b: (b, 0, 0)),
                   pl.BlockSpec((None, rb, npast, cdim), lambda b: (0, b, 0, 0)),
                   pl.BlockSpec((None, rb, N_V_HEADS, HEAD, HEAD), lambda b: (0, b, 0, 0, 0))],
        out_shape=[jax.ShapeDtypeStruct((nb, 1, vdim), BF16),
                   jax.ShapeDtypeStruct((1,) + conv_state.shape[1:], F32),
                   jax.ShapeDtypeStruct((1,) + s0.shape[1:], F32)],
        scratch_shapes=[pltpu.VMEM((HEAD, HEAD), F32)],
        compiler_params=_params(1),
        name="delta_sample",
    )(proj.reshape(nb, 1, width), conv_state, conv_w, beta.reshape(nb, 1, N_V_HEADS),
      g.reshape(nb, 1, N_V_HEADS), norm_w.reshape(-1, 1, HEAD), s0)


def kernel(x_prompt, x_sample, c_prompt, c_sample, cache_pool, state_conv, state_rec, cache_ffn_conv,
           norm_w, ada_w, ada_b, pool_w, pool_scale, dn_w_in, dn_conv_w, dn_a_log, dn_dt_bias, dn_norm_w,
           dn_w_out, ffn_w_gate, ffn_w_up, ffn_conv_w, ffn_w_down):
    bp, seq, d = x_prompt.shape
    bs = x_sample.shape[0]
    key_dim = N_QK_HEADS * HEAD
    val_dim = N_V_HEADS * HEAD
    conv_dim = 2 * key_dim + val_dim
    assert bs == 128 and bp <= 8 and x_sample.shape[1] == 1
    assert ada_w.shape[0] == 2 and dn_w_in.shape[0] == 1 and pool_w.shape[0] == 1

    c_all = jnp.concatenate([c_sample, c_prompt, jnp.zeros((MOD_ROWS - bs - bp, d), F32)], axis=0)
    mods = _ada(c_all, ada_w, ada_b)

    xp = x_prompt.reshape(bp * seq, d)
    xs = x_sample.reshape(bs, d)

    xp, hp, pool_p16 = _pool_prompt(xp, mods, 0, norm_w, pool_w, pool_scale, seq)
    xs, hs, pool_st = _pool_sample(xs, mods, 0, norm_w, pool_w, pool_scale,
                                   cache_pool.transpose(0, 2, 1, 3))
    pool_p = pool_p16[:, 16 - POOL_STATE:][None]
    pool_s = pool_st.transpose(1, 0, 2)[None]

    ffn_p, ffn_s = [], []
    ffn_cache_t = cache_ffn_conv.transpose(0, 2, 1, 3)

    def ffn(layer, xp, hp, xs, hs, nxt):
        ap, stp, a_s, sts, w_down = _ffn_a(hp, hs, layer, ffn_w_gate, ffn_w_up, ffn_conv_w,
                                           ffn_w_down, ffn_cache_t, seq)
        ffn_p.append(stp[:, 6:8])
        ffn_s.append(sts.transpose(1, 0, 2))
        xp, hp = _mm_epi(ap, w_down, 0, xp, mods, norm_w, layer, 5, 3, nxt,
                         per_row=False, seq=seq, name="ffn_b_prompt")
        xs, hs = _mm_epi(a_s, w_down, 0, xs, mods, norm_w, layer, 5, 3, nxt,
                         per_row=True, seq=1, name="ffn_b_sample")
        return xp, hp, xs, hs

    xp, hp, xs, hs = ffn(0, xp, hp, xs, hs, (1, 0, 0, 1, BF16))

    n_main = conv_dim + val_dim
    w_t = jnp.swapaxes(dn_w_in, 1, 2)
    qn, st_q, pq = _proj(hp, hs, w_t, dn_conv_w, 0, key_dim, seq, "qk", HEAD ** -0.5, "proj_q")
    kn, st_k, pk = _proj(hp, hs, w_t, dn_conv_w, key_dim, key_dim, seq, "qk", 1.0, "proj_k")
    vv, st_v, pv = _proj(hp, hs, w_t, dn_conv_w, 2 * key_dim, val_dim, seq, "v", 1.0, "proj_v")
    zz, pz, w_out = _proj(hp, hs, w_t, dn_conv_w, conv_dim, val_dim, seq, "z", 1.0, "proj_z",
                          w_extra=dn_w_out)
    proj_s = jnp.concatenate([pq, pk, pv, pz], axis=1)
    beta_p, gc_p = _gates(hp, w_t, n_main, dn_a_log, dn_dt_bias, DELTA_CHUNK, "gates_prompt")
    beta_s, g_s = _gates(hs, w_t, n_main, dn_a_log, dn_dt_bias, 1, "gates_sample")
    conv_p = jnp.concatenate([st_q, st_k, st_v], axis=2)[:, 5:8][None]
    o_p, rec_p = _delta_prompt(qn, kn, vv, zz, beta_p, gc_p, dn_norm_w, bp, seq)
    o_s, conv_s, rec_s = _delta_sample(proj_s, state_conv, dn_conv_w, beta_s, g_s, dn_norm_w, state_rec)
    nxt = (1, 2, 3, 4, BF16)
    xp, hp = _mm_epi(o_p, w_out, 0, xp, mods, norm_w, 1, 2, 1, nxt,
                     per_row=False, seq=seq, name="out_prompt")
    xs, hs = _mm_epi(o_s.reshape(bs, val_dim), w_out, 0, xs, mods, norm_w, 1, 2, 1, nxt,
                     per_row=True, seq=1, name="out_sample")
    xp, _, xs, _ = ffn(1, xp, hp, xs, hs, None)

    return (xp.reshape(bp, seq, d), xs.reshape(bs, 1, d), pool_p, pool_s, conv_p, conv_s,
            rec_p, rec_s, jnp.stack(ffn_p), jnp.stack(ffn_s))
```

```python
import functools
import math

import jax
import jax.numpy as jnp
from jax import lax
from jax.experimental import pallas as pl
from jax.experimental.pallas import tpu as pltpu

F32 = jnp.float32
BF16 = jnp.bfloat16
EPS = 1e-6
POOL_WINDOWS = (2, 4, 8, 16)
POOL_STATE = max(POOL_WINDOWS) - 1
HEAD = 128
N_QK_HEADS = 16
N_V_HEADS = 32
DELTA_CHUNK = 128
PAST_LEN = 16384
VMEM_LIMIT = 56 * 1024 * 1024
MOD_ROWS = 136
PROMPT_MOD_BLOCK = 16
NT_DIMS = (((1,), (1,)), ((), ()))
TN_DIMS = (((0,), (0,)), ((), ()))


def _params(n_axes, vmem=VMEM_LIMIT):
    return pltpu.CompilerParams(dimension_semantics=("arbitrary",) * n_axes, vmem_limit_bytes=vmem)


def _rms(x, w):
    return x * lax.rsqrt(jnp.mean(x * x, axis=-1, keepdims=True) + EPS) * w


def _silu(x):
    return x * jax.nn.sigmoid(x)


def _dot(a, b):
    return jnp.dot(a, b, preferred_element_type=F32)


def _ada_kernel(c_ref, w_ref, b_ref, o_ref):
    o_ref[...] = _dot(_silu(c_ref[...]), w_ref[...]) + b_ref[...]


def _ada(c_all, ada_w, ada_b):
    depth, d, n = ada_w.shape
    tn = 1024
    return pl.pallas_call(
        _ada_kernel,
        grid=(depth, n // tn),
        in_specs=[pl.BlockSpec((MOD_ROWS, d), lambda l, j: (0, 0)),
                  pl.BlockSpec((None, d, tn), lambda l, j: (l, 0, j)),
                  pl.BlockSpec((None, 1, tn), lambda l, j: (l, 0, j))],
        out_specs=pl.BlockSpec((None, MOD_ROWS, tn), lambda l, j: (l, 0, j)),
        out_shape=jax.ShapeDtypeStruct((depth, MOD_ROWS, n), F32),
        compiler_params=_params(2),
        name="ada_mod",
    )(c_all, ada_w, ada_b.reshape(depth, 1, n))


def _const_spec(block, idx, n_axes):
    if n_axes == 1:
        return pl.BlockSpec(block, lambda i: idx)
    if n_axes == 2:
        return pl.BlockSpec(block, lambda i, j: idx)
    return pl.BlockSpec(block, lambda i, j, k: idx)


def _mod_spec_prompt(d, layer, chunk, n_axes):
    return _const_spec((None, 8, d), (layer, PROMPT_MOD_BLOCK, chunk), n_axes)


def _mod_spec_sample(d, rows, layer, chunk, n_axes):
    if n_axes == 1:
        return pl.BlockSpec((None, rows, d), lambda i: (layer, i, chunk))
    return pl.BlockSpec((None, rows, d), lambda i, j: (layer, i, chunk))


def _pool_groups(h, window_sum, recip_cnt, pw_ref, ps_ref, gd):
    ys = []
    for gi, w in enumerate(POOL_WINDOWS):
        c0 = gi * gd
        dm = window_sum(w, c0) * recip_cnt(w) - h[:, c0:c0 + gd]
        ys.append(_dot(dm, pw_ref[gi]))
    return jnp.concatenate(ys, axis=1) * ps_ref[...]


def _pool_prompt_kernel(x_ref, shm, scm, gtm, shf, scf, nw_ref, pw_ref, ps_ref,
                        xo_ref, hf_ref, st_ref, hp, *, tm, tiles_per_b, n_past):
    i = pl.program_id(0)
    b = i // tiles_per_b
    t = i % tiles_per_b
    d = x_ref.shape[1]
    gd = d // len(POOL_WINDOWS)

    def row(r):
        return r[pl.ds(b, 1), :]

    x = x_ref[...]
    h = _rms(x, nw_ref[0:1, :] * (1.0 + row(scm))) + row(shm)

    @pl.when(t == 0)
    def _():
        hp[0:16, :] = jnp.zeros((16, d), F32)

    hp[16:16 + tm, :] = h
    pos = t * tm + lax.broadcasted_iota(jnp.int32, (tm, 1), 0) + n_past

    def window_sum(w, c0):
        acc = hp[:, c0:c0 + gd]
        span = 1
        while span < w:
            acc = acc + pltpu.roll(acc, span, 0)
            span *= 2
        return acc[16:16 + tm]

    def recip_cnt(w):
        return 1.0 / jnp.minimum(pos + 1, w).astype(F32)

    y = _pool_groups(h, window_sum, recip_cnt, pw_ref, ps_ref, gd)
    tail = hp[tm:tm + 16, :]
    st_ref[...] = tail
    hp[0:16, :] = tail
    xn = x + _rms(y, nw_ref[1:2, :] * row(gtm))
    xo_ref[...] = xn
    hf_ref[...] = (_rms(xn, nw_ref[2:3, :] * (1.0 + row(scf))) + row(shf)).astype(BF16)


def _pool_prompt(x2, mods, layer, norm_w, pool_w, pool_scale, seq):
    m, d = x2.shape
    j = layer // 2
    tm = 256
    tiles_per_b = seq // tm
    nb = m // seq
    ms = lambda c: _mod_spec_prompt(d, layer, c, 1)
    return pl.pallas_call(
        functools.partial(_pool_prompt_kernel, tm=tm, tiles_per_b=tiles_per_b, n_past=0),
        grid=(m // tm,),
        in_specs=[pl.BlockSpec((tm, d), lambda i: (i, 0)),
                  ms(0), ms(1), ms(2), ms(3), ms(4),
                  _const_spec((None,) + norm_w.shape[1:], (layer, 0, 0), 1),
                  _const_spec((None,) + pool_w.shape[1:], (j, 0, 0, 0), 1),
                  _const_spec((None, 1, d), (j, 0, 0), 1)],
        out_specs=[pl.BlockSpec((tm, d), lambda i: (i, 0)),
                   pl.BlockSpec((tm, d), lambda i: (i, 0)),
                   pl.BlockSpec((None, 16, d), lambda i: (i // tiles_per_b, 0, 0))],
        out_shape=[jax.ShapeDtypeStruct((m, d), F32),
                   jax.ShapeDtypeStruct((m, d), BF16),
                   jax.ShapeDtypeStruct((nb, 16, d), F32)],
        scratch_shapes=[pltpu.VMEM((tm + 16, d), F32)],
        compiler_params=_params(1),
        name="pool_prompt",
    )(x2, mods, mods, mods, mods, mods, norm_w, pool_w, pool_scale.reshape(-1, 1, d))


def _pool_sample_kernel(x_ref, shm, scm, gtm, shf, scf, nw_ref, pw_ref, ps_ref, c_ref,
                        xo_ref, hf_ref, co_ref, *, n_past):
    d = x_ref.shape[1]
    gd = d // len(POOL_WINDOWS)
    x = x_ref[...]
    h = _rms(x, nw_ref[0:1, :]) * (1.0 + scm[...]) + shm[...]

    def window_sum(w, c0):
        acc = h[:, c0:c0 + gd]
        for j in range(1, w):
            acc = acc + c_ref[POOL_STATE - j, :, c0:c0 + gd]
        return acc

    def recip_cnt(w):
        return 1.0 / float(min(n_past + 1, w))

    y = _pool_groups(h, window_sum, recip_cnt, pw_ref, ps_ref, gd)
    for r in range(POOL_STATE - 1):
        co_ref[r] = c_ref[r + 1]
    co_ref[POOL_STATE - 1] = h
    xn = x + gtm[...] * _rms(y, nw_ref[1:2, :])
    xo_ref[...] = xn
    hf_ref[...] = (_rms(xn, nw_ref[2:3, :]) * (1.0 + scf[...]) + shf[...]).astype(BF16)


def _pool_sample(x2, mods, layer, norm_w, pool_w, pool_scale, cache_t):
    m, d = x2.shape
    j = layer // 2
    bb = 32
    ms = lambda c: _mod_spec_sample(d, bb, layer, c, 1)
    return pl.pallas_call(
        functools.partial(_pool_sample_kernel, n_past=PAST_LEN),
        grid=(m // bb,),
        in_specs=[pl.BlockSpec((bb, d), lambda i: (i, 0)),
                  ms(0), ms(1), ms(2), ms(3), ms(4),
                  _const_spec((None,) + norm_w.shape[1:], (layer, 0, 0), 1),
                  _const_spec((None,) + pool_w.shape[1:], (j, 0, 0, 0), 1),
                  _const_spec((None, 1, d), (j, 0, 0), 1),
                  pl.BlockSpec((None, POOL_STATE, bb, d), lambda i: (j, 0, i, 0))],
        out_specs=[pl.BlockSpec((bb, d), lambda i: (i, 0)),
                   pl.BlockSpec((bb, d), lambda i: (i, 0)),
                   pl.BlockSpec((POOL_STATE, bb, d), lambda i: (0, i, 0))],
        out_shape=[jax.ShapeDtypeStruct((m, d), F32),
                   jax.ShapeDtypeStruct((m, d), BF16),
                   jax.ShapeDtypeStruct((POOL_STATE, m, d), F32)],
        compiler_params=_params(1),
        name="pool_sample",
    )(x2, mods, mods, mods, mods, mods, norm_w, pool_w, pool_scale.reshape(-1, 1, d), cache_t)


def _ffn_a_kernel(h_ref, hs_ref, wg_ref, wu_ref, cw_ref, cs_ref, wd_ref,
                  a_ref, st_ref, as_ref, cso_ref, wdb_ref, wgb, wub, carry, *gbufs,
                  tm, tiles_per_b, nsub):
    i = pl.program_id(1)
    t = i % tiles_per_b
    sub = tm // nsub
    cw0, cw1, cw2 = cw_ref[0:1, :], cw_ref[1:2, :], cw_ref[2:3, :]

    @pl.when(i == 0)
    def _():
        wgb[...] = wg_ref[...].astype(BF16)
        wub[...] = wu_ref[...].astype(BF16)
        hs = hs_ref[...]
        gs = _dot(hs, wgb[...])
        conv = cw2 * gs + cw1 * cs_ref[1] + cw0 * cs_ref[0]
        cso_ref[0] = cs_ref[1]
        cso_ref[1] = gs
        as_ref[...] = (_silu(conv) * _dot(hs, wub[...])).astype(BF16)

    @pl.when(t == 0)
    def _():
        carry[...] = jnp.zeros(carry.shape, F32)

    def gate_matmul(s):
        gbufs[s][8:8 + sub, :] = _dot(h_ref[s * sub:(s + 1) * sub, :], wgb[...])
        gbufs[s][0:8, :] = carry[...] if s == 0 else gbufs[s - 1][sub:sub + 8, :]

    def epilogue(s):
        g = gbufs[s]
        conv = cw2 * g[8:8 + sub, :] + cw1 * g[pl.ds(7, sub), :] + cw0 * g[pl.ds(6, sub), :]
        up = _dot(h_ref[s * sub:(s + 1) * sub, :], wub[...])
        a_ref[s * sub:(s + 1) * sub, :] = (_silu(conv) * up).astype(BF16)

    gate_matmul(0)
    for s in range(nsub):
        if s + 1 < nsub:
            gate_matmul(s + 1)
        epilogue(s)
    tail = gbufs[nsub - 1][sub:sub + 8, :]
    st_ref[...] = tail
    carry[...] = tail
    wdb_ref[...] = wd_ref[...].astype(BF16)


def _ffn_a(h, hs, layer, wg, wu, cw, wd, cache_t, seq):
    m, d = h.shape
    ms = hs.shape[0]
    f = wg.shape[2]
    tm, tf = 1024, 512
    nsub = 4
    sub = tm // nsub
    tiles_per_b = seq // tm
    nb = m // seq
    nrow = m // tm
    slab = f // ((f // tf) * nrow)
    assert slab % 16 == 0 and slab * (f // tf) * nrow == f
    return pl.pallas_call(
        functools.partial(_ffn_a_kernel, tm=tm, tiles_per_b=tiles_per_b, nsub=nsub),
        grid=(f // tf, m // tm),
        in_specs=[pl.BlockSpec((tm, d), lambda j, i: (i, 0)),
                  pl.BlockSpec((ms, d), lambda j, i: (0, 0)),
                  pl.BlockSpec((None, d, tf), lambda j, i: (layer, 0, j)),
                  pl.BlockSpec((None, d, tf), lambda j, i: (layer, 0, j)),
                  pl.BlockSpec((None, cw.shape[1], tf), lambda j, i: (layer, 0, j)),
                  pl.BlockSpec((None, 2, ms, tf), lambda j, i: (layer, 0, 0, j)),
                  pl.BlockSpec((None, slab, d), lambda j, i: (layer, j * nrow + i, 0))],
        out_specs=[pl.BlockSpec((tm, tf), lambda j, i: (i, j)),
                   pl.BlockSpec((None, 8, tf), lambda j, i: (i // tiles_per_b, 0, j)),
                   pl.BlockSpec((ms, tf), lambda j, i: (0, j)),
                   pl.BlockSpec((2, ms, tf), lambda j, i: (0, 0, j)),
                   pl.BlockSpec((None, slab, d), lambda j, i: (0, j * nrow + i, 0))],
        out_shape=[jax.ShapeDtypeStruct((m, f), BF16),
                   jax.ShapeDtypeStruct((nb, 8, f), F32),
                   jax.ShapeDtypeStruct((ms, f), BF16),
                   jax.ShapeDtypeStruct((2, ms, f), F32),
                   jax.ShapeDtypeStruct((1, f, d), BF16)],
        scratch_shapes=([pltpu.VMEM((d, tf), BF16), pltpu.VMEM((d, tf), BF16), pltpu.VMEM((8, tf), F32)]
                        + [pltpu.VMEM((8 + sub, tf), F32)] * nsub),
        compiler_params=_params(2),
        name="ffn_a",
    )(h, hs, wg, wu, cw, cache_t, wd)


def _mm_epi_kernel(*refs, per_row, tiles_per_b, post_idx, next_idx, next_dtype):
    has_next = next_idx is not None
    if has_next:
        a_ref, w_ref, x_ref, gt, nwp, shn, scn, nwn, xo_ref, ho_ref = refs
    else:
        a_ref, w_ref, x_ref, gt, nwp, xo_ref = refs
    b = pl.program_id(0) // tiles_per_b

    def sel(r):
        return r[...] if per_row else r[pl.ds(b, 1), :]

    def unit(v):
        return v * lax.rsqrt(jnp.mean(v * v, axis=-1, keepdims=True) + EPS)

    y = _dot(a_ref[...], w_ref[...])
    xn = x_ref[...] + unit(y) * (sel(gt) * nwp[post_idx:post_idx + 1, :])
    xo_ref[...] = xn
    if has_next:
        scale = nwn[next_idx:next_idx + 1, :] * (1.0 + sel(scn))
        ho_ref[...] = (unit(xn) * scale + sel(shn)).astype(next_dtype)


def _mm_epi(a, w, wl, x, mods, norm_w, layer, gate_chunk, post_idx, nxt, *, per_row, seq, name):
    m, kdim = a.shape
    d = w.shape[2]
    tm = m if per_row else 256
    assert w.dtype == BF16
    tiles_per_b = 1 if per_row else seq // tm
    if per_row:
        ms = lambda l, c: _mod_spec_sample(d, tm, l, c, 1)
    else:
        ms = lambda l, c: _mod_spec_prompt(d, l, c, 1)
    nw_spec = lambda l: _const_spec((None,) + norm_w.shape[1:], (l, 0, 0), 1)
    in_specs = [pl.BlockSpec((tm, kdim), lambda i: (i, 0)),
                pl.BlockSpec((None, kdim, d), lambda i: (wl, 0, 0), pipeline_mode=pl.Buffered(1)),
                pl.BlockSpec((tm, d), lambda i: (i, 0)),
                ms(layer, gate_chunk), nw_spec(layer)]
    args = [a, w, x, mods, norm_w]
    out_specs = [pl.BlockSpec((tm, d), lambda i: (i, 0))]
    out_shape = [jax.ShapeDtypeStruct((m, d), F32)]
    next_idx = next_dtype = None
    if nxt is not None:
        nlayer, next_idx, sh_chunk, sc_chunk, next_dtype = nxt
        in_specs += [ms(nlayer, sh_chunk), ms(nlayer, sc_chunk), nw_spec(nlayer)]
        args += [mods, mods, norm_w]
        out_specs.append(pl.BlockSpec((tm, d), lambda i: (i, 0)))
        out_shape.append(jax.ShapeDtypeStruct((m, d), next_dtype))
    res = pl.pallas_call(
        functools.partial(_mm_epi_kernel, per_row=per_row, tiles_per_b=tiles_per_b,
                          post_idx=post_idx, next_idx=next_idx, next_dtype=next_dtype),
        grid=(m // tm,),
        in_specs=in_specs,
        out_specs=out_specs,
        out_shape=out_shape,
        compiler_params=_params(1),
        name=name,
    )(*args)
    return res if nxt is not None else (res[0], None)


def _proj_kernel(*refs, tm, tiles_per_b, nsub, mode, scale):
    if mode == "z":
        h_ref, hs_ref, w_ref, wx_ref, o_ref, ps_ref, wxb_ref, wb = refs
    else:
        h_ref, hs_ref, w_ref, cw_ref, o_ref, st_ref, ps_ref, wb, buf = refs
    i = pl.program_id(1)
    t = i % tiles_per_b
    sub = tm // nsub
    tn = wb.shape[1]

    @pl.when(i == 0)
    def _():
        wb[...] = w_ref[...].T.astype(BF16)
        ps_ref[...] = _dot(hs_ref[...], wb[...])

    if mode == "z":
        o_ref[...] = _dot(h_ref[...], wb[...]).astype(BF16)
        wxb_ref[...] = wx_ref[...].astype(BF16)
        return

    @pl.when(t == 0)
    def _():
        buf[0:8, :] = jnp.zeros((8, tn), F32)

    def matmul(s):
        buf[8 + s * sub:8 + (s + 1) * sub, :] = _dot(h_ref[s * sub:(s + 1) * sub, :], wb[...])

    def epilogue(s):
        r0 = 8 + s * sub
        conv = (cw_ref[3:4, :] * buf[r0:r0 + sub, :] + cw_ref[2:3, :] * buf[pl.ds(r0 - 1, sub), :]
                + cw_ref[1:2, :] * buf[pl.ds(r0 - 2, sub), :] + cw_ref[0:1, :] * buf[pl.ds(r0 - 3, sub), :])
        act = _silu(conv)
        if mode == "qk":
            outs = []
            for hh in range(tn // HEAD):
                xs = act[:, hh * HEAD:(hh + 1) * HEAD]
                outs.append(xs * (lax.rsqrt(jnp.sum(xs * xs, axis=-1, keepdims=True) + EPS) * scale))
            act = jnp.concatenate(outs, axis=1)
        o_ref[s * sub:(s + 1) * sub, :] = act.astype(BF16)

    matmul(0)
    for s in range(nsub):
        if s + 1 < nsub:
            matmul(s + 1)
        epilogue(s)
    tail = buf[tm:tm + 8, :]
    st_ref[...] = tail
    buf[0:8, :] = tail


def _proj(h, hs, w_t, conv_w, col0, width, seq, mode, scale, name, w_extra=None):
    m, d = h.shape
    ms = hs.shape[0]
    tm, tn = 1024, 1024
    nsub = 4
    tiles_per_b = seq // tm
    nb = m // seq
    cb = col0 // tn
    conv = mode != "z"
    in_specs = [pl.BlockSpec((tm, d), lambda j, i: (i, 0)),
                pl.BlockSpec((ms, d), lambda j, i: (0, 0)),
                pl.BlockSpec((None, tn, d), lambda j, i: (0, cb + j, 0))]
    args = [h, hs, w_t]
    out_specs = [pl.BlockSpec((tm, tn), lambda j, i: (i, j))]
    out_shape = [jax.ShapeDtypeStruct((m, width), BF16)]
    scratch = [pltpu.VMEM((d, tn), BF16)]
    if conv:
        in_specs.append(pl.BlockSpec((None, conv_w.shape[1], tn), lambda j, i: (0, 0, cb + j)))
        args.append(conv_w)
        out_specs.append(pl.BlockSpec((None, 8, tn), lambda j, i: (i // tiles_per_b, 0, j)))
        out_shape.append(jax.ShapeDtypeStruct((nb, 8, width), F32))
        scratch.append(pltpu.VMEM((tm + 8, tn), F32))
    out_specs.append(pl.BlockSpec((ms, tn), lambda j, i: (0, j)))
    out_shape.append(jax.ShapeDtypeStruct((ms, width), F32))
    if not conv:
        nrow = m // tm
        xk, xn = w_extra.shape[1:]
        slab = xk // ((width // tn) * nrow)
        assert slab % 16 == 0 and slab * (width // tn) * nrow == xk
        in_specs.append(pl.BlockSpec((None, slab, xn), lambda j, i: (0, j * nrow + i, 0)))
        args.append(w_extra)
        out_specs.append(pl.BlockSpec((None, slab, xn), lambda j, i: (0, j * nrow + i, 0)))
        out_shape.append(jax.ShapeDtypeStruct(w_extra.shape, BF16))
    return pl.pallas_call(
        functools.partial(_proj_kernel, tm=tm, tiles_per_b=tiles_per_b, nsub=nsub, mode=mode, scale=scale),
        grid=(width // tn, m // tm),
        in_specs=in_specs,
        out_specs=out_specs,
        out_shape=out_shape,
        scratch_shapes=scratch,
        compiler_params=_params(2),
        name=name,
    )(*args)


def _gates_kernel(h_ref, w_ref, al_ref, dt_ref, beta_ref, gc_ref, *, tm, chunk):
    nh = al_ref.shape[1]
    ba = lax.dot_general(h_ref[...], w_ref[...].astype(BF16), NT_DIMS, preferred_element_type=F32)
    beta_ref[...] = jax.nn.sigmoid(ba[:, :nh])
    xx = ba[:, nh:] + dt_ref[...]
    softplus = jnp.maximum(xx, 0.0) + jnp.log1p(jnp.exp(-jnp.abs(xx)))
    g = -jnp.exp(al_ref[...]) * softplus
    if chunk == 1:
        gc_ref[...] = g
        return
    r = lax.broadcasted_iota(jnp.int32, (chunk, chunk), 0)
    c = lax.broadcasted_iota(jnp.int32, (chunk, chunk), 1)
    tri = jnp.where(c <= r, 1.0, 0.0).astype(F32)
    for n in range(tm // chunk):
        rows = slice(n * chunk, (n + 1) * chunk)
        gc_ref[rows, :] = jnp.dot(tri, g[rows, :], precision=lax.Precision.HIGHEST,
                                  preferred_element_type=F32)


def _gates(h, w_t, row0, a_log, dt_bias, chunk, name):
    m, d = h.shape
    nh = a_log.shape[1]
    tm = min(m, 1024)
    assert row0 % (2 * nh) == 0 and w_t.shape[1] - row0 == 2 * nh and tm % chunk == 0
    return pl.pallas_call(
        functools.partial(_gates_kernel, tm=tm, chunk=chunk),
        grid=(m // tm,),
        in_specs=[pl.BlockSpec((tm, d), lambda i: (i, 0)),
                  pl.BlockSpec((None, 2 * nh, d), lambda i: (0, row0 // (2 * nh), 0)),
                  pl.BlockSpec((None, 1, nh), lambda i: (0, 0, 0)),
                  pl.BlockSpec((None, 1, nh), lambda i: (0, 0, 0))],
        out_specs=[pl.BlockSpec((tm, nh), lambda i: (i, 0)),
                   pl.BlockSpec((tm, nh), lambda i: (i, 0))],
        out_shape=[jax.ShapeDtypeStruct((m, nh), F32), jax.ShapeDtypeStruct((m, nh), F32)],
        compiler_params=_params(1),
        name=name,
    )(h, w_t, a_log.reshape(-1, 1, nh), dt_bias.reshape(-1, 1, nh))


def _delta_kernel(q_ref, k_ref, v_ref, z_ref, bt_ref, gc_ref, gct_ref, nw_ref,
                  o_ref, so_ref, s_scr, *, tc, hps, chunk):
    hg = pl.program_id(1)
    t = pl.program_id(2)
    nt = pl.num_programs(2)
    nh = bt_ref.shape[1]
    nck = tc // chunk
    n_factors = int(math.log2(chunk))
    pack = 2 * HEAD // chunk

    @pl.when(t == 0)
    def _():
        s_scr[...] = jnp.zeros(s_scr.shape, F32)

    lane = lax.broadcasted_iota(jnp.int32, (chunk, nh), 1)
    ri = lax.broadcasted_iota(jnp.int32, (chunk, chunk), 0)
    ci = lax.broadcasted_iota(jnp.int32, (chunk, chunk), 1)
    tril = ci <= ri
    strict = ci < ri
    rp = lax.broadcasted_iota(jnp.int32, (chunk, pack * chunk), 0)
    cp = lax.broadcasted_iota(jnp.int32, (chunk, pack * chunk), 1)
    eye_p = jnp.where((cp & (chunk - 1)) == rp, 1.0, 0.0).astype(F32)
    block_of_lane = cp >> int(math.log2(chunk))

    def blockdiag(xp):
        return jnp.concatenate([jnp.where(block_of_lane == b, xp, 0.0) for b in range(pack)], axis=0)

    qk_heads = [(c, jq) for c in range(nck) for jq in range(hps // 2)]
    groups = [(c, g) for c in range(nck) for g in range(hps // pack)]
    heads = [(c, hh) for c in range(nck) for hh in range(hps)]
    kc, qc, kk, qk, low, qkm, bcol, gcol = {}, {}, {}, {}, {}, {}, {}, {}
    for c, jq in qk_heads:
        rows = slice(c * chunk, (c + 1) * chunk)
        qcols = slice(jq * HEAD, (jq + 1) * HEAD)
        qc[c, jq] = q_ref[rows, qcols].astype(F32)
        kc[c, jq] = k_ref[rows, qcols].astype(F32)
        kkqk = lax.dot_general(jnp.concatenate([kc[c, jq], qc[c, jq]], axis=0), kc[c, jq],
                               NT_DIMS, preferred_element_type=F32)
        kk[c, jq] = kkqk[:chunk]
        qk[c, jq] = kkqk[chunk:]
    for c, hh in heads:
        rows = slice(c * chunk, (c + 1) * chunk)
        hidx = hg * hps + hh
        pick = lane == hidx
        bcol[c, hh] = jnp.sum(jnp.where(pick, bt_ref[rows, :], 0.0), axis=1, keepdims=True)
        gcol[c, hh] = jnp.sum(jnp.where(pick, gc_ref[rows, :], 0.0), axis=1, keepdims=True)
        grow = gct_ref[c, pl.ds(hidx, 1), :]
        decay = jnp.exp(jnp.where(tril, gcol[c, hh] - grow, -jnp.inf))
        low[c, hh] = jnp.where(strict, kk[c, hh // 2] * bcol[c, hh] * decay, 0.0)
        qkm[c, hh] = jnp.where(tril, qk[c, hh // 2] * decay, 0.0)
    lowp = {(c, g): jnp.concatenate([low[c, g * pack + r] for r in range(pack)], axis=1)
            for c, g in groups}

    tinv = {p: eye_p - lowp[p] for p in groups}
    pw = {p: _dot(lowp[p], blockdiag(lowp[p])) for p in groups}
    for f in range(n_factors - 1):
        for p in groups:
            bd = blockdiag(pw[p])
            if f == n_factors - 2:
                tinv[p] = tinv[p] + _dot(tinv[p], bd)
            else:
                both = _dot(jnp.concatenate([tinv[p], pw[p]], axis=0), bd)
                tinv[p] = tinv[p] + both[:chunk]
                pw[p] = both[chunk:]

    uw, ecol = {}, {}
    for c, hh in heads:
        rows = slice(c * chunk, (c + 1) * chunk)
        vcols = slice(hh * HEAD, (hh + 1) * HEAD)
        ecol[c, hh] = jnp.exp(gcol[c, hh])
        vc = v_ref[rows, vcols].astype(F32)
        rhs = jnp.concatenate([vc * bcol[c, hh], kc[c, hh // 2] * (bcol[c, hh] * ecol[c, hh])], axis=1)
        r = hh % pack
        uw[c, hh] = _dot(tinv[c, hh // pack][:, r * chunk:(r + 1) * chunk], rhs)

    s = [s_scr[hh] for hh in range(hps)]
    for c in range(nck):
        rows = slice(c * chunk, (c + 1) * chunk)
        ws_qs = [_dot(jnp.concatenate([uw[c, hh][:, HEAD:], qc[c, hh // 2] * ecol[c, hh]], axis=0), s[hh])
                 for hh in range(hps)]
        vnew = [uw[c, hh][:, :HEAD] - ws_qs[hh][:chunk] for hh in range(hps)]
        outs = [ws_qs[hh][chunk:] + _dot(qkm[c, hh], vnew[hh]) for hh in range(hps)]
        for hh in range(hps):
            glast = gcol[c, hh][chunk - 1:chunk, :]
            kd = kc[c, hh // 2] * jnp.exp(glast - gcol[c, hh])
            s[hh] = s[hh] * jnp.exp(glast) + lax.dot_general(kd, vnew[hh], TN_DIMS,
                                                             preferred_element_type=F32)
        for hh in range(hps):
            vcols = slice(hh * HEAD, (hh + 1) * HEAD)
            o = outs[hh]
            og = (o * lax.rsqrt(jnp.mean(o * o, axis=-1, keepdims=True) + EPS) * nw_ref[...]
                  * _silu(z_ref[rows, vcols].astype(F32)))
            o_ref[rows, vcols] = og.astype(BF16)
    for hh in range(hps):
        s_scr[hh] = s[hh]

    @pl.when(t == nt - 1)
    def _():
        so_ref[...] = s_scr[...]


def _delta_prompt(q, k, v, z, beta, gc, norm_w, nb, seq):
    m = q.shape[0]
    chunk = DELTA_CHUNK
    tc = 256
    hps = 8
    nhg = N_V_HEADS // hps
    nt = seq // tc
    vw = hps * HEAD
    qw = vw // 2
    gct = gc.reshape(m // chunk, chunk, N_V_HEADS).transpose(0, 2, 1)
    return pl.pallas_call(
        functools.partial(_delta_kernel, tc=tc, hps=hps, chunk=chunk),
        grid=(nb, nhg, nt),
        in_specs=[pl.BlockSpec((tc, qw), lambda b, g, t: (b * nt + t, g)),
                  pl.BlockSpec((tc, qw), lambda b, g, t: (b * nt + t, g)),
                  pl.BlockSpec((tc, vw), lambda b, g, t: (b * nt + t, g)),
                  pl.BlockSpec((tc, vw), lambda b, g, t: (b * nt + t, g)),
                  pl.BlockSpec((tc, N_V_HEADS), lambda b, g, t: (b * nt + t, 0)),
                  pl.BlockSpec((tc, N_V_HEADS), lambda b, g, t: (b * nt + t, 0)),
                  pl.BlockSpec((tc // chunk, N_V_HEADS, chunk), lambda b, g, t: (b * nt + t, 0, 0)),
                  _const_spec((None, 1, HEAD), (0, 0, 0), 3)],
        out_specs=[pl.BlockSpec((tc, vw), lambda b, g, t: (b * nt + t, g)),
                   pl.BlockSpec((None, None, hps, HEAD, HEAD), lambda b, g, t: (0, b, g, 0, 0))],
        out_shape=[jax.ShapeDtypeStruct((m, N_V_HEADS * HEAD), BF16),
                   jax.ShapeDtypeStruct((1, nb, N_V_HEADS, HEAD, HEAD), F32)],
        scratch_shapes=[pltpu.VMEM((hps, HEAD, HEAD), F32)],
        compiler_params=_params(3),
        name="delta_prompt",
    )(q, k, v, z, beta, gc, gct, norm_w.reshape(-1, 1, HEAD))


def _delta_sample_kernel(p_ref, st_ref, cw_ref, bt_ref, g_ref, nw_ref, s_ref,
                         o_ref, sto_ref, so_ref, qk_scr, *, rows_per_step):
    cdim = st_ref.shape[2]
    key_dim = N_QK_HEADS * HEAD
    rep = N_V_HEADS // N_QK_HEADS
    hs = range(N_V_HEADS)

    def one_sequence(r, carry):
        xrow = p_ref[r, :, 0:cdim]
        conv = (cw_ref[3:4, :] * xrow + cw_ref[2:3, :] * st_ref[r, 2:3, :]
                + cw_ref[1:2, :] * st_ref[r, 1:2, :] + cw_ref[0:1, :] * st_ref[r, 0:1, :])
        sto_ref[r, 0:1, :] = st_ref[r, 1:2, :]
        sto_ref[r, 1:2, :] = st_ref[r, 2:3, :]
        sto_ref[r, 2:3, :] = xrow
        act = _silu(conv)
        qk_scr[...] = jnp.zeros(qk_scr.shape, F32)
        qdotk = []
        for j in range(N_QK_HEADS):
            qs = act[:, j * HEAD:(j + 1) * HEAD]
            qs = qs * (lax.rsqrt(jnp.sum(qs * qs, axis=-1, keepdims=True) + EPS) * (HEAD ** -0.5))
            ks = act[:, key_dim + j * HEAD:key_dim + (j + 1) * HEAD]
            ks = ks * lax.rsqrt(jnp.sum(ks * ks, axis=-1, keepdims=True) + EPS)
            qk_scr[8 * j:8 * j + 1, :] = ks
            qk_scr[8 * j + 1:8 * j + 2, :] = qs
            qdotk.append(jnp.sum(qs * ks, axis=-1, keepdims=True))
        qkt = qk_scr[...].T
        a_row = jnp.exp(g_ref[r])
        b_row = bt_ref[r]
        skq = [_dot(qk_scr[8 * (h // rep):8 * (h // rep) + 8, :], s_ref[r, h]) for h in hs]
        vnew = [b_row[:, h:h + 1] * (act[:, 2 * key_dim + h * HEAD:2 * key_dim + (h + 1) * HEAD]
                                      - a_row[:, h:h + 1] * skq[h][0:1, :]) for h in hs]
        outs = []
        for h in hs:
            j = h // rep
            so_ref[r, h] = a_row[:, h:h + 1] * s_ref[r, h] + qkt[:, 8 * j:8 * j + 1] * vnew[h]
            outs.append(a_row[:, h:h + 1] * skq[h][1:2, :] + qdotk[j] * vnew[h])
        for h in hs:
            o = outs[h]
            z = p_ref[r, :, cdim + h * HEAD:cdim + (h + 1) * HEAD]
            og = o * lax.rsqrt(jnp.mean(o * o, axis=-1, keepdims=True) + EPS) * nw_ref[...] * _silu(z)
            o_ref[r, :, h * HEAD:(h + 1) * HEAD] = og.astype(BF16)
        return carry

    lax.fori_loop(0, rows_per_step, one_sequence, 0)


def _delta_sample(proj, conv_state, conv_w, beta, g, norm_w, s0):
    nb, width = proj.shape
    npast, cdim = conv_state.shape[2:]
    vdim = N_V_HEADS * HEAD
    rb = 4
    return pl.pallas_call(
        functools.partial(_delta_sample_kernel, rows_per_step=rb),
        grid=(nb // rb,),
        in_specs=[pl.BlockSpec((rb, 1, width), lambda b: (b, 0, 0)),
                  pl.BlockSpec((None, rb, npast, cdim), lambda b: (0, b, 0, 0)),
                  _const_spec((None,) + conv_w.shape[1:], (0, 0, 0), 1),
                  pl.BlockSpec((rb, 1, N_V_HEADS), lambda b: (b, 0, 0)),
                  pl.BlockSpec((rb, 1, N_V_HEADS), lambda b: (b, 0, 0)),
                  _const_spec((None, 1, HEAD), (0, 0, 0), 1),
                  pl.BlockSpec((None, rb, N_V_HEADS, HEAD, HEAD), lambda b: (0, b, 0, 0, 0))],
        out_specs=[pl.BlockSpec((rb, 1, vdim), lambda b: (b, 0, 0)),
                   pl.BlockSpec((None, rb, npast, cdim), lambda b: (0, b, 0, 0)),
                   pl.BlockSpec((None, rb, N_V_HEADS, HEAD, HEAD), lambda b: (0, b, 0, 0, 0))],
        out_shape=[jax.ShapeDtypeStruct((nb, 1, vdim), BF16),
                   jax.ShapeDtypeStruct((1,) + conv_state.shape[1:], F32),
                   jax.ShapeDtypeStruct((1,) + s0.shape[1:], F32)],
        scratch_shapes=[pltpu.VMEM((HEAD, HEAD), F32)],
        compiler_params=_params(1),
        name="delta_sample",
    )(proj.reshape(nb, 1, width), conv_state, conv_w, beta.reshape(nb, 1, N_V_HEADS),
      g.reshape(nb, 1, N_V_HEADS), norm_w.reshape(-1, 1, HEAD), s0)


def kernel(x_prompt, x_sample, c_prompt, c_sample, cache_pool, state_conv, state_rec, cache_ffn_conv,
           norm_w, ada_w, ada_b, pool_w, pool_scale, dn_w_in, dn_conv_w, dn_a_log, dn_dt_bias, dn_norm_w,
           dn_w_out, ffn_w_gate, ffn_w_up, ffn_conv_w, ffn_w_down):
    bp, seq, d = x_prompt.shape
    bs = x_sample.shape[0]
    key_dim = N_QK_HEADS * HEAD
    val_dim = N_V_HEADS * HEAD
    conv_dim = 2 * key_dim + val_dim
    assert bs == 128 and bp <= 8 and x_sample.shape[1] == 1
    assert ada_w.shape[0] == 2 and dn_w_in.shape[0] == 1 and pool_w.shape[0] == 1

    c_all = jnp.concatenate([c_sample, c_prompt, jnp.zeros((MOD_ROWS - bs - bp, d), F32)], axis=0)
    mods = _ada(c_all, ada_w, ada_b)

    xp = x_prompt.reshape(bp * seq, d)
    xs = x_sample.reshape(bs, d)

    xp, hp, pool_p16 = _pool_prompt(xp, mods, 0, norm_w, pool_w, pool_scale, seq)
    xs, hs, pool_st = _pool_sample(xs, mods, 0, norm_w, pool_w, pool_scale,
                                   cache_pool.transpose(0, 2, 1, 3))
    pool_p = pool_p16[:, 16 - POOL_STATE:][None]
    pool_s = pool_st.transpose(1, 0, 2)[None]

    ffn_p, ffn_s = [], []
    ffn_cache_t = cache_ffn_conv.transpose(0, 2, 1, 3)

    def ffn(layer, xp, hp, xs, hs, nxt):
        ap, stp, a_s, sts, w_down = _ffn_a(hp, hs, layer, ffn_w_gate, ffn_w_up, ffn_conv_w,
                                           ffn_w_down, ffn_cache_t, seq)
        ffn_p.append(stp[:, 6:8])
        ffn_s.append(sts.transpose(1, 0, 2))
        xp, hp = _mm_epi(ap, w_down, 0, xp, mods, norm_w, layer, 5, 3, nxt,
                         per_row=False, seq=seq, name="ffn_b_prompt")
        xs, hs = _mm_epi(a_s, w_down, 0, xs, mods, norm_w, layer, 5, 3, nxt,
                         per_row=True, seq=1, name="ffn_b_sample")
        return xp, hp, xs, hs

    xp, hp, xs, hs = ffn(0, xp, hp, xs, hs, (1, 0, 0, 1, BF16))

    n_main = conv_dim + val_dim
    w_t = jnp.swapaxes(dn_w_in, 1, 2)
    qn, st_q, pq = _proj(hp, hs, w_t, dn_conv_w, 0, key_dim, seq, "qk", HEAD ** -0.5, "proj_q")
    kn, st_k, pk = _proj(hp, hs, w_t, dn_conv_w, key_dim, key_dim, seq, "qk", 1.0, "proj_k")
    vv, st_v, pv = _proj(hp, hs, w_t, dn_conv_w, 2 * key_dim, val_dim, seq, "v", 1.0, "proj_v")
    zz, pz, w_out = _proj(hp, hs, w_t, dn_conv_w, conv_dim, val_dim, seq, "z", 1.0, "proj_z",
                          w_extra=dn_w_out)
    proj_s = jnp.concatenate([pq, pk, pv, pz], axis=1)
    beta_p, gc_p = _gates(hp, w_t, n_main, dn_a_log, dn_dt_bias, DELTA_CHUNK, "gates_prompt")
    beta_s, g_s = _gates(hs, w_t, n_main, dn_a_log, dn_dt_bias, 1, "gates_sample")
    conv_p = jnp.concatenate([st_q, st_k, st_v], axis=2)[:, 5:8][None]
    o_p, rec_p = _delta_prompt(qn, kn, vv, zz, beta_p, gc_p, dn_norm_w, bp, seq)
    o_s, conv_s, rec_s = _delta_sample(proj_s, state_conv, dn_conv_w, beta_s, g_s, dn_norm_w, state_rec)
    nxt = (1, 2, 3, 4, BF16)
    xp, hp = _mm_epi(o_p, w_out, 0, xp, mods, norm_w, 1, 2, 1, nxt,
                     per_row=False, seq=seq, name="out_prompt")
    xs, hs = _mm_epi(o_s.reshape(bs, val_dim), w_out, 0, xs, mods, norm_w, 1, 2, 1, nxt,
                     per_row=True, seq=1, name="out_sample")
    xp, _, xs, _ = ffn(1, xp, hp, xs, hs, None)

    return (xp.reshape(bp, seq, d), xs.reshape(bs, 1, d), pool_p, pool_s, conv_p, conv_s,
            rec_p, rec_s, jnp.stack(ffn_p), jnp.stack(ffn_s))
```

```python
import functools
import math

import jax
import jax.numpy as jnp
from jax import lax
from jax.experimental import pallas as pl
from jax.experimental.pallas import tpu as pltpu

F32 = jnp.float32
BF16 = jnp.bfloat16
EPS = 1e-6
POOL_WINDOWS = (2, 4, 8, 16)
POOL_STATE = max(POOL_WINDOWS) - 1
HEAD = 128
N_QK_HEADS = 16
N_V_HEADS = 32
DELTA_CHUNK = 128
PAST_LEN = 16384
VMEM_LIMIT = 56 * 1024 * 1024
MOD_ROWS = 136
PROMPT_MOD_BLOCK = 16
NT_DIMS = (((1,), (1,)), ((), ()))
TN_DIMS = (((0,), (0,)), ((), ()))


def _params(n_axes, vmem=VMEM_LIMIT):
    return pltpu.CompilerParams(dimension_semantics=("arbitrary",) * n_axes, vmem_limit_bytes=vmem)


def _rms(x, w):
    return x * lax.rsqrt(jnp.mean(x * x, axis=-1, keepdims=True) + EPS) * w


def _silu(x):
    return x * jax.nn.sigmoid(x)


def _dot(a, b):
    return jnp.dot(a, b, preferred_element_type=F32)


def _ada_kernel(c_ref, w_ref, b_ref, o_ref):
    o_ref[...] = _dot(_silu(c_ref[...]), w_ref[...]) + b_ref[...]


def _ada(c_all, ada_w, ada_b):
    depth, d, n = ada_w.shape
    tn = 1024
    return pl.pallas_call(
        _ada_kernel,
        grid=(depth, n // tn),
        in_specs=[pl.BlockSpec((MOD_ROWS, d), lambda l, j: (0, 0)),
                  pl.BlockSpec((None, d, tn), lambda l, j: (l, 0, j)),
                  pl.BlockSpec((None, 1, tn), lambda l, j: (l, 0, j))],
        out_specs=pl.BlockSpec((None, MOD_ROWS, tn), lambda l, j: (l, 0, j)),
        out_shape=jax.ShapeDtypeStruct((depth, MOD_ROWS, n), F32),
        compiler_params=_params(2),
        name="ada_mod",
    )(c_all, ada_w, ada_b.reshape(depth, 1, n))


def _const_spec(block, idx, n_axes):
    if n_axes == 1:
        return pl.BlockSpec(block, lambda i: idx)
    if n_axes == 2:
        return pl.BlockSpec(block, lambda i, j: idx)
    return pl.BlockSpec(block, lambda i, j, k: idx)


def _mod_spec_prompt(d, layer, chunk, n_axes):
    return _const_spec((None, 8, d), (layer, PROMPT_MOD_BLOCK, chunk), n_axes)


def _mod_spec_sample(d, rows, layer, chunk, n_axes):
    if n_axes == 1:
        return pl.BlockSpec((None, rows, d), lambda i: (layer, i, chunk))
    return pl.BlockSpec((None, rows, d), lambda i, j: (layer, i, chunk))


def _pool_groups(h, window_sum, recip_cnt, pw_ref, ps_ref, gd):
    ys = []
    for gi, w in enumerate(POOL_WINDOWS):
        c0 = gi * gd
        dm = window_sum(w, c0) * recip_cnt(w) - h[:, c0:c0 + gd]
        ys.append(_dot(dm, pw_ref[gi]))
    return jnp.concatenate(ys, axis=1) * ps_ref[...]


def _pool_prompt_kernel(x_ref, shm, scm, gtm, shf, scf, nw_ref, pw_ref, ps_ref,
                        xo_ref, hf_ref, st_ref, hp, *, tm, tiles_per_b, n_past):
    i = pl.program_id(0)
    b = i // tiles_per_b
    t = i % tiles_per_b
    d = x_ref.shape[1]
    gd = d // len(POOL_WINDOWS)

    def row(r):
        return r[pl.ds(b, 1), :]

    x = x_ref[...]
    h = _rms(x, nw_ref[0:1, :] * (1.0 + row(scm))) + row(shm)

    @pl.when(t == 0)
    def _():
        hp[0:16, :] = jnp.zeros((16, d), F32)

    hp[16:16 + tm, :] = h
    pos = t * tm + lax.broadcasted_iota(jnp.int32, (tm, 1), 0) + n_past

    def window_sum(w, c0):
        acc = hp[:, c0:c0 + gd]
        span = 1
        while span < w:
            acc = acc + pltpu.roll(acc, span, 0)
            span *= 2
        return acc[16:16 + tm]

    def recip_cnt(w):
        return 1.0 / jnp.minimum(pos + 1, w).astype(F32)

    y = _pool_groups(h, window_sum, recip_cnt, pw_ref, ps_ref, gd)
    tail = hp[tm:tm + 16, :]
    st_ref[...] = tail
    hp[0:16, :] = tail
    xn = x + _rms(y, nw_ref[1:2, :] * row(gtm))
    xo_ref[...] = xn
    hf_ref[...] = (_rms(xn, nw_ref[2:3, :] * (1.0 + row(scf))) + row(shf)).astype(BF16)


def _pool_prompt(x2, mods, layer, norm_w, pool_w, pool_scale, seq):
    m, d = x2.shape
    j = layer // 2
    tm = 256
    tiles_per_b = seq // tm
    nb = m // seq
    ms = lambda c: _mod_spec_prompt(d, layer, c, 1)
    return pl.pallas_call(
        functools.partial(_pool_prompt_kernel, tm=tm, tiles_per_b=tiles_per_b, n_past=0),
        grid=(m // tm,),
        in_specs=[pl.BlockSpec((tm, d), lambda i: (i, 0)),
                  ms(0), ms(1), ms(2), ms(3), ms(4),
                  _const_spec((None,) + norm_w.shape[1:], (layer, 0, 0), 1),
                  _const_spec((None,) + pool_w.shape[1:], (j, 0, 0, 0), 1),
                  _const_spec((None, 1, d), (j, 0, 0), 1)],
        out_specs=[pl.BlockSpec((tm, d), lambda i: (i, 0)),
                   pl.BlockSpec((tm, d), lambda i: (i, 0)),
                   pl.BlockSpec((None, 16, d), lambda i: (i // tiles_per_b, 0, 0))],
        out_shape=[jax.ShapeDtypeStruct((m, d), F32),
                   jax.ShapeDtypeStruct((m, d), BF16),
                   jax.ShapeDtypeStruct((nb, 16, d), F32)],
        scratch_shapes=[pltpu.VMEM((tm + 16, d), F32)],
        compiler_params=_params(1),
        name="pool_prompt",
    )(x2, mods, mods, mods, mods, mods, norm_w, pool_w, pool_scale.reshape(-1, 1, d))


def _pool_sample_kernel(x_ref, shm, scm, gtm, shf, scf, nw_ref, pw_ref, ps_ref, c_ref,
                        xo_ref, hf_ref, co_ref, *, n_past):
    d = x_ref.shape[1]
    gd = d // len(POOL_WINDOWS)
    x = x_ref[...]
    h = _rms(x, nw_ref[0:1, :]) * (1.0 + scm[...]) + shm[...]

    def window_sum(w, c0):
        acc = h[:, c0:c0 + gd]
        for j in range(1, w):
            acc = acc + c_ref[POOL_STATE - j, :, c0:c0 + gd]
        return acc

    def recip_cnt(w):
        return 1.0 / float(min(n_past + 1, w))

    y = _pool_groups(h, window_sum, recip_cnt, pw_ref, ps_ref, gd)
    for r in range(POOL_STATE - 1):
        co_ref[r] = c_ref[r + 1]
    co_ref[POOL_STATE - 1] = h
    xn = x + gtm[...] * _rms(y, nw_ref[1:2, :])
    xo_ref[...] = xn
    hf_ref[...] = (_rms(xn, nw_ref[2:3, :]) * (1.0 + scf[...]) + shf[...]).astype(BF16)


def _pool_sample(x2, mods, layer, norm_w, pool_w, pool_scale, cache_t):
    m, d = x2.shape
    j = layer // 2
    bb = 32
    ms = lambda c: _mod_spec_sample(d, bb, layer, c, 1)
    return pl.pallas_call(
        functools.partial(_pool_sample_kernel, n_past=PAST_LEN),
        grid=(m // bb,),
        in_specs=[pl.BlockSpec((bb, d), lambda i: (i, 0)),
                  ms(0), ms(1), ms(2), ms(3), ms(4),
                  _const_spec((None,) + norm_w.shape[1:], (layer, 0, 0), 1),
                  _const_spec((None,) + pool_w.shape[1:], (j, 0, 0, 0), 1),
                  _const_spec((None, 1, d), (j, 0, 0), 1),
                  pl.BlockSpec((None, POOL_STATE, bb, d), lambda i: (j, 0, i, 0))],
        out_specs=[pl.BlockSpec((bb, d), lambda i: (i, 0)),
                   pl.BlockSpec((bb, d), lambda i: (i, 0)),
                   pl.BlockSpec((POOL_STATE, bb, d), lambda i: (0, i, 0))],
        out_shape=[jax.ShapeDtypeStruct((m, d), F32),
                   jax.ShapeDtypeStruct((m, d), BF16),
                   jax.ShapeDtypeStruct((POOL_STATE, m, d), F32)],
        compiler_params=_params(1),
        name="pool_sample",
    )(x2, mods, mods, mods, mods, mods, norm_w, pool_w, pool_scale.reshape(-1, 1, d), cache_t)


def _ffn_a_kernel(h_ref, hs_ref, wg_ref, wu_ref, cw_ref, cs_ref, wd_ref,
                  a_ref, st_ref, as_ref, cso_ref, wdb_ref, wgb, wub, carry, *gbufs,
                  tm, tiles_per_b, nsub):
    i = pl.program_id(1)
    t = i % tiles_per_b
    sub = tm // nsub
    cw0, cw1, cw2 = cw_ref[0:1, :], cw_ref[1:2, :], cw_ref[2:3, :]

    @pl.when(i == 0)
    def _():
        wgb[...] = wg_ref[...].astype(BF16)
        wub[...] = wu_ref[...].astype(BF16)
        hs = hs_ref[...]
        gs = _dot(hs, wgb[...])
        conv = cw2 * gs + cw1 * cs_ref[1] + cw0 * cs_ref[0]
        cso_ref[0] = cs_ref[1]
        cso_ref[1] = gs
        as_ref[...] = (_silu(conv) * _dot(hs, wub[...])).astype(BF16)

    @pl.when(t == 0)
    def _():
        carry[...] = jnp.zeros(carry.shape, F32)

    def gate_matmul(s):
        gbufs[s][8:8 + sub, :] = _dot(h_ref[s * sub:(s + 1) * sub, :], wgb[...])
        gbufs[s][0:8, :] = carry[...] if s == 0 else gbufs[s - 1][sub:sub + 8, :]

    def epilogue(s):
        g = gbufs[s]
        conv = cw2 * g[8:8 + sub, :] + cw1 * g[pl.ds(7, sub), :] + cw0 * g[pl.ds(6, sub), :]
        up = _dot(h_ref[s * sub:(s + 1) * sub, :], wub[...])
        a_ref[s * sub:(s + 1) * sub, :] = (_silu(conv) * up).astype(BF16)

    gate_matmul(0)
    for s in range(nsub):
        if s + 1 < nsub:
            gate_matmul(s + 1)
        epilogue(s)
    tail = gbufs[nsub - 1][sub:sub + 8, :]
    st_ref[...] = tail
    carry[...] = tail
    wdb_ref[...] = wd_ref[...].astype(BF16)


def _ffn_a(h, hs, layer, wg, wu, cw, wd, cache_t, seq):
    m, d = h.shape
    ms = hs.shape[0]
    f = wg.shape[2]
    tm, tf = 2048, 512
    nsub = 8
    sub = tm // nsub
    tiles_per_b = seq // tm
    nb = m // seq
    nrow = m // tm
    slab = f // ((f // tf) * nrow)
    assert slab % 16 == 0 and slab * (f // tf) * nrow == f
    return pl.pallas_call(
        functools.partial(_ffn_a_kernel, tm=tm, tiles_per_b=tiles_per_b, nsub=nsub),
        grid=(f // tf, m // tm),
        in_specs=[pl.BlockSpec((tm, d), lambda j, i: (i, 0)),
                  pl.BlockSpec((ms, d), lambda j, i: (0, 0)),
                  pl.BlockSpec((None, d, tf), lambda j, i: (layer, 0, j)),
                  pl.BlockSpec((None, d, tf), lambda j, i: (layer, 0, j)),
                  pl.BlockSpec((None, cw.shape[1], tf), lambda j, i: (layer, 0, j)),
                  pl.BlockSpec((None, 2, ms, tf), lambda j, i: (layer, 0, 0, j)),
                  pl.BlockSpec((None, slab, d), lambda j, i: (layer, j * nrow + i, 0))],
        out_specs=[pl.BlockSpec((tm, tf), lambda j, i: (i, j)),
                   pl.BlockSpec((None, 8, tf), lambda j, i: (i // tiles_per_b, 0, j)),
                   pl.BlockSpec((ms, tf), lambda j, i: (0, j)),
                   pl.BlockSpec((2, ms, tf), lambda j, i: (0, 0, j)),
                   pl.BlockSpec((None, slab, d), lambda j, i: (0, j * nrow + i, 0))],
        out_shape=[jax.ShapeDtypeStruct((m, f), BF16),
                   jax.ShapeDtypeStruct((nb, 8, f), F32),
                   jax.ShapeDtypeStruct((ms, f), BF16),
                   jax.ShapeDtypeStruct((2, ms, f), F32),
                   jax.ShapeDtypeStruct((1, f, d), BF16)],
        scratch_shapes=([pltpu.VMEM((d, tf), BF16), pltpu.VMEM((d, tf), BF16), pltpu.VMEM((8, tf), F32)]
                        + [pltpu.VMEM((8 + sub, tf), F32)] * nsub),
        compiler_params=_params(2),
        name="ffn_a",
    )(h, hs, wg, wu, cw, cache_t, wd)


def _mm_epi_kernel(*refs, per_row, tiles_per_b, post_idx, next_idx, next_dtype):
    has_next = next_idx is not None
    if has_next:
        a_ref, w_ref, x_ref, gt, nwp, shn, scn, nwn, xo_ref, ho_ref = refs
    else:
        a_ref, w_ref, x_ref, gt, nwp, xo_ref = refs
    b = pl.program_id(0) // tiles_per_b

    def sel(r):
        return r[...] if per_row else r[pl.ds(b, 1), :]

    def unit(v):
        return v * lax.rsqrt(jnp.mean(v * v, axis=-1, keepdims=True) + EPS)

    y = _dot(a_ref[...], w_ref[...])
    xn = x_ref[...] + unit(y) * (sel(gt) * nwp[post_idx:post_idx + 1, :])
    xo_ref[...] = xn
    if has_next:
        scale = nwn[next_idx:next_idx + 1, :] * (1.0 + sel(scn))
        ho_ref[...] = (unit(xn) * scale + sel(shn)).astype(next_dtype)


def _mm_epi(a, w, wl, x, mods, norm_w, layer, gate_chunk, post_idx, nxt, *, per_row, seq, name):
    m, kdim = a.shape
    d = w.shape[2]
    tm = m if per_row else 256
    assert w.dtype == BF16
    tiles_per_b = 1 if per_row else seq // tm
    if per_row:
        ms = lambda l, c: _mod_spec_sample(d, tm, l, c, 1)
    else:
        ms = lambda l, c: _mod_spec_prompt(d, l, c, 1)
    nw_spec = lambda l: _const_spec((None,) + norm_w.shape[1:], (l, 0, 0), 1)
    in_specs = [pl.BlockSpec((tm, kdim), lambda i: (i, 0)),
                pl.BlockSpec((None, kdim, d), lambda i: (wl, 0, 0), pipeline_mode=pl.Buffered(1)),
                pl.BlockSpec((tm, d), lambda i: (i, 0)),
                ms(layer, gate_chunk), nw_spec(layer)]
    args = [a, w, x, mods, norm_w]
    out_specs = [pl.BlockSpec((tm, d), lambda i: (i, 0))]
    out_shape = [jax.ShapeDtypeStruct((m, d), F32)]
    next_idx = next_dtype = None
    if nxt is not None:
        nlayer, next_idx, sh_chunk, sc_chunk, next_dtype = nxt
        in_specs += [ms(nlayer, sh_chunk), ms(nlayer, sc_chunk), nw_spec(nlayer)]
        args += [mods, mods, norm_w]
        out_specs.append(pl.BlockSpec((tm, d), lambda i: (i, 0)))
        out_shape.append(jax.ShapeDtypeStruct((m, d), next_dtype))
    res = pl.pallas_call(
        functools.partial(_mm_epi_kernel, per_row=per_row, tiles_per_b=tiles_per_b,
                          post_idx=post_idx, next_idx=next_idx, next_dtype=next_dtype),
        grid=(m // tm,),
        in_specs=in_specs,
        out_specs=out_specs,
        out_shape=out_shape,
        compiler_params=_params(1),
        name=name,
    )(*args)
    return res if nxt is not None else (res[0], None)


def _proj_kernel(*refs, tm, tiles_per_b, nsub, mode, scale):
    if mode == "z":
        h_ref, hs_ref, w_ref, wx_ref, o_ref, ps_ref, wxb_ref, wb = refs
    else:
        h_ref, hs_ref, w_ref, cw_ref, o_ref, st_ref, ps_ref, wb, buf = refs
    i = pl.program_id(1)
    t = i % tiles_per_b
    sub = tm // nsub
    tn = wb.shape[1]

    @pl.when(i == 0)
    def _():
        wb[...] = w_ref[...].T.astype(BF16)
        ps_ref[...] = _dot(hs_ref[...], wb[...])

    if mode == "z":
        o_ref[...] = _dot(h_ref[...], wb[...]).astype(BF16)
        wxb_ref[...] = wx_ref[...].astype(BF16)
        return

    @pl.when(t == 0)
    def _():
        buf[0:8, :] = jnp.zeros((8, tn), F32)

    def matmul(s):
        buf[8 + s * sub:8 + (s + 1) * sub, :] = _dot(h_ref[s * sub:(s + 1) * sub, :], wb[...])

    def epilogue(s):
        r0 = 8 + s * sub
        conv = (cw_ref[3:4, :] * buf[r0:r0 + sub, :] + cw_ref[2:3, :] * buf[pl.ds(r0 - 1, sub), :]
                + cw_ref[1:2, :] * buf[pl.ds(r0 - 2, sub), :] + cw_ref[0:1, :] * buf[pl.ds(r0 - 3, sub), :])
        act = _silu(conv)
        if mode == "qk":
            outs = []
            for hh in range(tn // HEAD):
                xs = act[:, hh * HEAD:(hh + 1) * HEAD]
                outs.append(xs * (lax.rsqrt(jnp.sum(xs * xs, axis=-1, keepdims=True) + EPS) * scale))
            act = jnp.concatenate(outs, axis=1)
        o_ref[s * sub:(s + 1) * sub, :] = act.astype(BF16)

    matmul(0)
    for s in range(nsub):
        if s + 1 < nsub:
            matmul(s + 1)
        epilogue(s)
    tail = buf[tm:tm + 8, :]
    st_ref[...] = tail
    buf[0:8, :] = tail


def _proj(h, hs, w_t, conv_w, col0, width, seq, mode, scale, name, w_extra=None):
    m, d = h.shape
    ms = hs.shape[0]
    tm, tn = 2048, 512
    nsub = 8
    tiles_per_b = seq // tm
    nb = m // seq
    cb = col0 // tn
    conv = mode != "z"
    in_specs = [pl.BlockSpec((tm, d), lambda j, i: (i, 0)),
                pl.BlockSpec((ms, d), lambda j, i: (0, 0)),
                pl.BlockSpec((None, tn, d), lambda j, i: (0, cb + j, 0))]
    args = [h, hs, w_t]
    out_specs = [pl.BlockSpec((tm, tn), lambda j, i: (i, j))]
    out_shape = [jax.ShapeDtypeStruct((m, width), BF16)]
    scratch = [pltpu.VMEM((d, tn), BF16)]
    if conv:
        in_specs.append(pl.BlockSpec((None, conv_w.shape[1], tn), lambda j, i: (0, 0, cb + j)))
        args.append(conv_w)
        out_specs.append(pl.BlockSpec((None, 8, tn), lambda j, i: (i // tiles_per_b, 0, j)))
        out_shape.append(jax.ShapeDtypeStruct((nb, 8, width), F32))
        scratch.append(pltpu.VMEM((tm + 8, tn), F32))
    out_specs.append(pl.BlockSpec((ms, tn), lambda j, i: (0, j)))
    out_shape.append(jax.ShapeDtypeStruct((ms, width), F32))
    if not conv:
        nrow = m // tm
        xk, xn = w_extra.shape[1:]
        slab = xk // ((width // tn) * nrow)
        assert slab % 16 == 0 and slab * (width // tn) * nrow == xk
        in_specs.append(pl.BlockSpec((None, slab, xn), lambda j, i: (0, j * nrow + i, 0)))
        args.append(w_extra)
        out_specs.append(pl.BlockSpec((None, slab, xn), lambda j, i: (0, j * nrow + i, 0)))
        out_shape.append(jax.ShapeDtypeStruct(w_extra.shape, BF16))
    return pl.pallas_call(
        functools.partial(_proj_kernel, tm=tm, tiles_per_b=tiles_per_b, nsub=nsub, mode=mode, scale=scale),
        grid=(width // tn, m // tm),
        in_specs=in_specs,
        out_specs=out_specs,
        out_shape=out_shape,
        scratch_shapes=scratch,
        compiler_params=_params(2),
        name=name,
    )(*args)


def _gates_kernel(h_ref, w_ref, al_ref, dt_ref, beta_ref, gc_ref, *, tm, chunk):
    nh = al_ref.shape[1]
    ba = lax.dot_general(h_ref[...], w_ref[...].astype(BF16), NT_DIMS, preferred_element_type=F32)
    beta_ref[...] = jax.nn.sigmoid(ba[:, :nh])
    xx = ba[:, nh:] + dt_ref[...]
    softplus = jnp.maximum(xx, 0.0) + jnp.log1p(jnp.exp(-jnp.abs(xx)))
    g = -jnp.exp(al_ref[...]) * softplus
    if chunk == 1:
        gc_ref[...] = g
        return
    r = lax.broadcasted_iota(jnp.int32, (chunk, chunk), 0)
    c = lax.broadcasted_iota(jnp.int32, (chunk, chunk), 1)
    tri = jnp.where(c <= r, 1.0, 0.0).astype(F32)
    for n in range(tm // chunk):
        rows = slice(n * chunk, (n + 1) * chunk)
        gc_ref[rows, :] = jnp.dot(tri, g[rows, :], precision=lax.Precision.HIGHEST,
                                  preferred_element_type=F32)


def _gates(h, w_t, row0, a_log, dt_bias, chunk, name):
    m, d = h.shape
    nh = a_log.shape[1]
    tm = min(m, 1024)
    assert row0 % (2 * nh) == 0 and w_t.shape[1] - row0 == 2 * nh and tm % chunk == 0
    return pl.pallas_call(
        functools.partial(_gates_kernel, tm=tm, chunk=chunk),
        grid=(m // tm,),
        in_specs=[pl.BlockSpec((tm, d), lambda i: (i, 0)),
                  pl.BlockSpec((None, 2 * nh, d), lambda i: (0, row0 // (2 * nh), 0)),
                  pl.BlockSpec((None, 1, nh), lambda i: (0, 0, 0)),
                  pl.BlockSpec((None, 1, nh), lambda i: (0, 0, 0))],
        out_specs=[pl.BlockSpec((tm, nh), lambda i: (i, 0)),
                   pl.BlockSpec((tm, nh), lambda i: (i, 0))],
        out_shape=[jax.ShapeDtypeStruct((m, nh), F32), jax.ShapeDtypeStruct((m, nh), F32)],
        compiler_params=_params(1),
        name=name,
    )(h, w_t, a_log.reshape(-1, 1, nh), dt_bias.reshape(-1, 1, nh))


def _delta_kernel(q_ref, k_ref, v_ref, z_ref, bt_ref, gc_ref, gct_ref, nw_ref,
                  o_ref, so_ref, s_scr, *, tc, hps, chunk):
    hg = pl.program_id(1)
    t = pl.program_id(2)
    nt = pl.num_programs(2)
    nh = bt_ref.shape[1]
    nck = tc // chunk
    n_factors = int(math.log2(chunk))
    pack = 2 * HEAD // chunk

    @pl.when(t == 0)
    def _():
        s_scr[...] = jnp.zeros(s_scr.shape, F32)

    lane = lax.broadcasted_iota(jnp.int32, (chunk, nh), 1)
    ri = lax.broadcasted_iota(jnp.int32, (chunk, chunk), 0)
    ci = lax.broadcasted_iota(jnp.int32, (chunk, chunk), 1)
    tril = ci <= ri
    strict = ci < ri
    rp = lax.broadcasted_iota(jnp.int32, (chunk, pack * chunk), 0)
    cp = lax.broadcasted_iota(jnp.int32, (chunk, pack * chunk), 1)
    eye_p = jnp.where((cp & (chunk - 1)) == rp, 1.0, 0.0).astype(F32)
    block_of_lane = cp >> int(math.log2(chunk))

    def blockdiag(xp):
        return jnp.concatenate([jnp.where(block_of_lane == b, xp, 0.0) for b in range(pack)], axis=0)

    qk_heads = [(c, jq) for c in range(nck) for jq in range(hps // 2)]
    groups = [(c, g) for c in range(nck) for g in range(hps // pack)]
    heads = [(c, hh) for c in range(nck) for hh in range(hps)]
    kc, qc, kk, qk, low, qkm, bcol, gcol = {}, {}, {}, {}, {}, {}, {}, {}
    for c, jq in qk_heads:
        rows = slice(c * chunk, (c + 1) * chunk)
        qcols = slice(jq * HEAD, (jq + 1) * HEAD)
        qc[c, jq] = q_ref[rows, qcols].astype(F32)
        kc[c, jq] = k_ref[rows, qcols].astype(F32)
        kkqk = lax.dot_general(jnp.concatenate([kc[c, jq], qc[c, jq]], axis=0), kc[c, jq],
                               NT_DIMS, preferred_element_type=F32)
        kk[c, jq] = kkqk[:chunk]
        qk[c, jq] = kkqk[chunk:]
    for c, hh in heads:
        rows = slice(c * chunk, (c + 1) * chunk)
        hidx = hg * hps + hh
        pick = lane == hidx
        bcol[c, hh] = jnp.sum(jnp.where(pick, bt_ref[rows, :], 0.0), axis=1, keepdims=True)
        gcol[c, hh] = jnp.sum(jnp.where(pick, gc_ref[rows, :], 0.0), axis=1, keepdims=True)
        grow = gct_ref[c, pl.ds(hidx, 1), :]
        decay = jnp.exp(jnp.where(tril, gcol[c, hh] - grow, -jnp.inf))
        low[c, hh] = jnp.where(strict, kk[c, hh // 2] * bcol[c, hh] * decay, 0.0)
        qkm[c, hh] = jnp.where(tril, qk[c, hh // 2] * decay, 0.0)
    lowp = {(c, g): jnp.concatenate([low[c, g * pack + r] for r in range(pack)], axis=1)
            for c, g in groups}

    tinv = {p: eye_p - lowp[p] for p in groups}
    pw = {p: _dot(lowp[p], blockdiag(lowp[p])) for p in groups}
    for f in range(n_factors - 1):
        for p in groups:
            bd = blockdiag(pw[p])
            if f == n_factors - 2:
                tinv[p] = tinv[p] + _dot(tinv[p], bd)
            else:
                both = _dot(jnp.concatenate([tinv[p], pw[p]], axis=0), bd)
                tinv[p] = tinv[p] + both[:chunk]
                pw[p] = both[chunk:]

    uw, ecol = {}, {}
    for c, hh in heads:
        rows = slice(c * chunk, (c + 1) * chunk)
        vcols = slice(hh * HEAD, (hh + 1) * HEAD)
        ecol[c, hh] = jnp.exp(gcol[c, hh])
        vc = v_ref[rows, vcols].astype(F32)
        rhs = jnp.concatenate([vc * bcol[c, hh], kc[c, hh // 2] * (bcol[c, hh] * ecol[c, hh])], axis=1)
        r = hh % pack
        uw[c, hh] = _dot(tinv[c, hh // pack][:, r * chunk:(r + 1) * chunk], rhs)

    s = [s_scr[hh] for hh in range(hps)]
    for c in range(nck):
        rows = slice(c * chunk, (c + 1) * chunk)
        ws_qs = [_dot(jnp.concatenate([uw[c, hh][:, HEAD:], qc[c, hh // 2] * ecol[c, hh]], axis=0), s[hh])
                 for hh in range(hps)]
        vnew = [uw[c, hh][:, :HEAD] - ws_qs[hh][:chunk] for hh in range(hps)]
        outs = [ws_qs[hh][chunk:] + _dot(qkm[c, hh], vnew[hh]) for hh in range(hps)]
        for hh in range(hps):
            glast = gcol[c, hh][chunk - 1:chunk, :]
            kd = kc[c, hh // 2] * jnp.exp(glast - gcol[c, hh])
            s[hh] = s[hh] * jnp.exp(glast) + lax.dot_general(kd, vnew[hh], TN_DIMS,
                                                             preferred_element_type=F32)
        for hh in range(hps):
            vcols = slice(hh * HEAD, (hh + 1) * HEAD)
            o = outs[hh]
            og = (o * lax.rsqrt(jnp.mean(o * o, axis=-1, keepdims=True) + EPS) * nw_ref[...]
                  * _silu(z_ref[rows, vcols].astype(F32)))
            o_ref[rows, vcols] = og.astype(BF16)
    for hh in range(hps):
        s_scr[hh] = s[hh]

    @pl.when(t == nt - 1)
    def _():
        so_ref[...] = s_scr[...]


def _delta_prompt(q, k, v, z, beta, gc, norm_w, nb, seq):
    m = q.shape[0]
    chunk = DELTA_CHUNK
    tc = 256
    hps = 8
    nhg = N_V_HEADS // hps
    nt = seq // tc
    vw = hps * HEAD
    qw = vw // 2
    gct = gc.reshape(m // chunk, chunk, N_V_HEADS).transpose(0, 2, 1)
    return pl.pallas_call(
        functools.partial(_delta_kernel, tc=tc, hps=hps, chunk=chunk),
        grid=(nb, nhg, nt),
        in_specs=[pl.BlockSpec((tc, qw), lambda b, g, t: (b * nt + t, g)),
                  pl.BlockSpec((tc, qw), lambda b, g, t: (b * nt + t, g)),
                  pl.BlockSpec((tc, vw), lambda b, g, t: (b * nt + t, g)),
                  pl.BlockSpec((tc, vw), lambda b, g, t: (b * nt + t, g)),
                  pl.BlockSpec((tc, N_V_HEADS), lambda b, g, t: (b * nt + t, 0)),
                  pl.BlockSpec((tc, N_V_HEADS), lambda b, g, t: (b * nt + t, 0)),
                  pl.BlockSpec((tc // chunk, N_V_HEADS, chunk), lambda b, g, t: (b * nt + t, 0, 0)),
                  _const_spec((None, 1, HEAD), (0, 0, 0), 3)],
        out_specs=[pl.BlockSpec((tc, vw), lambda b, g, t: (b * nt + t, g)),
                   pl.BlockSpec((None, None, hps, HEAD, HEAD), lambda b, g, t: (0, b, g, 0, 0))],
        out_shape=[jax.ShapeDtypeStruct((m, N_V_HEADS * HEAD), BF16),
                   jax.ShapeDtypeStruct((1, nb, N_V_HEADS, HEAD, HEAD), F32)],
        scratch_shapes=[pltpu.VMEM((hps, HEAD, HEAD), F32)],
        compiler_params=_params(3),
        name="delta_prompt",
    )(q, k, v, z, beta, gc, gct, norm_w.reshape(-1, 1, HEAD))


def _delta_sample_kernel(p_ref, st_ref, cw_ref, bt_ref, g_ref, nw_ref, s_ref,
                         o_ref, sto_ref, so_ref, qk_scr, *, rows_per_step):
    cdim = st_ref.shape[2]
    key_dim = N_QK_HEADS * HEAD
    rep = N_V_HEADS // N_QK_HEADS
    hs = range(N_V_HEADS)

    def one_sequence(r, carry):
        xrow = p_ref[r, :, 0:cdim]
        conv = (cw_ref[3:4, :] * xrow + cw_ref[2:3, :] * st_ref[r, 2:3, :]
                + cw_ref[1:2, :] * st_ref[r, 1:2, :] + cw_ref[0:1, :] * st_ref[r, 0:1, :])
        sto_ref[r, 0:1, :] = st_ref[r, 1:2, :]
        sto_ref[r, 1:2, :] = st_ref[r, 2:3, :]
        sto_ref[r, 2:3, :] = xrow
        act = _silu(conv)
        qk_scr[...] = jnp.zeros(qk_scr.shape, F32)
        qdotk = []
        for j in range(N_QK_HEADS):
            qs = act[:, j * HEAD:(j + 1) * HEAD]
            qs = qs * (lax.rsqrt(jnp.sum(qs * qs, axis=-1, keepdims=True) + EPS) * (HEAD ** -0.5))
            ks = act[:, key_dim + j * HEAD:key_dim + (j + 1) * HEAD]
            ks = ks * lax.rsqrt(jnp.sum(ks * ks, axis=-1, keepdims=True) + EPS)
            qk_scr[8 * j:8 * j + 1, :] = ks
            qk_scr[8 * j + 1:8 * j + 2, :] = qs
            qdotk.append(jnp.sum(qs * ks, axis=-1, keepdims=True))
        qkt = qk_scr[...].T
        a_row = jnp.exp(g_ref[r])
        b_row = bt_ref[r]
        skq = [_dot(qk_scr[8 * (h // rep):8 * (h // rep) + 8, :], s_ref[r, h]) for h in hs]
        vnew = [b_row[:, h:h + 1] * (act[:, 2 * key_dim + h * HEAD:2 * key_dim + (h + 1) * HEAD]
                                      - a_row[:, h:h + 1] * skq[h][0:1, :]) for h in hs]
        outs = []
        for h in hs:
            j = h // rep
            so_ref[r, h] = a_row[:, h:h + 1] * s_ref[r, h] + qkt[:, 8 * j:8 * j + 1] * vnew[h]
            outs.append(a_row[:, h:h + 1] * skq[h][1:2, :] + qdotk[j] * vnew[h])
        for h in hs:
            o = outs[h]
            z = p_ref[r, :, cdim + h * HEAD:cdim + (h + 1) * HEAD]
            og = o * lax.rsqrt(jnp.mean(o * o, axis=-1, keepdims=True) + EPS) * nw_ref[...] * _silu(z)
            o_ref[r, :, h * HEAD:(h + 1) * HEAD] = og.astype(BF16)
        return carry

    lax.fori_loop(0, rows_per_step, one_sequence, 0)


def _delta_sample(proj, conv_state, conv_w, beta, g, norm_w, s0):
    nb, width = proj.shape
    npast, cdim = conv_state.shape[2:]
    vdim = N_V_HEADS * HEAD
    rb = 4
    return pl.pallas_call(
        functools.partial(_delta_sample_kernel, rows_per_step=rb),
        grid=(nb // rb,),
        in_specs=[pl.BlockSpec((rb, 1, width), lambda b: (b, 0, 0)),
                  pl.BlockSpec((None, rb, npast, cdim), lambda b: (0, b, 0, 0)),
                  _const_spec((None,) + conv_w.shape[1:], (0, 0, 0), 1),
                  pl.BlockSpec((rb, 1, N_V_HEADS), lambda b: (b, 0, 0)),
                  pl.BlockSpec((rb, 1, N_V_HEADS), lambda b: (b, 0, 0)),
                  _const_spec((None, 1, HEAD), (0, 0, 0), 1),
                  pl.BlockSpec((None, rb, N_V_HEADS, HEAD, HEAD), lambda b: (0, b, 0, 0, 0))],
        out_specs=[pl.BlockSpec((rb, 1, vdim), lambda b: (b, 0, 0)),
                   pl.BlockSpec((None, rb, npast, cdim), lambda b: (0, b, 0, 0)),
                   pl.BlockSpec((None, rb, N_V_HEADS, HEAD, HEAD), lambda b: (0, b, 0, 0, 0))],
        out_shape=[jax.ShapeDtypeStruct((nb, 1, vdim), BF16),
                   jax.ShapeDtypeStruct((1,) + conv_state.shape[1:], F32),
                   jax.ShapeDtypeStruct((1,) + s0.shape[1:], F32)],
        scratch_shapes=[pltpu.VMEM((HEAD, HEAD), F32)],
        compiler_params=_params(1),
        name="delta_sample",
    )(proj.reshape(nb, 1, width), conv_state, conv_w, beta.reshape(nb, 1, N_V_HEADS),
      g.reshape(nb, 1, N_V_HEADS), norm_w.reshape(-1, 1, HEAD), s0)


def kernel(x_prompt, x_sample, c_prompt, c_sample, cache_pool, state_conv, state_rec, cache_ffn_conv,
           norm_w, ada_w, ada_b, pool_w, pool_scale, dn_w_in, dn_conv_w, dn_a_log, dn_dt_bias, dn_norm_w,
           dn_w_out, ffn_w_gate, ffn_w_up, ffn_conv_w, ffn_w_down):
    bp, seq, d = x_prompt.shape
    bs = x_sample.shape[0]
    key_dim = N_QK_HEADS * HEAD
    val_dim = N_V_HEADS * HEAD
    conv_dim = 2 * key_dim + val_dim
    assert bs == 128 and bp <= 8 and x_sample.shape[1] == 1
    assert ada_w.shape[0] == 2 and dn_w_in.shape[0] == 1 and pool_w.shape[0] == 1

    c_all = jnp.concatenate([c_sample, c_prompt, jnp.zeros((MOD_ROWS - bs - bp, d), F32)], axis=0)
    mods = _ada(c_all, ada_w, ada_b)

    xp = x_prompt.reshape(bp * seq, d)
    xs = x_sample.reshape(bs, d)

    xp, hp, pool_p16 = _pool_prompt(xp, mods, 0, norm_w, pool_w, pool_scale, seq)
    xs, hs, pool_st = _pool_sample(xs, mods, 0, norm_w, pool_w, pool_scale,
                                   cache_pool.transpose(0, 2, 1, 3))
    pool_p = pool_p16[:, 16 - POOL_STATE:][None]
    pool_s = pool_st.transpose(1, 0, 2)[None]

    ffn_p, ffn_s = [], []
    ffn_cache_t = cache_ffn_conv.transpose(0, 2, 1, 3)

    def ffn(layer, xp, hp, xs, hs, nxt):
        ap, stp, a_s, sts, w_down = _ffn_a(hp, hs, layer, ffn_w_gate, ffn_w_up, ffn_conv_w,
                                           ffn_w_down, ffn_cache_t, seq)
        ffn_p.append(stp[:, 6:8])
        ffn_s.append(sts.transpose(1, 0, 2))
        xp, hp = _mm_epi(ap, w_down, 0, xp, mods, norm_w, layer, 5, 3, nxt,
                         per_row=False, seq=seq, name="ffn_b_prompt")
        xs, hs = _mm_epi(a_s, w_down, 0, xs, mods, norm_w, layer, 5, 3, nxt,
                         per_row=True, seq=1, name="ffn_b_sample")
        return xp, hp, xs, hs

    xp, hp, xs, hs = ffn(0, xp, hp, xs, hs, (1, 0, 0, 1, BF16))

    n_main = conv_dim + val_dim
    w_t = jnp.swapaxes(dn_w_in, 1, 2)
    qn, st_q, pq = _proj(hp, hs, w_t, dn_conv_w, 0, key_dim, seq, "qk", HEAD ** -0.5, "proj_q")
    kn, st_k, pk = _proj(hp, hs, w_t, dn_conv_w, key_dim, key_dim, seq, "qk", 1.0, "proj_k")
    vv, st_v, pv = _proj(hp, hs, w_t, dn_conv_w, 2 * key_dim, val_dim, seq, "v", 1.0, "proj_v")
    zz, pz, w_out = _proj(hp, hs, w_t, dn_conv_w, conv_dim, val_dim, seq, "z", 1.0, "proj_z",
                          w_extra=dn_w_out)
    proj_s = jnp.concatenate([pq, pk, pv, pz], axis=1)
    beta_p, gc_p = _gates(hp, w_t, n_main, dn_a_log, dn_dt_bias, DELTA_CHUNK, "gates_prompt")
    beta_s, g_s = _gates(hs, w_t, n_main, dn_a_log, dn_dt_bias, 1, "gates_sample")
    conv_p = jnp.concatenate([st_q, st_k, st_v], axis=2)[:, 5:8][None]
    o_p, rec_p = _delta_prompt(qn, kn, vv, zz, beta_p, gc_p, dn_norm_w, bp, seq)
    o_s, conv_s, rec_s = _delta_sample(proj_s, state_conv, dn_conv_w, beta_s, g_s, dn_norm_w, state_rec)
    nxt = (1, 2, 3, 4, BF16)
    xp, hp = _mm_epi(o_p, w_out, 0, xp, mods, norm_w, 1, 2, 1, nxt,
                     per_row=False, seq=seq, name="out_prompt")
    xs, hs = _mm_epi(o_s.reshape(bs, val_dim), w_out, 0, xs, mods, norm_w, 1, 2, 1, nxt,
                     per_row=True, seq=1, name="out_sample")
    xp, _, xs, _ = ffn(1, xp, hp, xs, hs, None)

    return (xp.reshape(bp, seq, d), xs.reshape(bs, 1, d), pool_p, pool_s, conv_p, conv_s,
            rec_p, rec_s, jnp.stack(ffn_p), jnp.stack(ffn_s))
```

```python
import functools
import math

import jax
import jax.numpy as jnp
from jax import lax
from jax.experimental import pallas as pl
from jax.experimental.pallas import tpu as pltpu

F32 = jnp.float32
BF16 = jnp.bfloat16
EPS = 1e-6
POOL_WINDOWS = (2, 4, 8, 16)
POOL_STATE = max(POOL_WINDOWS) - 1
HEAD = 128
N_QK_HEADS = 16
N_V_HEADS = 32
DELTA_CHUNK = 128
PAST_LEN = 16384
VMEM_LIMIT = 56 * 1024 * 1024
MOD_ROWS = 136
PROMPT_MOD_BLOCK = 16
NT_DIMS = (((1,), (1,)), ((), ()))
TN_DIMS = (((0,), (0,)), ((), ()))


def _params(n_axes, vmem=VMEM_LIMIT):
    return pltpu.CompilerParams(dimension_semantics=("arbitrary",) * n_axes, vmem_limit_bytes=vmem)


def _rms(x, w):
    return x * lax.rsqrt(jnp.mean(x * x, axis=-1, keepdims=True) + EPS) * w


def _silu(x):
    return x * jax.nn.sigmoid(x)


def _dot(a, b):
    return jnp.dot(a, b, preferred_element_type=F32)


def _ada_kernel(c_ref, w_ref, b_ref, o_ref):
    o_ref[...] = _dot(_silu(c_ref[...]), w_ref[...]) + b_ref[...]


def _ada(c_all, ada_w, ada_b):
    depth, d, n = ada_w.shape
    tn = 2048
    return pl.pallas_call(
        _ada_kernel,
        grid=(depth, n // tn),
        in_specs=[pl.BlockSpec((MOD_ROWS, d), lambda l, j: (0, 0)),
                  pl.BlockSpec((None, d, tn), lambda l, j: (l, 0, j)),
                  pl.BlockSpec((None, 1, tn), lambda l, j: (l, 0, j))],
        out_specs=pl.BlockSpec((None, MOD_ROWS, tn), lambda l, j: (l, 0, j)),
        out_shape=jax.ShapeDtypeStruct((depth, MOD_ROWS, n), F32),
        compiler_params=_params(2),
        name="ada_mod",
    )(c_all, ada_w, ada_b.reshape(depth, 1, n))


def _const_spec(block, idx, n_axes):
    if n_axes == 1:
        return pl.BlockSpec(block, lambda i: idx)
    if n_axes == 2:
        return pl.BlockSpec(block, lambda i, j: idx)
    return pl.BlockSpec(block, lambda i, j, k: idx)


def _mod_spec_prompt(d, layer, chunk, n_axes):
    return _const_spec((None, 8, d), (layer, PROMPT_MOD_BLOCK, chunk), n_axes)


def _mod_spec_sample(d, rows, layer, chunk, n_axes):
    if n_axes == 1:
        return pl.BlockSpec((None, rows, d), lambda i: (layer, i, chunk))
    return pl.BlockSpec((None, rows, d), lambda i, j: (layer, i, chunk))


def _pool_groups(h, window_sum, recip_cnt, pw_ref, ps_ref, gd):
    ys = []
    for gi, w in enumerate(POOL_WINDOWS):
        c0 = gi * gd
        dm = window_sum(w, c0) * recip_cnt(w) - h[:, c0:c0 + gd]
        ys.append(_dot(dm, pw_ref[gi]))
    return jnp.concatenate(ys, axis=1) * ps_ref[...]


def _pool_prompt_kernel(x_ref, shm, scm, gtm, shf, scf, nw_ref, pw_ref, ps_ref,
                        xo_ref, hf_ref, st_ref, hp, *, tm, tiles_per_b, n_past):
    i = pl.program_id(0)
    b = i // tiles_per_b
    t = i % tiles_per_b
    d = x_ref.shape[1]
    gd = d // len(POOL_WINDOWS)

    def row(r):
        return r[pl.ds(b, 1), :]

    x = x_ref[...]
    h = _rms(x, nw_ref[0:1, :] * (1.0 + row(scm))) + row(shm)

    @pl.when(t == 0)
    def _():
        hp[0:16, :] = jnp.zeros((16, d), F32)

    hp[16:16 + tm, :] = h
    pos = t * tm + lax.broadcasted_iota(jnp.int32, (tm, 1), 0) + n_past

    def window_sum(w, c0):
        acc = hp[:, c0:c0 + gd]
        span = 1
        while span < w:
            acc = acc + pltpu.roll(acc, span, 0)
            span *= 2
        return acc[16:16 + tm]

    def recip_cnt(w):
        return 1.0 / jnp.minimum(pos + 1, w).astype(F32)

    y = _pool_groups(h, window_sum, recip_cnt, pw_ref, ps_ref, gd)
    tail = hp[tm:tm + 16, :]
    st_ref[...] = tail
    hp[0:16, :] = tail
    xn = x + _rms(y, nw_ref[1:2, :] * row(gtm))
    xo_ref[...] = xn
    hf_ref[...] = (_rms(xn, nw_ref[2:3, :] * (1.0 + row(scf))) + row(shf)).astype(BF16)


def _pool_prompt(x2, mods, layer, norm_w, pool_w, pool_scale, seq):
    m, d = x2.shape
    j = layer // 2
    tm = 256
    tiles_per_b = seq // tm
    nb = m // seq
    ms = lambda c: _mod_spec_prompt(d, layer, c, 1)
    return pl.pallas_call(
        functools.partial(_pool_prompt_kernel, tm=tm, tiles_per_b=tiles_per_b, n_past=0),
        grid=(m // tm,),
        in_specs=[pl.BlockSpec((tm, d), lambda i: (i, 0)),
                  ms(0), ms(1), ms(2), ms(3), ms(4),
                  _const_spec((None,) + norm_w.shape[1:], (layer, 0, 0), 1),
                  _const_spec((None,) + pool_w.shape[1:], (j, 0, 0, 0), 1),
                  _const_spec((None, 1, d), (j, 0, 0), 1)],
        out_specs=[pl.BlockSpec((tm, d), lambda i: (i, 0)),
                   pl.BlockSpec((tm, d), lambda i: (i, 0)),
                   pl.BlockSpec((None, 16, d), lambda i: (i // tiles_per_b, 0, 0))],
        out_shape=[jax.ShapeDtypeStruct((m, d), F32),
                   jax.ShapeDtypeStruct((m, d), BF16),
                   jax.ShapeDtypeStruct((nb, 16, d), F32)],
        scratch_shapes=[pltpu.VMEM((tm + 16, d), F32)],
        compiler_params=_params(1),
        name="pool_prompt",
    )(x2, mods, mods, mods, mods, mods, norm_w, pool_w, pool_scale.reshape(-1, 1, d))


def _pool_sample_kernel(x_ref, shm, scm, gtm, shf, scf, nw_ref, pw_ref, ps_ref, c_ref,
                        xo_ref, hf_ref, co_ref, *, n_past):
    d = x_ref.shape[1]
    gd = d // len(POOL_WINDOWS)
    x = x_ref[...]
    h = _rms(x, nw_ref[0:1, :]) * (1.0 + scm[...]) + shm[...]

    def window_sum(w, c0):
        acc = h[:, c0:c0 + gd]
        for j in range(1, w):
            acc = acc + c_ref[POOL_STATE - j, :, c0:c0 + gd]
        return acc

    def recip_cnt(w):
        return 1.0 / float(min(n_past + 1, w))

    y = _pool_groups(h, window_sum, recip_cnt, pw_ref, ps_ref, gd)
    for r in range(POOL_STATE - 1):
        co_ref[r] = c_ref[r + 1]
    co_ref[POOL_STATE - 1] = h
    xn = x + gtm[...] * _rms(y, nw_ref[1:2, :])
    xo_ref[...] = xn
    hf_ref[...] = (_rms(xn, nw_ref[2:3, :]) * (1.0 + scf[...]) + shf[...]).astype(BF16)


def _pool_sample(x2, mods, layer, norm_w, pool_w, pool_scale, cache_t):
    m, d = x2.shape
    j = layer // 2
    bb = 32
    ms = lambda c: _mod_spec_sample(d, bb, layer, c, 1)
    return pl.pallas_call(
        functools.partial(_pool_sample_kernel, n_past=PAST_LEN),
        grid=(m // bb,),
        in_specs=[pl.BlockSpec((bb, d), lambda i: (i, 0)),
                  ms(0), ms(1), ms(2), ms(3), ms(4),
                  _const_spec((None,) + norm_w.shape[1:], (layer, 0, 0), 1),
                  _const_spec((None,) + pool_w.shape[1:], (j, 0, 0, 0), 1),
                  _const_spec((None, 1, d), (j, 0, 0), 1),
                  pl.BlockSpec((None, POOL_STATE, bb, d), lambda i: (j, 0, i, 0))],
        out_specs=[pl.BlockSpec((bb, d), lambda i: (i, 0)),
                   pl.BlockSpec((bb, d), lambda i: (i, 0)),
                   pl.BlockSpec((POOL_STATE, bb, d), lambda i: (0, i, 0))],
        out_shape=[jax.ShapeDtypeStruct((m, d), F32),
                   jax.ShapeDtypeStruct((m, d), BF16),
                   jax.ShapeDtypeStruct((POOL_STATE, m, d), F32)],
        compiler_params=_params(1),
        name="pool_sample",
    )(x2, mods, mods, mods, mods, mods, norm_w, pool_w, pool_scale.reshape(-1, 1, d), cache_t)


def _ffn_a_kernel(h_ref, hs_ref, wg_ref, wu_ref, cw_ref, cs_ref, wd_ref,
                  a_ref, st_ref, as_ref, cso_ref, wdb_ref, wgb, wub, carry, *gbufs,
                  tm, tiles_per_b, nsub):
    i = pl.program_id(1)
    t = i % tiles_per_b
    sub = tm // nsub
    cw0, cw1, cw2 = cw_ref[0:1, :], cw_ref[1:2, :], cw_ref[2:3, :]

    @pl.when(i == 0)
    def _():
        wgb[...] = wg_ref[...].astype(BF16)
        wub[...] = wu_ref[...].astype(BF16)
        hs = hs_ref[...]
        gs = _dot(hs, wgb[...])
        conv = cw2 * gs + cw1 * cs_ref[1] + cw0 * cs_ref[0]
        cso_ref[0] = cs_ref[1]
        cso_ref[1] = gs
        as_ref[...] = (_silu(conv) * _dot(hs, wub[...])).astype(BF16)

    @pl.when(t == 0)
    def _():
        carry[...] = jnp.zeros(carry.shape, F32)

    def gate_matmul(s):
        gbufs[s][8:8 + sub, :] = _dot(h_ref[s * sub:(s + 1) * sub, :], wgb[...])
        gbufs[s][0:8, :] = carry[...] if s == 0 else gbufs[s - 1][sub:sub + 8, :]

    def epilogue(s):
        g = gbufs[s]
        conv = cw2 * g[8:8 + sub, :] + cw1 * g[pl.ds(7, sub), :] + cw0 * g[pl.ds(6, sub), :]
        up = _dot(h_ref[s * sub:(s + 1) * sub, :], wub[...])
        a_ref[s * sub:(s + 1) * sub, :] = (_silu(conv) * up).astype(BF16)

    gate_matmul(0)
    for s in range(nsub):
        if s + 1 < nsub:
            gate_matmul(s + 1)
        epilogue(s)
    tail = gbufs[nsub - 1][sub:sub + 8, :]
    st_ref[...] = tail
    carry[...] = tail
    wdb_ref[...] = wd_ref[...].astype(BF16)


def _ffn_a(h, hs, layer, wg, wu, cw, wd, cache_t, seq):
    m, d = h.shape
    ms = hs.shape[0]
    f = wg.shape[2]
    tm, tf = 2048, 512
    nsub = 8
    sub = tm // nsub
    tiles_per_b = seq // tm
    nb = m // seq
    nrow = m // tm
    slab = f // ((f // tf) * nrow)
    assert slab % 16 == 0 and slab * (f // tf) * nrow == f
    return pl.pallas_call(
        functools.partial(_ffn_a_kernel, tm=tm, tiles_per_b=tiles_per_b, nsub=nsub),
        grid=(f // tf, m // tm),
        in_specs=[pl.BlockSpec((tm, d), lambda j, i: (i, 0)),
                  pl.BlockSpec((ms, d), lambda j, i: (0, 0)),
                  pl.BlockSpec((None, d, tf), lambda j, i: (layer, 0, j)),
                  pl.BlockSpec((None, d, tf), lambda j, i: (layer, 0, j)),
                  pl.BlockSpec((None, cw.shape[1], tf), lambda j, i: (layer, 0, j)),
                  pl.BlockSpec((None, 2, ms, tf), lambda j, i: (layer, 0, 0, j)),
                  pl.BlockSpec((None, slab, d), lambda j, i: (layer, j * nrow + i, 0))],
        out_specs=[pl.BlockSpec((tm, tf), lambda j, i: (i, j)),
                   pl.BlockSpec((None, 8, tf), lambda j, i: (i // tiles_per_b, 0, j)),
                   pl.BlockSpec((ms, tf), lambda j, i: (0, j)),
                   pl.BlockSpec((2, ms, tf), lambda j, i: (0, 0, j)),
                   pl.BlockSpec((None, slab, d), lambda j, i: (0, j * nrow + i, 0))],
        out_shape=[jax.ShapeDtypeStruct((m, f), BF16),
                   jax.ShapeDtypeStruct((nb, 8, f), F32),
                   jax.ShapeDtypeStruct((ms, f), BF16),
                   jax.ShapeDtypeStruct((2, ms, f), F32),
                   jax.ShapeDtypeStruct((1, f, d), BF16)],
        scratch_shapes=([pltpu.VMEM((d, tf), BF16), pltpu.VMEM((d, tf), BF16), pltpu.VMEM((8, tf), F32)]
                        + [pltpu.VMEM((8 + sub, tf), F32)] * nsub),
        compiler_params=_params(2),
        name="ffn_a",
    )(h, hs, wg, wu, cw, cache_t, wd)


def _mm_epi_kernel(*refs, per_row, tiles_per_b, post_idx, next_idx, next_dtype):
    has_next = next_idx is not None
    if has_next:
        a_ref, w_ref, x_ref, gt, nwp, shn, scn, nwn, xo_ref, ho_ref = refs
    else:
        a_ref, w_ref, x_ref, gt, nwp, xo_ref = refs
    b = pl.program_id(0) // tiles_per_b

    def sel(r):
        return r[...] if per_row else r[pl.ds(b, 1), :]

    def unit(v):
        return v * lax.rsqrt(jnp.mean(v * v, axis=-1, keepdims=True) + EPS)

    y = _dot(a_ref[...], w_ref[...])
    xn = x_ref[...] + unit(y) * (sel(gt) * nwp[post_idx:post_idx + 1, :])
    xo_ref[...] = xn
    if has_next:
        scale = nwn[next_idx:next_idx + 1, :] * (1.0 + sel(scn))
        ho_ref[...] = (unit(xn) * scale + sel(shn)).astype(next_dtype)


def _mm_epi(a, w, wl, x, mods, norm_w, layer, gate_chunk, post_idx, nxt, *, per_row, seq, name):
    m, kdim = a.shape
    d = w.shape[2]
    tm = m if per_row else 256
    assert w.dtype == BF16
    tiles_per_b = 1 if per_row else seq // tm
    if per_row:
        ms = lambda l, c: _mod_spec_sample(d, tm, l, c, 1)
    else:
        ms = lambda l, c: _mod_spec_prompt(d, l, c, 1)
    nw_spec = lambda l: _const_spec((None,) + norm_w.shape[1:], (l, 0, 0), 1)
    in_specs = [pl.BlockSpec((tm, kdim), lambda i: (i, 0)),
                pl.BlockSpec((None, kdim, d), lambda i: (wl, 0, 0), pipeline_mode=pl.Buffered(1)),
                pl.BlockSpec((tm, d), lambda i: (i, 0)),
                ms(layer, gate_chunk), nw_spec(layer)]
    args = [a, w, x, mods, norm_w]
    out_specs = [pl.BlockSpec((tm, d), lambda i: (i, 0))]
    out_shape = [jax.ShapeDtypeStruct((m, d), F32)]
    next_idx = next_dtype = None
    if nxt is not None:
        nlayer, next_idx, sh_chunk, sc_chunk, next_dtype = nxt
        in_specs += [ms(nlayer, sh_chunk), ms(nlayer, sc_chunk), nw_spec(nlayer)]
        args += [mods, mods, norm_w]
        out_specs.append(pl.BlockSpec((tm, d), lambda i: (i, 0)))
        out_shape.append(jax.ShapeDtypeStruct((m, d), next_dtype))
    res = pl.pallas_call(
        functools.partial(_mm_epi_kernel, per_row=per_row, tiles_per_b=tiles_per_b,
                          post_idx=post_idx, next_idx=next_idx, next_dtype=next_dtype),
        grid=(m // tm,),
        in_specs=in_specs,
        out_specs=out_specs,
        out_shape=out_shape,
        compiler_params=_params(1),
        name=name,
    )(*args)
    return res if nxt is not None else (res[0], None)


def _proj_kernel(*refs, tm, tiles_per_b, nsub, mode, scale):
    if mode == "z":
        h_ref, hs_ref, w_ref, wx_ref, o_ref, ps_ref, wxb_ref, wb = refs
    else:
        h_ref, hs_ref, w_ref, cw_ref, o_ref, st_ref, ps_ref, wb, buf = refs
    i = pl.program_id(1)
    t = i % tiles_per_b
    sub = tm // nsub
    tn = wb.shape[1]

    @pl.when(i == 0)
    def _():
        wb[...] = w_ref[...].T.astype(BF16)
        ps_ref[...] = _dot(hs_ref[...], wb[...])

    if mode == "z":
        o_ref[...] = _dot(h_ref[...], wb[...]).astype(BF16)
        wxb_ref[...] = wx_ref[...].astype(BF16)
        return

    @pl.when(t == 0)
    def _():
        buf[0:8, :] = jnp.zeros((8, tn), F32)

    def matmul(s):
        buf[8 + s * sub:8 + (s + 1) * sub, :] = _dot(h_ref[s * sub:(s + 1) * sub, :], wb[...])

    def epilogue(s):
        r0 = 8 + s * sub
        conv = (cw_ref[3:4, :] * buf[r0:r0 + sub, :] + cw_ref[2:3, :] * buf[pl.ds(r0 - 1, sub), :]
                + cw_ref[1:2, :] * buf[pl.ds(r0 - 2, sub), :] + cw_ref[0:1, :] * buf[pl.ds(r0 - 3, sub), :])
        act = _silu(conv)
        if mode == "qk":
            outs = []
            for hh in range(tn // HEAD):
                xs = act[:, hh * HEAD:(hh + 1) * HEAD]
                outs.append(xs * (lax.rsqrt(jnp.sum(xs * xs, axis=-1, keepdims=True) + EPS) * scale))
            act = jnp.concatenate(outs, axis=1)
        o_ref[s * sub:(s + 1) * sub, :] = act.astype(BF16)

    matmul(0)
    for s in range(nsub):
        if s + 1 < nsub:
            matmul(s + 1)
        epilogue(s)
    tail = buf[tm:tm + 8, :]
    st_ref[...] = tail
    buf[0:8, :] = tail


def _proj(h, hs, w_t, conv_w, col0, width, seq, mode, scale, name, w_extra=None):
    m, d = h.shape
    ms = hs.shape[0]
    tm, tn = (1024, 1024) if mode == "z" else (2048, 512)
    nsub = tm // 256
    tiles_per_b = seq // tm
    nb = m // seq
    cb = col0 // tn
    conv = mode != "z"
    in_specs = [pl.BlockSpec((tm, d), lambda j, i: (i, 0)),
                pl.BlockSpec((ms, d), lambda j, i: (0, 0)),
                pl.BlockSpec((None, tn, d), lambda j, i: (0, cb + j, 0))]
    args = [h, hs, w_t]
    out_specs = [pl.BlockSpec((tm, tn), lambda j, i: (i, j))]
    out_shape = [jax.ShapeDtypeStruct((m, width), BF16)]
    scratch = [pltpu.VMEM((d, tn), BF16)]
    if conv:
        in_specs.append(pl.BlockSpec((None, conv_w.shape[1], tn), lambda j, i: (0, 0, cb + j)))
        args.append(conv_w)
        out_specs.append(pl.BlockSpec((None, 8, tn), lambda j, i: (i // tiles_per_b, 0, j)))
        out_shape.append(jax.ShapeDtypeStruct((nb, 8, width), F32))
        scratch.append(pltpu.VMEM((tm + 8, tn), F32))
    out_specs.append(pl.BlockSpec((ms, tn), lambda j, i: (0, j)))
    out_shape.append(jax.ShapeDtypeStruct((ms, width), F32))
    if not conv:
        nrow = m // tm
        xk, xn = w_extra.shape[1:]
        slab = xk // ((width // tn) * nrow)
        assert slab % 16 == 0 and slab * (width // tn) * nrow == xk
        in_specs.append(pl.BlockSpec((None, slab, xn), lambda j, i: (0, j * nrow + i, 0)))
        args.append(w_extra)
        out_specs.append(pl.BlockSpec((None, slab, xn), lambda j, i: (0, j * nrow + i, 0)))
        out_shape.append(jax.ShapeDtypeStruct(w_extra.shape, BF16))
    return pl.pallas_call(
        functools.partial(_proj_kernel, tm=tm, tiles_per_b=tiles_per_b, nsub=nsub, mode=mode, scale=scale),
        grid=(width // tn, m // tm),
        in_specs=in_specs,
        out_specs=out_specs,
        out_shape=out_shape,
        scratch_shapes=scratch,
        compiler_params=_params(2),
        name=name,
    )(*args)


def _gates_kernel(h_ref, w_ref, al_ref, dt_ref, beta_ref, gc_ref, *, tm, chunk):
    nh = al_ref.shape[1]
    ba = lax.dot_general(h_ref[...], w_ref[...].astype(BF16), NT_DIMS, preferred_element_type=F32)
    beta_ref[...] = jax.nn.sigmoid(ba[:, :nh])
    xx = ba[:, nh:] + dt_ref[...]
    softplus = jnp.maximum(xx, 0.0) + jnp.log1p(jnp.exp(-jnp.abs(xx)))
    g = -jnp.exp(al_ref[...]) * softplus
    if chunk == 1:
        gc_ref[...] = g
        return
    r = lax.broadcasted_iota(jnp.int32, (chunk, chunk), 0)
    c = lax.broadcasted_iota(jnp.int32, (chunk, chunk), 1)
    tri = jnp.where(c <= r, 1.0, 0.0).astype(F32)
    for n in range(tm // chunk):
        rows = slice(n * chunk, (n + 1) * chunk)
        gc_ref[rows, :] = jnp.dot(tri, g[rows, :], precision=lax.Precision.HIGHEST,
                                  preferred_element_type=F32)


def _gates(h, w_t, row0, a_log, dt_bias, chunk, name):
    m, d = h.shape
    nh = a_log.shape[1]
    tm = min(m, 1024)
    assert row0 % (2 * nh) == 0 and w_t.shape[1] - row0 == 2 * nh and tm % chunk == 0
    return pl.pallas_call(
        functools.partial(_gates_kernel, tm=tm, chunk=chunk),
        grid=(m // tm,),
        in_specs=[pl.BlockSpec((tm, d), lambda i: (i, 0)),
                  pl.BlockSpec((None, 2 * nh, d), lambda i: (0, row0 // (2 * nh), 0)),
                  pl.BlockSpec((None, 1, nh), lambda i: (0, 0, 0)),
                  pl.BlockSpec((None, 1, nh), lambda i: (0, 0, 0))],
        out_specs=[pl.BlockSpec((tm, nh), lambda i: (i, 0)),
                   pl.BlockSpec((tm, nh), lambda i: (i, 0))],
        out_shape=[jax.ShapeDtypeStruct((m, nh), F32), jax.ShapeDtypeStruct((m, nh), F32)],
        compiler_params=_params(1),
        name=name,
    )(h, w_t, a_log.reshape(-1, 1, nh), dt_bias.reshape(-1, 1, nh))


def _delta_kernel(q_ref, k_ref, v_ref, z_ref, bt_ref, gc_ref, gct_ref, nw_ref,
                  o_ref, so_ref, s_scr, *, tc, hps, chunk):
    hg = pl.program_id(1)
    t = pl.program_id(2)
    nt = pl.num_programs(2)
    nh = bt_ref.shape[1]
    nck = tc // chunk
    n_factors = int(math.log2(chunk))
    pack = 2 * HEAD // chunk

    @pl.when(t == 0)
    def _():
        s_scr[...] = jnp.zeros(s_scr.shape, F32)

    lane = lax.broadcasted_iota(jnp.int32, (chunk, nh), 1)
    ri = lax.broadcasted_iota(jnp.int32, (chunk, chunk), 0)
    ci = lax.broadcasted_iota(jnp.int32, (chunk, chunk), 1)
    tril = ci <= ri
    strict = ci < ri
    rp = lax.broadcasted_iota(jnp.int32, (chunk, pack * chunk), 0)
    cp = lax.broadcasted_iota(jnp.int32, (chunk, pack * chunk), 1)
    eye_p = jnp.where((cp & (chunk - 1)) == rp, 1.0, 0.0).astype(F32)
    block_of_lane = cp >> int(math.log2(chunk))

    def blockdiag(xp):
        return jnp.concatenate([jnp.where(block_of_lane == b, xp, 0.0) for b in range(pack)], axis=0)

    qk_heads = [(c, jq) for c in range(nck) for jq in range(hps // 2)]
    groups = [(c, g) for c in range(nck) for g in range(hps // pack)]
    heads = [(c, hh) for c in range(nck) for hh in range(hps)]
    kc, qc, kk, qk, low, qkm, bcol, gcol = {}, {}, {}, {}, {}, {}, {}, {}
    for c, jq in qk_heads:
        rows = slice(c * chunk, (c + 1) * chunk)
        qcols = slice(jq * HEAD, (jq + 1) * HEAD)
        qc[c, jq] = q_ref[rows, qcols].astype(F32)
        kc[c, jq] = k_ref[rows, qcols].astype(F32)
        kkqk = lax.dot_general(jnp.concatenate([kc[c, jq], qc[c, jq]], axis=0), kc[c, jq],
                               NT_DIMS, preferred_element_type=F32)
        kk[c, jq] = kkqk[:chunk]
        qk[c, jq] = kkqk[chunk:]
    for c, hh in heads:
        rows = slice(c * chunk, (c + 1) * chunk)
        hidx = hg * hps + hh
        pick = lane == hidx
        bcol[c, hh] = jnp.sum(jnp.where(pick, bt_ref[rows, :], 0.0), axis=1, keepdims=True)
        gcol[c, hh] = jnp.sum(jnp.where(pick, gc_ref[rows, :], 0.0), axis=1, keepdims=True)
        grow = gct_ref[c, pl.ds(hidx, 1), :]
        decay = jnp.exp(jnp.where(tril, gcol[c, hh] - grow, -jnp.inf))
        low[c, hh] = jnp.where(strict, kk[c, hh // 2] * bcol[c, hh] * decay, 0.0)
        qkm[c, hh] = jnp.where(tril, qk[c, hh // 2] * decay, 0.0)
    lowp = {(c, g): jnp.concatenate([low[c, g * pack + r] for r in range(pack)], axis=1)
            for c, g in groups}

    tinv = {p: eye_p - lowp[p] for p in groups}
    pw = {p: _dot(lowp[p], blockdiag(lowp[p])) for p in groups}
    for f in range(n_factors - 1):
        for p in groups:
            bd = blockdiag(pw[p])
            if f == n_factors - 2:
                tinv[p] = tinv[p] + _dot(tinv[p], bd)
            else:
                both = _dot(jnp.concatenate([tinv[p], pw[p]], axis=0), bd)
                tinv[p] = tinv[p] + both[:chunk]
                pw[p] = both[chunk:]

    uw, ecol = {}, {}
    for c, hh in heads:
        rows = slice(c * chunk, (c + 1) * chunk)
        vcols = slice(hh * HEAD, (hh + 1) * HEAD)
        ecol[c, hh] = jnp.exp(gcol[c, hh])
        vc = v_ref[rows, vcols].astype(F32)
        rhs = jnp.concatenate([vc * bcol[c, hh], kc[c, hh // 2] * (bcol[c, hh] * ecol[c, hh])], axis=1)
        r = hh % pack
        uw[c, hh] = _dot(tinv[c, hh // pack][:, r * chunk:(r + 1) * chunk], rhs)

    s = [s_scr[hh] for hh in range(hps)]
    for c in range(nck):
        rows = slice(c * chunk, (c + 1) * chunk)
        ws_qs = [_dot(jnp.concatenate([uw[c, hh][:, HEAD:], qc[c, hh // 2] * ecol[c, hh]], axis=0), s[hh])
                 for hh in range(hps)]
        vnew = [uw[c, hh][:, :HEAD] - ws_qs[hh][:chunk] for hh in range(hps)]
        outs = [ws_qs[hh][chunk:] + _dot(qkm[c, hh], vnew[hh]) for hh in range(hps)]
        for hh in range(hps):
            glast = gcol[c, hh][chunk - 1:chunk, :]
            kd = kc[c, hh // 2] * jnp.exp(glast - gcol[c, hh])
            s[hh] = s[hh] * jnp.exp(glast) + lax.dot_general(kd, vnew[hh], TN_DIMS,
                                                             preferred_element_type=F32)
        for hh in range(hps):
            vcols = slice(hh * HEAD, (hh + 1) * HEAD)
            o = outs[hh]
            og = (o * lax.rsqrt(jnp.mean(o * o, axis=-1, keepdims=True) + EPS) * nw_ref[...]
                  * _silu(z_ref[rows, vcols].astype(F32)))
            o_ref[rows, vcols] = og.astype(BF16)
    for hh in range(hps):
        s_scr[hh] = s[hh]

    @pl.when(t == nt - 1)
    def _():
        so_ref[...] = s_scr[...]


def _delta_prompt(q, k, v, z, beta, gc, norm_w, nb, seq):
    m = q.shape[0]
    chunk = DELTA_CHUNK
    tc = 256
    hps = 8
    nhg = N_V_HEADS // hps
    nt = seq // tc
    vw = hps * HEAD
    qw = vw // 2
    gct = gc.reshape(m // chunk, chunk, N_V_HEADS).transpose(0, 2, 1)
    return pl.pallas_call(
        functools.partial(_delta_kernel, tc=tc, hps=hps, chunk=chunk),
        grid=(nb, nhg, nt),
        in_specs=[pl.BlockSpec((tc, qw), lambda b, g, t: (b * nt + t, g)),
                  pl.BlockSpec((tc, qw), lambda b, g, t: (b * nt + t, g)),
                  pl.BlockSpec((tc, vw), lambda b, g, t: (b * nt + t, g)),
                  pl.BlockSpec((tc, vw), lambda b, g, t: (b * nt + t, g)),
                  pl.BlockSpec((tc, N_V_HEADS), lambda b, g, t: (b * nt + t, 0)),
                  pl.BlockSpec((tc, N_V_HEADS), lambda b, g, t: (b * nt + t, 0)),
                  pl.BlockSpec((tc // chunk, N_V_HEADS, chunk), lambda b, g, t: (b * nt + t, 0, 0)),
                  _const_spec((None, 1, HEAD), (0, 0, 0), 3)],
        out_specs=[pl.BlockSpec((tc, vw), lambda b, g, t: (b * nt + t, g)),
                   pl.BlockSpec((None, None, hps, HEAD, HEAD), lambda b, g, t: (0, b, g, 0, 0))],
        out_shape=[jax.ShapeDtypeStruct((m, N_V_HEADS * HEAD), BF16),
                   jax.ShapeDtypeStruct((1, nb, N_V_HEADS, HEAD, HEAD), F32)],
        scratch_shapes=[pltpu.VMEM((hps, HEAD, HEAD), F32)],
        compiler_params=_params(3),
        name="delta_prompt",
    )(q, k, v, z, beta, gc, gct, norm_w.reshape(-1, 1, HEAD))


def _delta_sample_kernel(p_ref, st_ref, cw_ref, bt_ref, g_ref, nw_ref, s_ref,
                         o_ref, sto_ref, so_ref, qk_scr, *, rows_per_step):
    cdim = st_ref.shape[2]
    key_dim = N_QK_HEADS * HEAD
    rep = N_V_HEADS // N_QK_HEADS
    hs = range(N_V_HEADS)

    def one_sequence(r, carry):
        xrow = p_ref[r, :, 0:cdim]
        conv = (cw_ref[3:4, :] * xrow + cw_ref[2:3, :] * st_ref[r, 2:3, :]
                + cw_ref[1:2, :] * st_ref[r, 1:2, :] + cw_ref[0:1, :] * st_ref[r, 0:1, :])
        sto_ref[r, 0:1, :] = st_ref[r, 1:2, :]
        sto_ref[r, 1:2, :] = st_ref[r, 2:3, :]
        sto_ref[r, 2:3, :] = xrow
        act = _silu(conv)
        qk_scr[...] = jnp.zeros(qk_scr.shape, F32)
        qdotk = []
        for j in range(N_QK_HEADS):
            qs = act[:, j * HEAD:(j + 1) * HEAD]
            qs = qs * (lax.rsqrt(jnp.sum(qs * qs, axis=-1, keepdims=True) + EPS) * (HEAD ** -0.5))
            ks = act[:, key_dim + j * HEAD:key_dim + (j + 1) * HEAD]
            ks = ks * lax.rsqrt(jnp.sum(ks * ks, axis=-1, keepdims=True) + EPS)
            qk_scr[8 * j:8 * j + 1, :] = ks
            qk_scr[8 * j + 1:8 * j + 2, :] = qs
            qdotk.append(jnp.sum(qs * ks, axis=-1, keepdims=True))
        qkt = qk_scr[...].T
        a_row = jnp.exp(g_ref[r])
        b_row = bt_ref[r]
        skq = [_dot(qk_scr[8 * (h // rep):8 * (h // rep) + 8, :], s_ref[r, h]) for h in hs]
        vnew = [b_row[:, h:h + 1] * (act[:, 2 * key_dim + h * HEAD:2 * key_dim + (h + 1) * HEAD]
                                      - a_row[:, h:h + 1] * skq[h][0:1, :]) for h in hs]
        outs = []
        for h in hs:
            j = h // rep
            so_ref[r, h] = a_row[:, h:h + 1] * s_ref[r, h] + qkt[:, 8 * j:8 * j + 1] * vnew[h]
            outs.append(a_row[:, h:h + 1] * skq[h][1:2, :] + qdotk[j] * vnew[h])
        for h in hs:
            o = outs[h]
            z = p_ref[r, :, cdim + h * HEAD:cdim + (h + 1) * HEAD]
            og = o * lax.rsqrt(jnp.mean(o * o, axis=-1, keepdims=True) + EPS) * nw_ref[...] * _silu(z)
            o_ref[r, :, h * HEAD:(h + 1) * HEAD] = og.astype(BF16)
        return carry

    lax.fori_loop(0, rows_per_step, one_sequence, 0)


def _delta_sample(proj, conv_state, conv_w, beta, g, norm_w, s0):
    nb, width = proj.shape
    npast, cdim = conv_state.shape[2:]
    vdim = N_V_HEADS * HEAD
    rb = 4
    return pl.pallas_call(
        functools.partial(_delta_sample_kernel, rows_per_step=rb),
        grid=(nb // rb,),
        in_specs=[pl.BlockSpec((rb, 1, width), lambda b: (b, 0, 0)),
                  pl.BlockSpec((None, rb, npast, cdim), lambda b: (0, b, 0, 0)),
                  _const_spec((None,) + conv_w.shape[1:], (0, 0, 0), 1),
                  pl.BlockSpec((rb, 1, N_V_HEADS), lambda b: (b, 0, 0)),
                  pl.BlockSpec((rb, 1, N_V_HEADS), lambda b: (b, 0, 0)),
                  _const_spec((None, 1, HEAD), (0, 0, 0), 1),
                  pl.BlockSpec((None, rb, N_V_HEADS, HEAD, HEAD), lambda b: (0, b, 0, 0, 0))],
        out_specs=[pl.BlockSpec((rb, 1, vdim), lambda b: (b, 0, 0)),
                   pl.BlockSpec((None, rb, npast, cdim), lambda b: (0, b, 0, 0)),
                   pl.BlockSpec((None, rb, N_V_HEADS, HEAD, HEAD), lambda b: (0, b, 0, 0, 0))],
        out_shape=[jax.ShapeDtypeStruct((nb, 1, vdim), BF16),
                   jax.ShapeDtypeStruct((1,) + conv_state.shape[1:], F32),
                   jax.ShapeDtypeStruct((1,) + s0.shape[1:], F32)],
        scratch_shapes=[pltpu.VMEM((HEAD, HEAD), F32)],
        compiler_params=_params(1),
        name="delta_sample",
    )(proj.reshape(nb, 1, width), conv_state, conv_w, beta.reshape(nb, 1, N_V_HEADS),
      g.reshape(nb, 1, N_V_HEADS), norm_w.reshape(-1, 1, HEAD), s0)


def kernel(x_prompt, x_sample, c_prompt, c_sample, cache_pool, state_conv, state_rec, cache_ffn_conv,
           norm_w, ada_w, ada_b, pool_w, pool_scale, dn_w_in, dn_conv_w, dn_a_log, dn_dt_bias, dn_norm_w,
           dn_w_out, ffn_w_gate, ffn_w_up, ffn_conv_w, ffn_w_down):
    bp, seq, d = x_prompt.shape
    bs = x_sample.shape[0]
    key_dim = N_QK_HEADS * HEAD
    val_dim = N_V_HEADS * HEAD
    conv_dim = 2 * key_dim + val_dim
    assert bs == 128 and bp <= 8 and x_sample.shape[1] == 1
    assert ada_w.shape[0] == 2 and dn_w_in.shape[0] == 1 and pool_w.shape[0] == 1

    c_all = jnp.concatenate([c_sample, c_prompt, jnp.zeros((MOD_ROWS - bs - bp, d), F32)], axis=0)
    mods = _ada(c_all, ada_w, ada_b)

    xp = x_prompt.reshape(bp * seq, d)
    xs = x_sample.reshape(bs, d)

    xp, hp, pool_p16 = _pool_prompt(xp, mods, 0, norm_w, pool_w, pool_scale, seq)
    xs, hs, pool_st = _pool_sample(xs, mods, 0, norm_w, pool_w, pool_scale,
                                   cache_pool.transpose(0, 2, 1, 3))
    pool_p = pool_p16[:, 16 - POOL_STATE:][None]
    pool_s = pool_st.transpose(1, 0, 2)[None]

    ffn_p, ffn_s = [], []
    ffn_cache_t = cache_ffn_conv.transpose(0, 2, 1, 3)

    def ffn(layer, xp, hp, xs, hs, nxt):
        ap, stp, a_s, sts, w_down = _ffn_a(hp, hs, layer, ffn_w_gate, ffn_w_up, ffn_conv_w,
                                           ffn_w_down, ffn_cache_t, seq)
        ffn_p.append(stp[:, 6:8])
        ffn_s.append(sts.transpose(1, 0, 2))
        xp, hp = _mm_epi(ap, w_down, 0, xp, mods, norm_w, layer, 5, 3, nxt,
                         per_row=False, seq=seq, name="ffn_b_prompt")
        xs, hs = _mm_epi(a_s, w_down, 0, xs, mods, norm_w, layer, 5, 3, nxt,
                         per_row=True, seq=1, name="ffn_b_sample")
        return xp, hp, xs, hs

    xp, hp, xs, hs = ffn(0, xp, hp, xs, hs, (1, 0, 0, 1, BF16))

    n_main = conv_dim + val_dim
    w_t = jnp.swapaxes(dn_w_in, 1, 2)
    qn, st_q, pq = _proj(hp, hs, w_t, dn_conv_w, 0, key_dim, seq, "qk", HEAD ** -0.5, "proj_q")
    kn, st_k, pk = _proj(hp, hs, w_t, dn_conv_w, key_dim, key_dim, seq, "qk", 1.0, "proj_k")
    vv, st_v, pv = _proj(hp, hs, w_t, dn_conv_w, 2 * key_dim, val_dim, seq, "v", 1.0, "proj_v")
    zz, pz, w_out = _proj(hp, hs, w_t, dn_conv_w, conv_dim, val_dim, seq, "z", 1.0, "proj_z",
                          w_extra=dn_w_out)
    proj_s = jnp.concatenate([pq, pk, pv, pz], axis=1)
    beta_p, gc_p = _gates(hp, w_t, n_main, dn_a_log, dn_dt_bias, DELTA_CHUNK, "gates_prompt")
    beta_s, g_s = _gates(hs, w_t, n_main, dn_a_log, dn_dt_bias, 1, "gates_sample")
    conv_p = jnp.concatenate([st_q, st_k, st_v], axis=2)[:, 5:8][None]
    o_p, rec_p = _delta_prompt(qn, kn, vv, zz, beta_p, gc_p, dn_norm_w, bp, seq)
    o_s, conv_s, rec_s = _delta_sample(proj_s, state_conv, dn_conv_w, beta_s, g_s, dn_norm_w, state_rec)
    nxt = (1, 2, 3, 4, BF16)
    xp, hp = _mm_epi(o_p, w_out, 0, xp, mods, norm_w, 1, 2, 1, nxt,
                     per_row=False, seq=seq, name="out_prompt")
    xs, hs = _mm_epi(o_s.reshape(bs, val_dim), w_out, 0, xs, mods, norm_w, 1, 2, 1, nxt,
                     per_row=True, seq=1, name="out_sample")
    xp, _, xs, _ = ffn(1, xp, hp, xs, hs, None)

    return (xp.reshape(bp, seq, d), xs.reshape(bs, 1, d), pool_p, pool_s, conv_p, conv_s,
            rec_p, rec_s, jnp.stack(ffn_p), jnp.stack(ffn_s))
```
